```python
import math
import functools
import jax, jax.numpy as jnp
from jax import lax
import numpy as np

D_MODEL = 2048
BATCH = 4
SEQ = 4096
DEPTH = 2

CHUNK = 64
MIX_WIDTH = D_MODEL
DN_HEADS = 8
DN_HEAD_DIM = 128
DN_WIDTH = DN_HEADS * DN_HEAD_DIM
CONV_K = 4
DA_HEADS = 8
DA_HEAD_DIM = 64
DA_V_DIM = 2 * DA_HEAD_DIM
DA_WIDTH = DA_HEADS * DA_V_DIM
Q_BLOCK = 128
D_FF = 7 * D_MODEL // 2
N_EXPERTS = 8
TOP_K = 2
MOE_BLOCK = 512
PLE_DIM = 256
N_DENSE = (DEPTH + 1) // 2
N_MOE = DEPTH // 2
PROJ_SIZES = (3 * DN_WIDTH, DN_WIDTH, DN_HEADS, DN_HEADS, DA_WIDTH, DA_WIDTH, DA_WIDTH)
PROJ_WIDTH = 3 * DN_WIDTH + DN_WIDTH + 2 * DN_HEADS + 3 * DA_WIDTH
EPS = 1e-6

kernel_name = 'hybrid_deltanet_diffattn_moe_encoder'


def rms_norm(x, w):
    xf = x.astype(jnp.float32)
    y = xf * lax.rsqrt(jnp.mean(xf * xf, axis=-1, keepdims=True) + EPS)
    return (y * w.astype(jnp.float32)).astype(x.dtype)


def l2_normalize(x):
    return x * lax.rsqrt(jnp.sum(x * x, axis=-1, keepdims=True) + EPS)


def split_cols(x, sizes):
    out, start = [], 0
    for s in sizes:
        out.append(x[..., start:start + s])
        start += s
    return out


def causal_depthwise_conv(x, w):
    c = x.shape[-1]
    return lax.conv_general_dilated(
        x, w[:, None, :].astype(x.dtype), window_strides=(1,), padding=((CONV_K - 1, 0),),
        dimension_numbers=('NWC', 'WIO', 'NWC'), feature_group_count=c)


def to_chunks(a):
    b, t = a.shape[:2]
    a = a.reshape(b, t // CHUNK, CHUNK, *a.shape[2:])
    return jnp.swapaxes(jnp.moveaxis(a, 1, 0), 2, 3)


def from_chunks(a):
    a = jnp.moveaxis(jnp.swapaxes(a, 2, 3), 0, 1)
    b, n, c = a.shape[:3]
    return a.reshape(b, n * c, *a.shape[3:])


def gated_delta_rule(q, k, v, g, beta):
    f32 = jnp.float32
    b, t, h, dk = q.shape
    qc = to_chunks(q.astype(f32)) * (dk ** -0.5)
    kc = to_chunks(k.astype(f32))
    vc = to_chunks(v.astype(f32))
    bc = to_chunks(beta.astype(f32))
    gcum = jnp.cumsum(to_chunks(g.astype(f32)), axis=-1)
    idx = jnp.arange(CHUNK)
    incl = idx[:, None] >= idx[None, :]
    strict = idx[:, None] > idx[None, :]
    decay = jnp.exp(jnp.where(incl, gcum[..., :, None] - gcum[..., None, :], -jnp.inf))
    kb = kc * bc[..., None]
    m = jnp.where(strict, jnp.einsum('nbhik,nbhjk->nbhij', kb, kc) * decay, 0.0)
    a = m + jnp.eye(CHUNK, dtype=f32)
    solve = functools.partial(lax.linalg.triangular_solve, left_side=True, lower=True,
                              unit_diagonal=True)
    u = solve(a, vc * bc[..., None])
    w = solve(a, kb * jnp.exp(gcum)[..., None])
    attn = jnp.where(incl, jnp.einsum('nbhik,nbhjk->nbhij', qc, kc) * decay, 0.0)

    def step(s, xs):
        q_i, k_i, u_i, w_i, g_i, a_i = xs
        v_new = u_i - jnp.einsum('bhck,bhkv->bhcv', w_i, s)
        o_i = (jnp.einsum('bhck,bhkv->bhcv', q_i * jnp.exp(g_i)[..., None], s)
               + jnp.einsum('bhij,bhjv->bhiv', a_i, v_new))
        g_last = g_i[..., -1]
        s = (s * jnp.exp(g_last)[..., None, None]
             + jnp.einsum('bhck,bhcv->bhkv', k_i * jnp.exp(g_last[..., None] - g_i)[..., None], v_new))
        return s, o_i

    s0 = jnp.zeros((b, h, dk, v.shape[-1]), f32)
    _, o = lax.scan(step, s0, (qc, kc, u, w, gcum, attn))
    return from_chunks(o).astype(v.dtype)


def diff_attention(q, k, v, lam):
    b, t, h = q.shape[:3]
    nq = t // Q_BLOCK
    slopes = jnp.exp2(-8.0 * jnp.arange(1, h + 1, dtype=jnp.float32) / h)
    pos_k = jnp.arange(t)
    scale = DA_HEAD_DIM ** -0.5
    q_blocks = jnp.moveaxis(q.reshape(b, nq, Q_BLOCK, *q.shape[2:]), 1, 0)

    def one_block(args):
        q_blk, blk = args
        pos_q = blk * Q_BLOCK + jnp.arange(Q_BLOCK)
        s = jnp.einsum('bqhcd,bkhcd->bhcqk', q_blk, k).astype(jnp.float32) * scale
        dist = jnp.abs(pos_q[:, None] - pos_k[None, :]).astype(jnp.float32)
        bias = -slopes[:, None, None, None] * dist
        allowed = (pos_k[None, :] // CHUNK) <= (pos_q[:, None] // CHUNK)
        s = jnp.where(allowed, s + bias, -jnp.inf)
        prob = jax.nn.softmax(s, axis=-1)
        pdiff = prob[:, :, 0] - lam * prob[:, :, 1]
        return jnp.einsum('bhqk,bkhv->bqhv', pdiff.astype(v.dtype), v)

    o = lax.map(one_block, (q_blocks, jnp.arange(nq)))
    return jnp.moveaxis(o, 0, 1).reshape(b, t, h, v.shape[-1])


def swiglu(x, w1, w3, w2):
    return (jax.nn.silu(x @ w1) * (x @ w3)) @ w2


def moe_swiglu(x, w_router, w1, w3, w2):
    b, t, d = x.shape
    n = b * t
    xf = x.reshape(n, d)
    logits = (xf @ w_router).astype(jnp.float32)
    top_logit, top_idx = lax.top_k(logits, TOP_K)
    gates = jax.nn.softmax(top_logit, axis=-1)
    n_assign = n * TOP_K
    flat_e = top_idx.reshape(n_assign)
    flat_tok = jnp.repeat(jnp.arange(n, dtype=jnp.int32), TOP_K)
    flat_gate = gates.reshape(n_assign)
    order = jnp.argsort(flat_e)
    e_sorted = flat_e[order]
    counts = jax.ops.segment_sum(jnp.ones((n_assign,), jnp.int32), flat_e, num_segments=N_EXPERTS)
    padded = (counts + MOE_BLOCK - 1) // MOE_BLOCK * MOE_BLOCK
    start = jnp.cumsum(counts) - counts
    pad_end = jnp.cumsum(padded)
    pad_start = pad_end - padded
    dest = pad_start[e_sorted] + (jnp.arange(n_assign, dtype=jnp.int32) - start[e_sorted])
    n_blocks = -(-n_assign // MOE_BLOCK) + N_EXPERTS
    n_rows = n_blocks * MOE_BLOCK
    row_tok = jnp.full((n_rows,), n, jnp.int32).at[dest].set(flat_tok[order])
    row_gate = jnp.zeros((n_rows,), jnp.float32).at[dest].set(flat_gate[order])
    x_pad = jnp.concatenate([xf, jnp.zeros((1, d), xf.dtype)], axis=0)
    rows = x_pad[row_tok].reshape(n_blocks, MOE_BLOCK, d)
    block_e = jnp.minimum(
        jnp.searchsorted(pad_end, jnp.arange(n_blocks, dtype=jnp.int32) * MOE_BLOCK, side='right'),
        N_EXPERTS - 1)

    def expert_block(args):
        r, e = args
        return (jax.nn.silu(r @ w1[e]) * (r @ w3[e])) @ w2[e]

    y_rows = lax.map(expert_block, (rows, block_e)).reshape(n_rows, d)
    y_rows = y_rows * row_gate[:, None].astype(y_rows.dtype)
    y = jnp.zeros((n + 1, d), y_rows.dtype).at[row_tok].add(y_rows)
    return y[:n].reshape(b, t, d).astype(x.dtype)


def setup_inputs(seed: int = 0) -> dict:
    key = jax.random.key(seed)
    ks = jax.random.split(key, 32)
    f32 = jnp.float32

    def normal(k, shape, fan_in):
        return jax.random.normal(k, shape, f32) * (fan_in ** -0.5)

    def gain(k, shape):
        return 1.0 + 0.05 * jax.random.normal(k, shape, f32)

    x = jax.random.normal(ks[0], (BATCH, SEQ, D_MODEL), f32)
    p = jax.random.normal(ks[1], (DEPTH, BATCH, SEQ, PLE_DIM), f32)
    norm_mix = gain(ks[2], (DEPTH, D_MODEL))
    w_in = normal(ks[3], (DEPTH, D_MODEL, PROJ_WIDTH), D_MODEL)
    conv_w = normal(ks[4], (DEPTH, CONV_K, 3 * DN_WIDTH), CONV_K)
    dn_a_log = jnp.log(jax.random.uniform(ks[5], (DEPTH, DN_HEADS), f32, 1.0, 16.0))
    dt = jnp.exp(jax.random.uniform(ks[6], (DEPTH, DN_HEADS), f32, math.log(1e-3), math.log(1e-1)))
    dn_dt_bias = dt + jnp.log(-jnp.expm1(-dt))
    dn_norm = gain(ks[7], (DEPTH, DN_HEAD_DIM))
    da_lambda_q1 = 0.1 * jax.random.normal(ks[8], (DEPTH, DA_HEAD_DIM), f32)
    da_lambda_k1 = 0.1 * jax.random.normal(ks[9], (DEPTH, DA_HEAD_DIM), f32)
    da_lambda_q2 = 0.1 * jax.random.normal(ks[10], (DEPTH, DA_HEAD_DIM), f32)
    da_lambda_k2 = 0.1 * jax.random.normal(ks[11], (DEPTH, DA_HEAD_DIM), f32)
    da_norm = gain(ks[12], (DEPTH, DA_V_DIM))
    w_out = normal(ks[13], (DEPTH, MIX_WIDTH, D_MODEL), MIX_WIDTH)
    norm_ffn = gain(ks[14], (DEPTH, D_MODEL))
    ffn_w1 = normal(ks[15], (N_DENSE, D_MODEL, D_FF), D_MODEL)
    ffn_w3 = normal(ks[16], (N_DENSE, D_MODEL, D_FF), D_MODEL)
    ffn_w2 = normal(ks[17], (N_DENSE, D_FF, D_MODEL), D_FF)
    moe_router = normal(ks[18], (N_MOE, D_MODEL, N_EXPERTS), D_MODEL)
    moe_w1 = normal(ks[19], (N_MOE, N_EXPERTS, D_MODEL, D_FF), D_MODEL)
    moe_w3 = normal(ks[20], (N_MOE, N_EXPERTS, D_MODEL, D_FF), D_MODEL)
    moe_w2 = normal(ks[21], (N_MOE, N_EXPERTS, D_FF, D_MODEL), D_FF)
    ple_proj = normal(ks[22], (DEPTH, PLE_DIM, D_MODEL), PLE_DIM)
    ple_norm = gain(ks[23], (DEPTH, D_MODEL))
    ple_gate = normal(ks[24], (DEPTH, D_MODEL, D_MODEL), D_MODEL)
    norm_final = gain(ks[25], (D_MODEL,))
    return {'x': x, 'p': p, 'norm_mix': norm_mix, 'w_in': w_in, 'conv_w': conv_w,
            'dn_a_log': dn_a_log, 'dn_dt_bias': dn_dt_bias, 'dn_norm': dn_norm,
            'da_lambda_q1': da_lambda_q1, 'da_lambda_k1': da_lambda_k1,
            'da_lambda_q2': da_lambda_q2, 'da_lambda_k2': da_lambda_k2, 'da_norm': da_norm,
            'w_out': w_out, 'norm_ffn': norm_ffn, 'ffn_w1': ffn_w1, 'ffn_w3': ffn_w3,
            'ffn_w2': ffn_w2, 'moe_router': moe_router, 'moe_w1': moe_w1, 'moe_w3': moe_w3,
            'moe_w2': moe_w2, 'ple_proj': ple_proj, 'ple_norm': ple_norm, 'ple_gate': ple_gate,
            'norm_final': norm_final}


def reference(x, p, norm_mix, w_in, conv_w, dn_a_log, dn_dt_bias, dn_norm,
              da_lambda_q1, da_lambda_k1, da_lambda_q2, da_lambda_k2, da_norm,
              w_out, norm_ffn, ffn_w1, ffn_w3, ffn_w2, moe_router, moe_w1, moe_w3, moe_w2,
              ple_proj, ple_norm, ple_gate, norm_final):
    f32 = jnp.float32
    b, t, _ = x.shape
    h = x
    for i in range(DEPTH):
        hn = rms_norm(h, norm_mix[i])
        proj = hn @ w_in[i]
        dqkv, dgate, dbeta, dalpha, aq, ak, av = split_cols(proj, PROJ_SIZES)

        qkv = jax.nn.silu(causal_depthwise_conv(dqkv, conv_w[i]))
        q, k, v = [z.reshape(b, t, DN_HEADS, DN_HEAD_DIM) for z in split_cols(qkv, (DN_WIDTH,) * 3)]
        q = l2_normalize(q.astype(f32))
        k = l2_normalize(k.astype(f32))
        beta = jax.nn.sigmoid(dbeta.astype(f32))
        g = -jnp.exp(dn_a_log[i].astype(f32)) * jax.nn.softplus(
            dalpha.astype(f32) + dn_dt_bias[i].astype(f32))
        o_dn = gated_delta_rule(q, k, v, g, beta)
        o_dn = rms_norm(o_dn, dn_norm[i]) * jax.nn.silu(dgate.reshape(b, t, DN_HEADS, DN_HEAD_DIM))

        lam_init = 0.8 - 0.6 * math.exp(-0.3 * i)
        lam = (jnp.exp(jnp.sum(da_lambda_q1[i].astype(f32) * da_lambda_k1[i].astype(f32)))
               - jnp.exp(jnp.sum(da_lambda_q2[i].astype(f32) * da_lambda_k2[i].astype(f32)))
               + lam_init)
        qa = aq.reshape(b, t, DA_HEADS, 2, DA_HEAD_DIM)
        ka = ak.reshape(b, t, DA_HEADS, 2, DA_HEAD_DIM)
        va = av.reshape(b, t, DA_HEADS, DA_V_DIM)
        o_da = diff_attention(qa, ka, va, lam)
        o_da = rms_norm(o_da, da_norm[i]) * (1.0 - lam_init)

        mixed = jnp.concatenate([o_dn.reshape(b, t, DN_WIDTH).astype(h.dtype),
                                 o_da.reshape(b, t, DA_WIDTH).astype(h.dtype)], axis=-1)
        h = h + mixed @ w_out[i]

        hn = rms_norm(h, norm_ffn[i])
        j = i // 2
        if i % 2 == 0:
            ffn = swiglu(hn, ffn_w1[j], ffn_w3[j], ffn_w2[j])
        else:
            ffn = moe_swiglu(hn, moe_router[j], moe_w1[j], moe_w3[j], moe_w2[j])
        h = h + ffn

        e = rms_norm(p[i] @ ple_proj[i], ple_norm[i])
        h = h + jax.nn.sigmoid(h @ ple_gate[i]) * e
    return rms_norm(h, norm_final)
```

```python
import functools
import math

import jax
import jax.numpy as jnp
from jax import lax
from jax.experimental import pallas as pl
from jax.experimental.pallas import tpu as pltpu

F32 = jnp.float32
BF16 = jnp.bfloat16
EPS = 1e-6

CHUNK = 64
DN_HEADS = 8
DN_HEAD_DIM = 128
DN_WIDTH = DN_HEADS * DN_HEAD_DIM
CONV_K = 4
DA_HEADS = 8
DA_HEAD_DIM = 64
DA_V_DIM = 2 * DA_HEAD_DIM
DA_WIDTH = DA_HEADS * DA_V_DIM
N_EXPERTS = 8
TOP_K = 2
LANES = 128
SUBLANES = 8
VMEM_LIMIT = 56 * 1024 * 1024

PROJ_TM = 512
FFN_TM = 512
FFN_TF = 512
MOE_TM = 512
ATT_TQ = 256
ROW_TM = 512
HALO = SUBLANES


def _cparams(sem):
    return pltpu.CompilerParams(dimension_semantics=sem, vmem_limit_bytes=VMEM_LIMIT)


def _sigmoid(x):
    return 1.0 / (1.0 + jnp.exp(-x))


def _silu(x):
    return x * _sigmoid(x)


def _softplus(x):
    return jnp.maximum(x, 0.0) + jnp.log1p(jnp.exp(-jnp.abs(x)))


def _rms(x, gain):
    y = x * lax.rsqrt(jnp.mean(x * x, axis=-1, keepdims=True) + EPS)
    return y * gain


def _mm(a, b):
    return jnp.dot(a.astype(BF16), b.astype(BF16), preferred_element_type=F32)


def _mm_nt(a, b):
    return lax.dot_general(a.astype(BF16), b.astype(BF16), (((1,), (1,)), ((), ())),
                           preferred_element_type=F32)


def _mm_tn(a, b):
    return lax.dot_general(a.astype(BF16), b.astype(BF16), (((0,), (0,)), ((), ())),
                           preferred_element_type=F32)


def _split3(a):
    a1 = a.astype(BF16)
    r1 = a - a1.astype(F32)
    a2 = r1.astype(BF16)
    a3 = (r1 - a2.astype(F32)).astype(BF16)
    return a1, a2, a3


def _norm_matmul_body(x_ref, g_ref, w_ref, o_ref, xn_ref):
    @pl.when(pl.program_id(1) == 0)
    def _():
        xn_ref[...] = _rms(x_ref[...], g_ref[...]).astype(BF16)

    o_ref[...] = jnp.dot(xn_ref[...], w_ref[...], preferred_element_type=F32).astype(o_ref.dtype)


def _norm_matmul(x, gain, w, out_dtype, tn):
    n, d = x.shape
    nout = w.shape[1]
    tm = min(PROJ_TM, n)
    return pl.pallas_call(
        _norm_matmul_body,
        grid=(n // tm, nout // tn),
        in_specs=[pl.BlockSpec((tm, d), lambda i, j: (i, 0)),
                  pl.BlockSpec((1, d), lambda i, j: (0, 0)),
                  pl.BlockSpec((d, tn), lambda i, j: (0, j))],
        out_specs=pl.BlockSpec((tm, tn), lambda i, j: (i, j)),
        out_shape=jax.ShapeDtypeStruct((n, nout), out_dtype),
        scratch_shapes=[pltpu.VMEM((tm, d), BF16)],
        compiler_params=_cparams(("parallel", "arbitrary")),
    )(x, gain.reshape(1, d), w)


def _unit_lower_inverse(m, r, c):
    eye = (r == c).astype(F32)
    bd16 = (r >> 4) == (c >> 4)
    x = jnp.where(bd16, -m, 0.0)
    p = eye + x
    x2 = _mm(x, x)
    p = p + _mm(p, x2)
    x4 = _mm(x2, x2)
    p = p + _mm(p, x4)
    x8 = _mm(x4, x4)
    p = p + _mm(p, x8)
    l32 = jnp.where(((r >> 5) == (c >> 5)) & ((r >> 4) > (c >> 4)), m, 0.0)
    p = p - _mm(_mm(p, l32), p)
    l64 = jnp.where((r >> 5) > (c >> 5), m, 0.0)
    return p - _mm(_mm(p, l64), p)


def _deltanet_body(qkv_ref, gate_ref, ba_ref, convw_ref, hp_ref, dnorm_ref, o_ref, ext_ref, s_ref):
    ci = pl.program_id(1)

    @pl.when(ci == 0)
    def _():
        ext_ref[0:HALO, :] = jnp.zeros((HALO, 3 * DN_WIDTH), F32)
        s_ref[...] = jnp.zeros_like(s_ref)

    ext_ref[HALO:HALO + CHUNK, :] = qkv_ref[...]
    conv = convw_ref[0:1, :] * ext_ref[HALO - 3:HALO - 3 + CHUNK, :]
    for j in range(1, CONV_K):
        conv = conv + convw_ref[j:j + 1, :] * ext_ref[HALO - 3 + j:HALO - 3 + j + CHUNK, :]
    ext_ref[0:HALO, :] = ext_ref[CHUNK:CHUNK + HALO, :]
    act = _silu(conv)

    ba = ba_ref[...]
    beta_all = _sigmoid(ba)
    g_all = -jnp.exp(hp_ref[1:2, :]) * _softplus(ba + hp_ref[0:1, :])
    r = lax.broadcasted_iota(jnp.int32, (CHUNK, CHUNK), 0)
    c = lax.broadcasted_iota(jnp.int32, (CHUNK, CHUNK), 1)
    incl = r >= c
    strict = r > c
    tri = incl.astype(BF16)
    g1, g2, g3 = _split3(g_all)
    gcum_all = (jnp.dot(tri, g1, preferred_element_type=F32)
                + jnp.dot(tri, g2, preferred_element_type=F32)
                + jnp.dot(tri, g3, preferred_element_type=F32))
    gcum_t = gcum_all.T
    dnorm = dnorm_ref[...]
    scale = DN_HEAD_DIM ** -0.5

    for h in range(DN_HEADS):
        lo = h * DN_HEAD_DIM
        q = act[:, lo:lo + DN_HEAD_DIM]
        k = act[:, DN_WIDTH + lo:DN_WIDTH + lo + DN_HEAD_DIM]
        v = act[:, 2 * DN_WIDTH + lo:2 * DN_WIDTH + lo + DN_HEAD_DIM]
        q = q * lax.rsqrt(jnp.sum(q * q, axis=-1, keepdims=True) + EPS)
        k = k * lax.rsqrt(jnp.sum(k * k, axis=-1, keepdims=True) + EPS)
        beta = beta_all[:, h:h + 1]
        gc = gcum_all[:, DN_HEADS + h:DN_HEADS + h + 1]
        gr = gcum_t[DN_HEADS + h:DN_HEADS + h + 1, :]
        g_last = gcum_all[CHUNK - 1:CHUNK, DN_HEADS + h:DN_HEADS + h + 1]
        decay = jnp.where(incl, jnp.exp(jnp.where(incl, gc - gr, 0.0)), 0.0)
        qs = q * scale
        kb = k * beta
        m = jnp.where(strict, _mm_nt(kb, k) * decay, 0.0)
        t = _unit_lower_inverse(m, r, c)
        u = _mm(t, v * beta)
        w = _mm(t, kb * jnp.exp(gc))
        attn = jnp.where(incl, _mm_nt(qs, k) * decay, 0.0)
        s = s_ref[h]
        v_new = u - _mm(w, s)
        o = _mm(qs * jnp.exp(gc), s) + _mm(attn, v_new)
        s_ref[h] = s * jnp.exp(g_last) + _mm_tn(k * jnp.exp(g_last - gc), v_new)
        gt = gate_ref[:, lo:lo + DN_HEAD_DIM]
        o_ref[:, lo:lo + DN_HEAD_DIM] = (_rms(o, dnorm) * _silu(gt)).astype(o_ref.dtype)


def _deltanet(proj_a, conv_w, dt_bias, a_log, dn_norm, batch, seq):
    n = batch * seq
    nc = seq // CHUNK
    wq = 3 * DN_WIDTH
    hp = jnp.zeros((SUBLANES, LANES), F32)
    hp = hp.at[0, DN_HEADS:2 * DN_HEADS].set(dt_bias).at[1, DN_HEADS:2 * DN_HEADS].set(a_log)
    row = lambda b, ci: b * nc + ci
    return pl.pallas_call(
        _deltanet_body,
        grid=(batch, nc),
        in_specs=[pl.BlockSpec((CHUNK, wq), lambda b, ci: (row(b, ci), 0)),
                  pl.BlockSpec((CHUNK, DN_WIDTH), lambda b, ci: (row(b, ci), wq // DN_WIDTH)),
                  pl.BlockSpec((CHUNK, LANES), lambda b, ci: (row(b, ci), (wq + DN_WIDTH) // LANES)),
                  pl.BlockSpec((CONV_K, wq), lambda b, ci: (0, 0)),
                  pl.BlockSpec((SUBLANES, LANES), lambda b, ci: (0, 0)),
                  pl.BlockSpec((1, DN_HEAD_DIM), lambda b, ci: (0, 0))],
        out_specs=pl.BlockSpec((CHUNK, DN_WIDTH), lambda b, ci: (row(b, ci), 0)),
        out_shape=jax.ShapeDtypeStruct((n, DN_WIDTH), BF16),
        scratch_shapes=[pltpu.VMEM((CHUNK + HALO, wq), F32),
                        pltpu.VMEM((DN_HEADS, DN_HEAD_DIM, DN_HEAD_DIM), F32)],
        compiler_params=_cparams(("parallel", "arbitrary")),
    )(proj_a, proj_a, proj_a, conv_w, hp, dn_norm.reshape(1, DN_HEAD_DIM))


def _diffattn_body(slopes_ref, q_ref, k_ref, v_ref, lamv_ref, danorm_ref, o_ref,
                   m_ref, l_ref, acc_ref, *, tq, lam_init):
    h = pl.program_id(1)
    qi = pl.program_id(2)
    slope = slopes_ref[h]
    q = q_ref[...]
    lane = lax.broadcasted_iota(jnp.int32, q.shape, 1)
    zero = jnp.zeros_like(q)
    qq = jnp.concatenate([jnp.where(lane < DA_HEAD_DIM, q, zero),
                          jnp.where(lane >= DA_HEAD_DIM, q, zero)], axis=0)
    r = lax.broadcasted_iota(jnp.int32, (2 * tq, tq), 0) & (tq - 1)
    c = lax.broadcasted_iota(jnp.int32, (2 * tq, tq), 1)
    rc = (r - c).astype(F32)

    start = pl.multiple_of(qi * tq, tq)
    s = _mm_nt(qq, k_ref[pl.ds(start, tq), :])
    s = jnp.where((c >> 6) <= (r >> 6), s - slope * jnp.abs(rc), -jnp.inf)
    m0 = jnp.max(s, axis=-1, keepdims=True)
    p = jnp.exp(s - m0)
    m_ref[...] = m0
    l_ref[...] = jnp.sum(p, axis=-1, keepdims=True)
    acc_ref[...] = jnp.dot(p.astype(BF16), v_ref[pl.ds(start, tq), :], preferred_element_type=F32)

    nrc = -slope * rc

    def body(kj, carry):
        ks = pl.multiple_of(kj * tq, tq)
        off = ((qi - kj) * tq).astype(F32)
        sb = _mm_nt(qq, k_ref[pl.ds(ks, tq), :]) + (nrc - slope * off)
        m_old = m_ref[...]
        m_new = jnp.maximum(m_old, jnp.max(sb, axis=-1, keepdims=True))
        a = jnp.exp(m_old - m_new)
        pb = jnp.exp(sb - m_new)
        l_ref[...] = a * l_ref[...] + jnp.sum(pb, axis=-1, keepdims=True)
        acc_ref[...] = a * acc_ref[...] + jnp.dot(pb.astype(BF16), v_ref[pl.ds(ks, tq), :],
                                                  preferred_element_type=F32)
        m_ref[...] = m_new
        return carry

    lax.fori_loop(0, qi, body, 0)

    lv = lamv_ref[...]
    lam = (jnp.exp(jnp.sum(lv[0:1] * lv[1:2], axis=-1, keepdims=True))
           - jnp.exp(jnp.sum(lv[2:3] * lv[3:4], axis=-1, keepdims=True)) + lam_init)
    o1 = acc_ref[0:tq, :] / l_ref[0:tq, :]
    o2 = acc_ref[tq:2 * tq, :] / l_ref[tq:2 * tq, :]
    o = o1 - lam * o2
    o_ref[...] = (_rms(o, danorm_ref[...]) * (1.0 - lam_init)).astype(o_ref.dtype)


def _diffattn(proj_b, lam_vecs, da_norm, lam_init, batch, seq):
    n = batch * seq
    tq = min(ATT_TQ, seq)
    nq = seq // tq
    slopes = jnp.exp2(-8.0 * jnp.arange(1, DA_HEADS + 1, dtype=F32) / DA_HEADS)
    grid_spec = pltpu.PrefetchScalarGridSpec(
        num_scalar_prefetch=1,
        grid=(batch, DA_HEADS, nq),
        in_specs=[pl.BlockSpec((tq, DA_V_DIM), lambda b, h, qi, sl: (b * nq + qi, h)),
                  pl.BlockSpec((seq, DA_V_DIM), lambda b, h, qi, sl: (b, DA_HEADS + h)),
                  pl.BlockSpec((seq, DA_V_DIM), lambda b, h, qi, sl: (b, 2 * DA_HEADS + h)),
                  pl.BlockSpec((4, DA_HEAD_DIM), lambda b, h, qi, sl: (0, 0)),
                  pl.BlockSpec((1, DA_V_DIM), lambda b, h, qi, sl: (0, 0))],
        out_specs=pl.BlockSpec((tq, DA_V_DIM), lambda b, h, qi, sl: (b * nq + qi, h)),
        scratch_shapes=[pltpu.VMEM((2 * tq, 1), F32), pltpu.VMEM((2 * tq, 1), F32),
                        pltpu.VMEM((2 * tq, DA_V_DIM), F32)],
    )
    return pl.pallas_call(
        functools.partial(_diffattn_body, tq=tq, lam_init=lam_init),
        grid_spec=grid_spec,
        out_shape=jax.ShapeDtypeStruct((n, DA_WIDTH), BF16),
        compiler_params=_cparams(("parallel", "parallel", "arbitrary")),
    )(slopes, proj_b, proj_b, proj_b, lam_vecs, da_norm.reshape(1, DA_V_DIM))


def _outproj_body(h_ref, odn_ref, oda_ref, w_ref, o_ref):
    o_ref[...] = (h_ref[...]
                  + jnp.dot(odn_ref[...], w_ref[0:DN_WIDTH, :], preferred_element_type=F32)
                  + jnp.dot(oda_ref[...], w_ref[DN_WIDTH:DN_WIDTH + DA_WIDTH, :],
                            preferred_element_type=F32))


def _outproj(h, o_dn, o_da, w_out):
    n, d = h.shape
    tm = min(ROW_TM, n)
    return pl.pallas_call(
        _outproj_body,
        grid=(n // tm,),
        in_specs=[pl.BlockSpec((tm, d), lambda i: (i, 0)),
                  pl.BlockSpec((tm, DN_WIDTH), lambda i: (i, 0)),
                  pl.BlockSpec((tm, DA_WIDTH), lambda i: (i, 0)),
                  pl.BlockSpec((DN_WIDTH + DA_WIDTH, d), lambda i: (0, 0))],
        out_specs=pl.BlockSpec((tm, d), lambda i: (i, 0)),
        out_shape=jax.ShapeDtypeStruct((n, d), F32),
        compiler_params=_cparams(("parallel",)),
    )(h, o_dn, o_da, w_out)


def _ffn_body(x_ref, g_ref, w1_ref, w3_ref, w2_ref, o_ref, xn_ref):
    @pl.when(pl.program_id(1) == 0)
    def _():
        x = x_ref[...]
        xn_ref[...] = _rms(x, g_ref[...]).astype(BF16)
        o_ref[...] = x

    xn = xn_ref[...]
    g = jnp.dot(xn, w1_ref[...], preferred_element_type=F32)
    u = jnp.dot(xn, w3_ref[...], preferred_element_type=F32)
    a = (_silu(g) * u).astype(BF16)
    o_ref[...] += jnp.dot(a, w2_ref[...], preferred_element_type=F32)


def _ffn(x, gain, w1, w3, w2):
    n, d = x.shape
    dff = w1.shape[1]
    tm = min(FFN_TM, n)
    tf = FFN_TF
    return pl.pallas_call(
        _ffn_body,
        grid=(n // tm, dff // tf),
        in_specs=[pl.BlockSpec((tm, d), lambda i, f: (i, 0)),
                  pl.BlockSpec((1, d), lambda i, f: (0, 0)),
                  pl.BlockSpec((d, tf), lambda i, f: (0, f)),
                  pl.BlockSpec((d, tf), lambda i, f: (0, f)),
                  pl.BlockSpec((tf, d), lambda i, f: (f, 0))],
        out_specs=pl.BlockSpec((tm, d), lambda i, f: (i, 0)),
        out_shape=jax.ShapeDtypeStruct((n, d), F32),
        scratch_shapes=[pltpu.VMEM((tm, d), BF16)],
        compiler_params=_cparams(("parallel", "arbitrary")),
    )(x, gain.reshape(1, d), w1, w3, w2)


def _router_body(x_ref, g_ref, wt_ref, idx_ref, gate_ref):
    xn = _rms(x_ref[...], g_ref[...])
    x1, x2, x3 = _split3(xn)
    w1, w2, w3 = _split3(wt_ref[...])
    nt = lambda a, b: lax.dot_general(a, b, (((1,), (1,)), ((), ())), preferred_element_type=F32)
    logits = (nt(w3, x1) + nt(w2, x2) + nt(w1, x3)) + (nt(w2, x1) + nt(w1, x2)) + nt(w1, x1)
    e = lax.broadcasted_iota(jnp.int32, logits.shape, 0)
    m0 = jnp.max(logits, axis=0, keepdims=True)
    i0 = jnp.min(jnp.where(logits == m0, e, N_EXPERTS), axis=0, keepdims=True)
    rest = jnp.where(e == i0, -jnp.inf, logits)
    m1 = jnp.max(rest, axis=0, keepdims=True)
    i1 = jnp.min(jnp.where(rest == m1, e, N_EXPERTS), axis=0, keepdims=True)
    t = jnp.exp(m1 - m0)
    idx_ref[0:1, :] = i0
    idx_ref[1:2, :] = i1
    gate_ref[0:1, :] = 1.0 / (1.0 + t)
    gate_ref[1:2, :] = t / (1.0 + t)


def _router(x, gain, w_router):
    n, d = x.shape
    tm = min(ROW_TM, n)
    return pl.pallas_call(
        _router_body,
        grid=(n // tm,),
        in_specs=[pl.BlockSpec((tm, d), lambda i: (i, 0)),
                  pl.BlockSpec((1, d), lambda i: (0, 0)),
                  pl.BlockSpec((N_EXPERTS, d), lambda i: (0, 0))],
        out_specs=[pl.BlockSpec((TOP_K, tm), lambda i: (0, i)),
                   pl.BlockSpec((TOP_K, tm), lambda i: (0, i))],
        out_shape=[jax.ShapeDtypeStruct((TOP_K, n), jnp.int32),
                   jax.ShapeDtypeStruct((TOP_K, n), F32)],
        compiler_params=_cparams(("parallel",)),
    )(x, gain.reshape(1, d), w_router.T)


def _gather_rows(src_hbm, dst_ref, sem, n_rows, row_index):
    def copy(i):
        return pltpu.make_async_copy(src_hbm.at[pl.ds(row_index(i), 1), :],
                                     dst_ref.at[pl.ds(i, 1), :], sem)

    def start(i, carry):
        copy(i).start()
        return carry

    def wait(i, carry):
        copy(i).wait()
        return carry

    lax.fori_loop(0, n_rows, start, 0)
    lax.fori_loop(0, n_rows, wait, 0)


def _experts_body(be_ref, tok_ref, nb_ref, x_hbm, g_ref, w1_ref, w3_ref, w2_ref, o_ref,
                  xg_ref, xn_ref, sem, *, tm):
    b = pl.program_id(0)
    f = pl.program_id(1)
    used = b < nb_ref[0]

    @pl.when(jnp.logical_and(f == 0, used))
    def _():
        _gather_rows(x_hbm, xg_ref, sem, tm, lambda i: tok_ref[b * tm + i])
        xn_ref[...] = _rms(xg_ref[...], g_ref[...]).astype(BF16)

    @pl.when(f == 0)
    def _():
        o_ref[...] = jnp.zeros_like(o_ref)

    @pl.when(used)
    def _():
        xn = xn_ref[...]
        g = jnp.dot(xn, w1_ref[...], preferred_element_type=F32)
        u = jnp.dot(xn, w3_ref[...], preferred_element_type=F32)
        a = (_silu(g) * u).astype(BF16)
        o_ref[...] += jnp.dot(a, w2_ref[...], preferred_element_type=F32)


def _experts(x, gain, w1, w3, w2, block_e, row_tok, n_used, n_blocks, tm):
    n, d = x.shape
    dff = w1.shape[2]
    tf = FFN_TF
    nf = dff // tf

    def widx(b, f, be, tok, nb):
        live = b < nb[0]
        last = jnp.maximum(nb[0] - 1, 0)
        return jnp.where(live, be[b], be[last]), jnp.where(live, f, nf - 1)

    def w13_map(b, f, be, tok, nb):
        e, ff = widx(b, f, be, tok, nb)
        return e, 0, ff

    def w2_map(b, f, be, tok, nb):
        e, ff = widx(b, f, be, tok, nb)
        return e, ff, 0

    grid_spec = pltpu.PrefetchScalarGridSpec(
        num_scalar_prefetch=3,
        grid=(n_blocks, nf),
        in_specs=[pl.BlockSpec(memory_space=pl.ANY),
                  pl.BlockSpec((1, d), lambda b, f, be, tok, nb: (0, 0)),
                  pl.BlockSpec((None, d, tf), w13_map),
                  pl.BlockSpec((None, d, tf), w13_map),
                  pl.BlockSpec((None, tf, d), w2_map)],
        out_specs=pl.BlockSpec((tm, d), lambda b, f, be, tok, nb: (b, 0)),
        scratch_shapes=[pltpu.VMEM((tm, d), F32), pltpu.VMEM((tm, d), BF16),
                        pltpu.SemaphoreType.DMA(())],
    )
    return pl.pallas_call(
        functools.partial(_experts_body, tm=tm),
        grid_spec=grid_spec,
        out_shape=jax.ShapeDtypeStruct((n_blocks * tm, d), F32),
        compiler_params=_cparams(("arbitrary", "arbitrary")),
    )(block_e, row_tok, n_used, x, gain.reshape(1, d), w1, w3, w2)


def _route_plan(idx, n, tm):
    n_assign = n * TOP_K
    flat_e = idx.T.reshape(n_assign)
    onehot = (flat_e[:, None] == jnp.arange(N_EXPERTS, dtype=jnp.int32)[None, :]).astype(jnp.int32)
    csum = jnp.cumsum(onehot, axis=0)
    counts = csum[-1]
    rank = jnp.sum((csum - onehot) * onehot, axis=1)
    padded = (counts + tm - 1) // tm * tm
    pad_end = jnp.cumsum(padded)
    pad_start = pad_end - padded
    dest = pad_start[flat_e] + rank
    n_blocks = -(-n_assign // tm) + N_EXPERTS
    n_rows = n_blocks * tm
    flat_tok = jnp.arange(n_assign, dtype=jnp.int32) // TOP_K
    row_tok = jnp.zeros((n_rows,), jnp.int32).at[dest].set(flat_tok)
    block_e = jnp.minimum(
        jnp.searchsorted(pad_end, jnp.arange(n_blocks, dtype=jnp.int32) * tm, side='right'),
        N_EXPERTS - 1).astype(jnp.int32)
    n_used = (pad_end[-1] // tm).astype(jnp.int32).reshape(1)
    return dest.reshape(n, TOP_K).astype(jnp.int32), row_tok, block_e, n_used, n_blocks


def _ple_tail(h, p_ref, wp_ref, pn_ref, wg_ref, nf_ref, o_ref, final):
    e = _rms(_mm(p_ref[...], wp_ref[...]), pn_ref[...])
    out = h + _sigmoid(_mm(h, wg_ref[...])) * e
    if final:
        out = _rms(out, nf_ref[...])
    o_ref[...] = out


def _ple_body(h_ref, p_ref, wp_ref, pn_ref, wg_ref, nf_ref, o_ref, *, final):
    _ple_tail(h_ref[...], p_ref, wp_ref, pn_ref, wg_ref, nf_ref, o_ref, final)


def _ple_combine_body(pos_ref, h_ref, y_hbm, g0_ref, g1_ref, p_ref, wp_ref, pn_ref, wg_ref, nf_ref,
                      o_ref, y0_ref, y1_ref, sem, *, tm, final):
    i = pl.program_id(0)
    _gather_rows(y_hbm, y0_ref, sem, tm, lambda t: pos_ref[(i * tm + t) * TOP_K])
    _gather_rows(y_hbm, y1_ref, sem, tm, lambda t: pos_ref[(i * tm + t) * TOP_K + 1])
    h = h_ref[...] + (y0_ref[...] * g0_ref[...] + y1_ref[...] * g1_ref[...])
    _ple_tail(h, p_ref, wp_ref, pn_ref, wg_ref, nf_ref, o_ref, final)


def _ple(h, p, w_proj, p_norm, w_gate, norm_final, final, moe=None):
    n, d = h.shape
    dp = p.shape[1]
    tm = min(ROW_TM // 2, n) if moe is not None else min(ROW_TM, n)
    row = lambda i, *_: (i, 0)
    fixed = lambda i, *_: (0, 0)
    tail_specs = [pl.BlockSpec((tm, dp), row), pl.BlockSpec((dp, d), fixed),
                  pl.BlockSpec((1, d), fixed), pl.BlockSpec((d, d), fixed),
                  pl.BlockSpec((1, d), fixed)]
    tail_args = (p, w_proj, p_norm.reshape(1, d), w_gate, norm_final.reshape(1, d))
    out_shape = jax.ShapeDtypeStruct((n, d), F32)
    if moe is None:
        return pl.pallas_call(
            functools.partial(_ple_body, final=final),
            grid=(n // tm,),
            in_specs=[pl.BlockSpec((tm, d), row)] + tail_specs,
            out_specs=pl.BlockSpec((tm, d), row),
            out_shape=out_shape,
            compiler_params=_cparams(("parallel",)),
        )(h, *tail_args)
    y_rows, pos, gates = moe
    grid_spec = pltpu.PrefetchScalarGridSpec(
        num_scalar_prefetch=1,
        grid=(n // tm,),
        in_specs=[pl.BlockSpec((tm, d), row), pl.BlockSpec(memory_space=pl.ANY),
                  pl.BlockSpec((tm, 1), row), pl.BlockSpec((tm, 1), row)] + tail_specs,
        out_specs=pl.BlockSpec((tm, d), row),
        scratch_shapes=[pltpu.VMEM((tm, d), F32), pltpu.VMEM((tm, d), F32),
                        pltpu.SemaphoreType.DMA(())],
    )
    return pl.pallas_call(
        functools.partial(_ple_combine_body, tm=tm, final=final),
        grid_spec=grid_spec,
        out_shape=out_shape,
        compiler_params=_cparams(("arbitrary",)),
    )(pos.reshape(n * TOP_K), h, y_rows, gates[0].reshape(n, 1), gates[1].reshape(n, 1), *tail_args)


def kernel(x, p, norm_mix, w_in, conv_w, dn_a_log, dn_dt_bias, dn_norm, da_lambda_q1, da_lambda_k1, da_lambda_q2, da_lambda_k2, da_norm, w_out, norm_ffn, ffn_w1, ffn_w3, ffn_w2, moe_router, moe_w1, moe_w3, moe_w2, ple_proj, ple_norm, ple_gate, norm_final):
    batch, seq, d = x.shape
    n = batch * seq
    depth = w_in.shape[0]
    h = x.reshape(n, d)
    wq = 3 * DN_WIDTH
    for i in range(depth):
        wi = w_in[i]
        c0 = wq + DN_WIDTH
        c1 = c0 + 2 * DN_HEADS
        pad = jnp.zeros((d, LANES - 2 * DN_HEADS), F32)
        w_a = jnp.concatenate([wi[:, :c1], pad], axis=1).astype(BF16)
        att_scale = DA_HEAD_DIM ** -0.5
        w_b = jnp.concatenate([wi[:, c1:c1 + DA_WIDTH] * att_scale, wi[:, c1 + DA_WIDTH:]],
                              axis=1).astype(BF16)
        proj_a = _norm_matmul(h, norm_mix[i], w_a, F32, (c0 + LANES) // 3)
        proj_b = _norm_matmul(h, norm_mix[i], w_b, BF16, DA_WIDTH)

        o_dn = _deltanet(proj_a, conv_w[i], dn_dt_bias[i], dn_a_log[i], dn_norm[i], batch, seq)
        lam_init = 0.8 - 0.6 * math.exp(-0.3 * i)
        lam_vecs = jnp.stack([da_lambda_q1[i], da_lambda_k1[i], da_lambda_q2[i], da_lambda_k2[i]])
        o_da = _diffattn(proj_b, lam_vecs, da_norm[i], lam_init, batch, seq)
        h = _outproj(h, o_dn, o_da, w_out[i].astype(BF16))

        j = i // 2
        last = i == depth - 1
        wp = ple_proj[i].astype(BF16)
        wg = ple_gate[i].astype(BF16)
        if i % 2 == 0:
            h = _ffn(h, norm_ffn[i], ffn_w1[j].astype(BF16), ffn_w3[j].astype(BF16),
                     ffn_w2[j].astype(BF16))
            h = _ple(h, p[i].reshape(n, -1), wp, ple_norm[i], wg, norm_final, last)
        else:
            idx, gates = _router(h, norm_ffn[i], moe_router[j])
            tm = min(MOE_TM, n)
            pos, row_tok, block_e, n_used, n_blocks = _route_plan(idx, n, tm)
            y_rows = _experts(h, norm_ffn[i], moe_w1[j].astype(BF16), moe_w3[j].astype(BF16),
                              moe_w2[j].astype(BF16), block_e, row_tok, n_used, n_blocks, tm)
            h = _ple(h, p[i].reshape(n, -1), wp, ple_norm[i], wg, norm_final, last,
                     moe=(y_rows, pos, gates))
    return h.reshape(batch, seq, d)
```

```python
import functools
import math

import jax
import jax.numpy as jnp
from jax import lax
from jax.experimental import pallas as pl
from jax.experimental.pallas import tpu as pltpu

F32 = jnp.float32
BF16 = jnp.bfloat16
EPS = 1e-6

CHUNK = 64
DN_HEADS = 8
DN_HEAD_DIM = 128
DN_WIDTH = DN_HEADS * DN_HEAD_DIM
CONV_K = 4
DA_HEADS = 8
DA_HEAD_DIM = 64
DA_V_DIM = 2 * DA_HEAD_DIM
DA_WIDTH = DA_HEADS * DA_V_DIM
N_EXPERTS = 8
TOP_K = 2
LANES = 128
SUBLANES = 8
VMEM_LIMIT = 56 * 1024 * 1024

PROJ_TM = 512
FFN_TM = 512
FFN_TF = 512
MOE_TM = 512
ATT_TQ = 256
ROW_TM = 512
HALO = SUBLANES


def _cparams(sem):
    return pltpu.CompilerParams(dimension_semantics=sem, vmem_limit_bytes=VMEM_LIMIT)


def _sigmoid(x):
    return 1.0 / (1.0 + jnp.exp(-x))


def _silu(x):
    return x * _sigmoid(x)


def _softplus(x):
    return jnp.maximum(x, 0.0) + jnp.log1p(jnp.exp(-jnp.abs(x)))


def _rms(x, gain):
    y = x * lax.rsqrt(jnp.mean(x * x, axis=-1, keepdims=True) + EPS)
    return y * gain


def _mm(a, b):
    return jnp.dot(a.astype(BF16), b.astype(BF16), preferred_element_type=F32)


def _mm_nt(a, b):
    return lax.dot_general(a.astype(BF16), b.astype(BF16), (((1,), (1,)), ((), ())),
                           preferred_element_type=F32)


def _mm_tn(a, b):
    return lax.dot_general(a.astype(BF16), b.astype(BF16), (((0,), (0,)), ((), ())),
                           preferred_element_type=F32)


def _split3(a):
    a1 = a.astype(BF16)
    r1 = a - a1.astype(F32)
    a2 = r1.astype(BF16)
    a3 = (r1 - a2.astype(F32)).astype(BF16)
    return a1, a2, a3


def _norm_matmul_body(x_ref, g_ref, w_ref, o_ref, xn_ref):
    @pl.when(pl.program_id(1) == 0)
    def _():
        xn_ref[...] = _rms(x_ref[...], g_ref[...]).astype(BF16)

    o_ref[...] = jnp.dot(xn_ref[...], w_ref[...], preferred_element_type=F32).astype(o_ref.dtype)


def _norm_matmul(x, gain, w, out_dtype, tn):
    n, d = x.shape
    nout = w.shape[1]
    tm = min(PROJ_TM, n)
    return pl.pallas_call(
        _norm_matmul_body,
        grid=(n // tm, nout // tn),
        in_specs=[pl.BlockSpec((tm, d), lambda i, j: (i, 0)),
                  pl.BlockSpec((1, d), lambda i, j: (0, 0)),
                  pl.BlockSpec((d, tn), lambda i, j: (0, j))],
        out_specs=pl.BlockSpec((tm, tn), lambda i, j: (i, j)),
        out_shape=jax.ShapeDtypeStruct((n, nout), out_dtype),
        scratch_shapes=[pltpu.VMEM((tm, d), BF16)],
        compiler_params=_cparams(("parallel", "arbitrary")),
    )(x, gain.reshape(1, d), w)


def _unit_lower_inverse(ms, r, c):
    eye = (r == c).astype(F32)
    bd16 = (r >> 4) == (c >> 4)
    in32 = ((r >> 5) == (c >> 5)) & ((r >> 4) > (c >> 4))
    in64 = (r >> 5) > (c >> 5)
    xs = [jnp.where(bd16, -m, 0.0) for m in ms]
    ps = [eye + x for x in xs]
    xs = [_mm(x, x) for x in xs]
    for _ in range(2):
        ts = [_mm(jnp.concatenate([p, x], axis=0), x) for p, x in zip(ps, xs)]
        ps = [p + t[0:CHUNK] for p, t in zip(ps, ts)]
        xs = [t[CHUNK:2 * CHUNK] for t in ts]
    ps = [p + _mm(p, x) for p, x in zip(ps, xs)]
    for sel in (in32, in64):
        ls = [_mm(p, jnp.where(sel, m, 0.0)) for p, m in zip(ps, ms)]
        ps = [p - _mm(l, p) for p, l in zip(ps, ls)]
    return ps


def _deltanet_body(qkv_ref, gate_ref, ba_ref, convw_ref, hp_ref, dnorm_ref, o_ref, ext_ref, s_ref):
    ci = pl.program_id(1)

    @pl.when(ci == 0)
    def _():
        ext_ref[0:HALO, :] = jnp.zeros((HALO, 3 * DN_WIDTH), F32)
        s_ref[...] = jnp.zeros_like(s_ref)

    ext_ref[HALO:HALO + CHUNK, :] = qkv_ref[...]
    conv = convw_ref[0:1, :] * ext_ref[HALO - 3:HALO - 3 + CHUNK, :]
    for j in range(1, CONV_K):
        conv = conv + convw_ref[j:j + 1, :] * ext_ref[HALO - 3 + j:HALO - 3 + j + CHUNK, :]
    ext_ref[0:HALO, :] = ext_ref[CHUNK:CHUNK + HALO, :]
    act = _silu(conv)

    ba = ba_ref[...]
    beta_all = _sigmoid(ba)
    g_all = -jnp.exp(hp_ref[1:2, :]) * _softplus(ba + hp_ref[0:1, :])
    r = lax.broadcasted_iota(jnp.int32, (CHUNK, CHUNK), 0)
    c = lax.broadcasted_iota(jnp.int32, (CHUNK, CHUNK), 1)
    incl = r >= c
    strict = r > c
    tri = incl.astype(BF16)
    g1, g2, g3 = _split3(g_all)
    gcum_all = (jnp.dot(tri, g1, preferred_element_type=F32)
                + jnp.dot(tri, g2, preferred_element_type=F32)
                + jnp.dot(tri, g3, preferred_element_type=F32))
    gcum_t = gcum_all.T
    dnorm = dnorm_ref[...]
    scale = DN_HEAD_DIM ** -0.5
    heads = range(DN_HEADS)

    def l2n(x):
        return x * lax.rsqrt(jnp.sum(x * x, axis=-1, keepdims=True) + EPS)

    qs, ks, vs = [], [], []
    for h in heads:
        lo = h * DN_HEAD_DIM
        qs.append(l2n(act[:, lo:lo + DN_HEAD_DIM]) * scale)
        ks.append(l2n(act[:, DN_WIDTH + lo:DN_WIDTH + lo + DN_HEAD_DIM]))
        vs.append(act[:, 2 * DN_WIDTH + lo:2 * DN_WIDTH + lo + DN_HEAD_DIM])
    betas = [beta_all[:, h:h + 1] for h in heads]
    gcs = [gcum_all[:, DN_HEADS + h:DN_HEADS + h + 1] for h in heads]
    g_lasts = [gcum_all[CHUNK - 1:CHUNK, DN_HEADS + h:DN_HEADS + h + 1] for h in heads]
    decays = [jnp.where(incl, jnp.exp(jnp.where(
        incl, gcs[h] - gcum_t[DN_HEADS + h:DN_HEADS + h + 1, :], 0.0)), 0.0) for h in heads]
    kbs = [ks[h] * betas[h] for h in heads]

    kq = [_mm_nt(jnp.concatenate([kbs[h], qs[h]], axis=0), ks[h]) for h in heads]
    ms = [jnp.where(strict, kq[h][0:CHUNK] * decays[h], 0.0) for h in heads]
    attns = [jnp.where(incl, kq[h][CHUNK:2 * CHUNK] * decays[h], 0.0) for h in heads]
    ts = _unit_lower_inverse(ms, r, c)
    uw = [_mm(ts[h], jnp.concatenate([vs[h] * betas[h], kbs[h] * jnp.exp(gcs[h])], axis=1))
          for h in heads]
    states = [s_ref[h] for h in heads]
    ws = [_mm(jnp.concatenate([uw[h][:, DN_HEAD_DIM:], qs[h] * jnp.exp(gcs[h])], axis=0), states[h])
          for h in heads]
    v_news = [uw[h][:, 0:DN_HEAD_DIM] - ws[h][0:CHUNK] for h in heads]
    intra = [_mm(attns[h], v_news[h]) for h in heads]
    upd = [_mm_tn(ks[h] * jnp.exp(g_lasts[h] - gcs[h]), v_news[h]) for h in heads]
    for h in heads:
        lo = h * DN_HEAD_DIM
        s_ref[h] = states[h] * jnp.exp(g_lasts[h]) + upd[h]
        o = ws[h][CHUNK:2 * CHUNK] + intra[h]
        gt = gate_ref[:, lo:lo + DN_HEAD_DIM]
        o_ref[:, lo:lo + DN_HEAD_DIM] = (_rms(o, dnorm) * _silu(gt)).astype(o_ref.dtype)


def _deltanet(proj_a, conv_w, dt_bias, a_log, dn_norm, batch, seq):
    n = batch * seq
    nc = seq // CHUNK
    wq = 3 * DN_WIDTH
    hp = jnp.zeros((SUBLANES, LANES), F32)
    hp = hp.at[0, DN_HEADS:2 * DN_HEADS].set(dt_bias).at[1, DN_HEADS:2 * DN_HEADS].set(a_log)
    row = lambda b, ci: b * nc + ci
    return pl.pallas_call(
        _deltanet_body,
        grid=(batch, nc),
        in_specs=[pl.BlockSpec((CHUNK, wq), lambda b, ci: (row(b, ci), 0)),
                  pl.BlockSpec((CHUNK, DN_WIDTH), lambda b, ci: (row(b, ci), wq // DN_WIDTH)),
                  pl.BlockSpec((CHUNK, LANES), lambda b, ci: (row(b, ci), (wq + DN_WIDTH) // LANES)),
                  pl.BlockSpec((CONV_K, wq), lambda b, ci: (0, 0)),
                  pl.BlockSpec((SUBLANES, LANES), lambda b, ci: (0, 0)),
                  pl.BlockSpec((1, DN_HEAD_DIM), lambda b, ci: (0, 0))],
        out_specs=pl.BlockSpec((CHUNK, DN_WIDTH), lambda b, ci: (row(b, ci), 0)),
        out_shape=jax.ShapeDtypeStruct((n, DN_WIDTH), BF16),
        scratch_shapes=[pltpu.VMEM((CHUNK + HALO, wq), F32),
                        pltpu.VMEM((DN_HEADS, DN_HEAD_DIM, DN_HEAD_DIM), F32)],
        compiler_params=_cparams(("parallel", "arbitrary")),
    )(proj_a, proj_a, proj_a, conv_w, hp, dn_norm.reshape(1, DN_HEAD_DIM))


def _diffattn_body(slopes_ref, q_ref, k_ref, vt_ref, lamv_ref, danorm_ref, o_ref,
                   m_ref, l_ref, acc_ref, nrc_ref, dbias_ref, *, tq, lam_init):
    h = pl.program_id(1)
    qi = pl.program_id(2)
    slope = slopes_ref[h]
    q = q_ref[...]
    lane = lax.broadcasted_iota(jnp.int32, q.shape, 1)
    zero = jnp.zeros_like(q)
    qq = jnp.concatenate([jnp.where(lane < DA_HEAD_DIM, q, zero),
                          jnp.where(lane >= DA_HEAD_DIM, q, zero)], axis=0)
    cols = 2 * tq

    @pl.when(qi == 0)
    def _():
        kr = lax.broadcasted_iota(jnp.int32, (tq, cols), 0)
        qc = lax.broadcasted_iota(jnp.int32, (tq, cols), 1) & (tq - 1)
        rc = (qc - kr).astype(F32)
        nrc_ref[...] = -slope * rc
        dbias_ref[...] = jnp.where((kr >> 6) <= (qc >> 6), -slope * jnp.abs(rc), -jnp.inf)

    def logits_t(j):
        kb = k_ref[pl.ds(pl.multiple_of(j * tq, tq), tq), :]
        return _mm_nt(kb, qq)

    def pv_update(p, a, j):
        acc_ref[...] = a * acc_ref[...] + jnp.dot(vt_ref[j], p, preferred_element_type=F32)

    def softmax_step(s, bias_ref, shift, first):
        z = s + bias_ref[...]
        zmax = jnp.max(z, axis=0, keepdims=True) - shift
        if first:
            m_new = zmax
        else:
            m_old = m_ref[...]
            m_new = jnp.maximum(m_old, zmax)
        p = jnp.exp(z - (m_new + shift))
        psum = jnp.sum(p, axis=0, keepdims=True)
        m_ref[...] = m_new
        if first:
            l_ref[...] = psum
            return p.astype(BF16)
        scale = jnp.exp(m_old - m_new)
        l_ref[...] = scale * l_ref[...] + psum
        return p.astype(BF16), scale

    s_first = logits_t(0)
    p0 = softmax_step(logits_t(qi), dbias_ref, 0.0, True)
    acc_ref[...] = jnp.zeros_like(acc_ref)

    def body(kj, carry):
        s_cur, p_prev, a_prev = carry
        pv_update(p_prev, a_prev, jnp.where(kj == 0, qi, kj - 1))
        s_next = logits_t(jnp.minimum(kj + 1, qi))
        shift = slope * ((qi - kj) * tq).astype(F32)
        pb, a = softmax_step(s_cur, nrc_ref, shift, False)
        return s_next, pb, a

    _, p_last, a_last = lax.fori_loop(0, qi, body, (s_first, p0, jnp.ones((1, cols), F32)))
    pv_update(p_last, a_last, jnp.where(qi == 0, qi, qi - 1))

    lv = lamv_ref[...]
    lam = (jnp.exp(jnp.sum(lv[0:1] * lv[1:2], axis=-1, keepdims=True))
           - jnp.exp(jnp.sum(lv[2:3] * lv[3:4], axis=-1, keepdims=True)) + lam_init)
    o1 = acc_ref[:, 0:tq] / l_ref[:, 0:tq]
    o2 = acc_ref[:, tq:2 * tq] / l_ref[:, tq:2 * tq]
    o = o1 - lam * o2
    o = o * lax.rsqrt(jnp.mean(o * o, axis=0, keepdims=True) + EPS) * danorm_ref[...]
    o_ref[...] = (o * (1.0 - lam_init)).T.astype(o_ref.dtype)


def _diffattn(qk, v_t, lam_vecs, da_norm, lam_init, batch, seq):
    n = batch * seq
    tq = v_t.shape[2]
    nq = seq // tq
    slopes = jnp.exp2(-8.0 * jnp.arange(1, DA_HEADS + 1, dtype=F32) / DA_HEADS)
    grid_spec = pltpu.PrefetchScalarGridSpec(
        num_scalar_prefetch=1,
        grid=(batch, DA_HEADS, nq),
        in_specs=[pl.BlockSpec((tq, DA_V_DIM), lambda b, h, qi, sl: (b * nq + qi, h)),
                  pl.BlockSpec((seq, DA_V_DIM), lambda b, h, qi, sl: (b, DA_HEADS + h)),
                  pl.BlockSpec((nq, DA_V_DIM, tq), lambda b, h, qi, sl: (b, h, 0)),
                  pl.BlockSpec((4, DA_HEAD_DIM), lambda b, h, qi, sl: (0, 0)),
                  pl.BlockSpec((DA_V_DIM, 1), lambda b, h, qi, sl: (0, 0))],
        out_specs=pl.BlockSpec((tq, DA_V_DIM), lambda b, h, qi, sl: (b * nq + qi, h)),
        scratch_shapes=[pltpu.VMEM((1, 2 * tq), F32), pltpu.VMEM((1, 2 * tq), F32),
                        pltpu.VMEM((DA_V_DIM, 2 * tq), F32), pltpu.VMEM((tq, 2 * tq), F32),
                        pltpu.VMEM((tq, 2 * tq), F32)],
    )
    return pl.pallas_call(
        functools.partial(_diffattn_body, tq=tq, lam_init=lam_init),
        grid_spec=grid_spec,
        out_shape=jax.ShapeDtypeStruct((n, DA_WIDTH), BF16),
        compiler_params=_cparams(("parallel", "parallel", "arbitrary")),
    )(slopes, qk, qk, v_t, lam_vecs, da_norm.reshape(DA_V_DIM, 1))


def _attn_proj_body(x_ref, g_ref, wqk_ref, wvt_ref, qk_ref, vt_ref, *, tq):
    xn = _rms(x_ref[...], g_ref[...]).astype(BF16)
    qk_ref[...] = jnp.dot(xn, wqk_ref[...], preferred_element_type=F32).astype(qk_ref.dtype)
    for j in range(vt_ref.shape[0]):
        vt_ref[j] = lax.dot_general(wvt_ref[...], xn[j * tq:(j + 1) * tq], (((1,), (1,)), ((), ())),
                                    preferred_element_type=F32).astype(vt_ref.dtype)


def _attn_proj(x, gain, w_qk, w_vt, tq):
    n, d = x.shape
    tm = min(ROW_TM, n)
    nb = tm // tq
    return pl.pallas_call(
        functools.partial(_attn_proj_body, tq=tq),
        grid=(n // tm,),
        in_specs=[pl.BlockSpec((tm, d), lambda i: (i, 0)),
                  pl.BlockSpec((1, d), lambda i: (0, 0)),
                  pl.BlockSpec((d, 2 * DA_WIDTH), lambda i: (0, 0)),
                  pl.BlockSpec((DA_WIDTH, d), lambda i: (0, 0))],
        out_specs=[pl.BlockSpec((tm, 2 * DA_WIDTH), lambda i: (i, 0)),
                   pl.BlockSpec((nb, DA_WIDTH, tq), lambda i: (i, 0, 0))],
        out_shape=[jax.ShapeDtypeStruct((n, 2 * DA_WIDTH), BF16),
                   jax.ShapeDtypeStruct((n // tq, DA_WIDTH, tq), BF16)],
        compiler_params=_cparams(("parallel",)),
    )(x, gain.reshape(1, d), w_qk, w_vt)


def _outproj_body(h_ref, odn_ref, oda_ref, w_ref, o_ref):
    o_ref[...] = (h_ref[...]
                  + jnp.dot(odn_ref[...], w_ref[0:DN_WIDTH, :], preferred_element_type=F32)
                  + jnp.dot(oda_ref[...], w_ref[DN_WIDTH:DN_WIDTH + DA_WIDTH, :],
                            preferred_element_type=F32))


def _outproj(h, o_dn, o_da, w_out):
    n, d = h.shape
    tm = min(ROW_TM, n)
    return pl.pallas_call(
        _outproj_body,
        grid=(n // tm,),
        in_specs=[pl.BlockSpec((tm, d), lambda i: (i, 0)),
                  pl.BlockSpec((tm, DN_WIDTH), lambda i: (i, 0)),
                  pl.BlockSpec((tm, DA_WIDTH), lambda i: (i, 0)),
                  pl.BlockSpec((DN_WIDTH + DA_WIDTH, d), lambda i: (0, 0))],
        out_specs=pl.BlockSpec((tm, d), lambda i: (i, 0)),
        out_shape=jax.ShapeDtypeStruct((n, d), F32),
        compiler_params=_cparams(("parallel",)),
    )(h, o_dn, o_da, w_out)


def _ffn_body(x_ref, g_ref, w1_ref, w3_ref, w2_ref, o_ref, xn_ref):
    @pl.when(pl.program_id(1) == 0)
    def _():
        x = x_ref[...]
        xn_ref[...] = _rms(x, g_ref[...]).astype(BF16)
        o_ref[...] = x

    xn = xn_ref[...]
    g = jnp.dot(xn, w1_ref[...], preferred_element_type=F32)
    u = jnp.dot(xn, w3_ref[...], preferred_element_type=F32)
    a = (_silu(g) * u).astype(BF16)
    o_ref[...] += jnp.dot(a, w2_ref[...], preferred_element_type=F32)


def _ffn(x, gain, w1, w3, w2):
    n, d = x.shape
    dff = w1.shape[1]
    tm = min(FFN_TM, n)
    tf = FFN_TF
    return pl.pallas_call(
        _ffn_body,
        grid=(n // tm, dff // tf),
        in_specs=[pl.BlockSpec((tm, d), lambda i, f: (i, 0)),
                  pl.BlockSpec((1, d), lambda i, f: (0, 0)),
                  pl.BlockSpec((d, tf), lambda i, f: (0, f)),
                  pl.BlockSpec((d, tf), lambda i, f: (0, f)),
                  pl.BlockSpec((tf, d), lambda i, f: (f, 0))],
        out_specs=pl.BlockSpec((tm, d), lambda i, f: (i, 0)),
        out_shape=jax.ShapeDtypeStruct((n, d), F32),
        scratch_shapes=[pltpu.VMEM((tm, d), BF16)],
        compiler_params=_cparams(("parallel", "arbitrary")),
    )(x, gain.reshape(1, d), w1, w3, w2)


def _router_body(x_ref, g_ref, wt_ref, idx_ref, gate_ref):
    xn = _rms(x_ref[...], g_ref[...])
    x1, x2, x3 = _split3(xn)
    w1, w2, w3 = _split3(wt_ref[...])
    nt = lambda a, b: lax.dot_general(a, b, (((1,), (1,)), ((), ())), preferred_element_type=F32)
    logits = (nt(w3, x1) + nt(w2, x2) + nt(w1, x3)) + (nt(w2, x1) + nt(w1, x2)) + nt(w1, x1)
    e = lax.broadcasted_iota(jnp.int32, logits.shape, 0)
    m0 = jnp.max(logits, axis=0, keepdims=True)
    i0 = jnp.min(jnp.where(logits == m0, e, N_EXPERTS), axis=0, keepdims=True)
    rest = jnp.where(e == i0, -jnp.inf, logits)
    m1 = jnp.max(rest, axis=0, keepdims=True)
    i1 = jnp.min(jnp.where(rest == m1, e, N_EXPERTS), axis=0, keepdims=True)
    t = jnp.exp(m1 - m0)
    idx_ref[0:1, :] = i0
    idx_ref[1:2, :] = i1
    gate_ref[0:1, :] = 1.0 / (1.0 + t)
    gate_ref[1:2, :] = t / (1.0 + t)


def _router(x, gain, w_router):
    n, d = x.shape
    tm = min(ROW_TM, n)
    return pl.pallas_call(
        _router_body,
        grid=(n // tm,),
        in_specs=[pl.BlockSpec((tm, d), lambda i: (i, 0)),
                  pl.BlockSpec((1, d), lambda i: (0, 0)),
                  pl.BlockSpec((N_EXPERTS, d), lambda i: (0, 0))],
        out_specs=[pl.BlockSpec((TOP_K, tm), lambda i: (0, i)),
                   pl.BlockSpec((TOP_K, tm), lambda i: (0, i))],
        out_shape=[jax.ShapeDtypeStruct((TOP_K, n), jnp.int32),
                   jax.ShapeDtypeStruct((TOP_K, n), F32)],
        compiler_params=_cparams(("parallel",)),
    )(x, gain.reshape(1, d), w_router.T)


def _gather_rows(src_hbm, dst_ref, sem, n_rows, row_index):
    def copy(i):
        return pltpu.make_async_copy(src_hbm.at[pl.ds(row_index(i), 1), :],
                                     dst_ref.at[pl.ds(i, 1), :], sem)

    def start(i, carry):
        copy(i).start()
        return carry

    def wait(i, carry):
        copy(i).wait()
        return carry

    lax.fori_loop(0, n_rows, start, 0)
    lax.fori_loop(0, n_rows, wait, 0)


def _experts_body(be_ref, tok_ref, nb_ref, x_hbm, g_ref, w1_ref, w3_ref, w2_ref, o_ref,
                  xg_ref, xn_ref, sem, *, tm):
    b = pl.program_id(0)
    f = pl.program_id(1)
    used = b < nb_ref[0]

    @pl.when(jnp.logical_and(f == 0, used))
    def _():
        _gather_rows(x_hbm, xg_ref, sem, tm, lambda i: tok_ref[b * tm + i])
        xn_ref[...] = _rms(xg_ref[...], g_ref[...]).astype(BF16)

    @pl.when(f == 0)
    def _():
        o_ref[...] = jnp.zeros_like(o_ref)

    @pl.when(used)
    def _():
        xn = xn_ref[...]
        g = jnp.dot(xn, w1_ref[...], preferred_element_type=F32)
        u = jnp.dot(xn, w3_ref[...], preferred_element_type=F32)
        a = (_silu(g) * u).astype(BF16)
        o_ref[...] += jnp.dot(a, w2_ref[...], preferred_element_type=F32)


def _experts(x, gain, w1, w3, w2, block_e, row_tok, n_used, n_blocks, tm):
    n, d = x.shape
    dff = w1.shape[2]
    tf = FFN_TF
    nf = dff // tf

    def widx(b, f, be, tok, nb):
        live = b < nb[0]
        last = jnp.maximum(nb[0] - 1, 0)
        return jnp.where(live, be[b], be[last]), jnp.where(live, f, nf - 1)

    def w13_map(b, f, be, tok, nb):
        e, ff = widx(b, f, be, tok, nb)
        return e, 0, ff

    def w2_map(b, f, be, tok, nb):
        e, ff = widx(b, f, be, tok, nb)
        return e, ff, 0

    grid_spec = pltpu.PrefetchScalarGridSpec(
        num_scalar_prefetch=3,
        grid=(n_blocks, nf),
        in_specs=[pl.BlockSpec(memory_space=pl.ANY),
                  pl.BlockSpec((1, d), lambda b, f, be, tok, nb: (0, 0)),
                  pl.BlockSpec((None, d, tf), w13_map),
                  pl.BlockSpec((None, d, tf), w13_map),
                  pl.BlockSpec((None, tf, d), w2_map)],
        out_specs=pl.BlockSpec((tm, d), lambda b, f, be, tok, nb: (b, 0)),
        scratch_shapes=[pltpu.VMEM((tm, d), F32), pltpu.VMEM((tm, d), BF16),
                        pltpu.SemaphoreType.DMA(())],
    )
    return pl.pallas_call(
        functools.partial(_experts_body, tm=tm),
        grid_spec=grid_spec,
        out_shape=jax.ShapeDtypeStruct((n_blocks * tm, d), F32),
        compiler_params=_cparams(("arbitrary", "arbitrary")),
    )(block_e, row_tok, n_used, x, gain.reshape(1, d), w1, w3, w2)


def _route_plan(idx, n, tm):
    n_assign = n * TOP_K
    flat_e = idx.T.reshape(n_assign)
    onehot = (flat_e[:, None] == jnp.arange(N_EXPERTS, dtype=jnp.int32)[None, :]).astype(jnp.int32)
    csum = jnp.cumsum(onehot, axis=0)
    counts = csum[-1]
    rank = jnp.sum((csum - onehot) * onehot, axis=1)
    padded = (counts + tm - 1) // tm * tm
    pad_end = jnp.cumsum(padded)
    pad_start = pad_end - padded
    dest = pad_start[flat_e] + rank
    n_blocks = -(-n_assign // tm) + N_EXPERTS
    n_rows = n_blocks * tm
    flat_tok = jnp.arange(n_assign, dtype=jnp.int32) // TOP_K
    row_tok = jnp.zeros((n_rows,), jnp.int32).at[dest].set(flat_tok)
    block_e = jnp.minimum(
        jnp.searchsorted(pad_end, jnp.arange(n_blocks, dtype=jnp.int32) * tm, side='right'),
        N_EXPERTS - 1).astype(jnp.int32)
    n_used = (pad_end[-1] // tm).astype(jnp.int32).reshape(1)
    return dest.reshape(n, TOP_K).astype(jnp.int32), row_tok, block_e, n_used, n_blocks


def _ple_tail(h, p_ref, wp_ref, pn_ref, wg_ref, nf_ref, o_ref, final):
    e = _rms(_mm(p_ref[...], wp_ref[...]), pn_ref[...])
    out = h + _sigmoid(_mm(h, wg_ref[...])) * e
    if final:
        out = _rms(out, nf_ref[...])
    o_ref[...] = out


def _ple_body(h_ref, p_ref, wp_ref, pn_ref, wg_ref, nf_ref, o_ref, *, final):
    _ple_tail(h_ref[...], p_ref, wp_ref, pn_ref, wg_ref, nf_ref, o_ref, final)


def _ple_combine_body(pos_ref, h_ref, y_hbm, g0_ref, g1_ref, p_ref, wp_ref, pn_ref, wg_ref, nf_ref,
                      o_ref, y0_ref, y1_ref, sem, *, tm, final):
    i = pl.program_id(0)
    _gather_rows(y_hbm, y0_ref, sem, tm, lambda t: pos_ref[(i * tm + t) * TOP_K])
    _gather_rows(y_hbm, y1_ref, sem, tm, lambda t: pos_ref[(i * tm + t) * TOP_K + 1])
    h = h_ref[...] + (y0_ref[...] * g0_ref[...] + y1_ref[...] * g1_ref[...])
    _ple_tail(h, p_ref, wp_ref, pn_ref, wg_ref, nf_ref, o_ref, final)


def _ple(h, p, w_proj, p_norm, w_gate, norm_final, final, moe=None):
    n, d = h.shape
    dp = p.shape[1]
    tm = min(ROW_TM // 2, n) if moe is not None else min(ROW_TM, n)
    row = lambda i, *_: (i, 0)
    fixed = lambda i, *_: (0, 0)
    tail_specs = [pl.BlockSpec((tm, dp), row), pl.BlockSpec((dp, d), fixed),
                  pl.BlockSpec((1, d), fixed), pl.BlockSpec((d, d), fixed),
                  pl.BlockSpec((1, d), fixed)]
    tail_args = (p, w_proj, p_norm.reshape(1, d), w_gate, norm_final.reshape(1, d))
    out_shape = jax.ShapeDtypeStruct((n, d), F32)
    if moe is None:
        return pl.pallas_call(
            functools.partial(_ple_body, final=final),
            grid=(n // tm,),
            in_specs=[pl.BlockSpec((tm, d), row)] + tail_specs,
            out_specs=pl.BlockSpec((tm, d), row),
            out_shape=out_shape,
            compiler_params=_cparams(("parallel",)),
        )(h, *tail_args)
    y_rows, pos, gates = moe
    grid_spec = pltpu.PrefetchScalarGridSpec(
        num_scalar_prefetch=1,
        grid=(n // tm,),
        in_specs=[pl.BlockSpec((tm, d), row), pl.BlockSpec(memory_space=pl.ANY),
                  pl.BlockSpec((tm, 1), row), pl.BlockSpec((tm, 1), row)] + tail_specs,
        out_specs=pl.BlockSpec((tm, d), row),
        scratch_shapes=[pltpu.VMEM((tm, d), F32), pltpu.VMEM((tm, d), F32),
                        pltpu.SemaphoreType.DMA(())],
    )
    return pl.pallas_call(
        functools.partial(_ple_combine_body, tm=tm, final=final),
        grid_spec=grid_spec,
        out_shape=out_shape,
        compiler_params=_cparams(("arbitrary",)),
    )(pos.reshape(n * TOP_K), h, y_rows, gates[0].reshape(n, 1), gates[1].reshape(n, 1), *tail_args)


def kernel(x, p, norm_mix, w_in, conv_w, dn_a_log, dn_dt_bias, dn_norm, da_lambda_q1, da_lambda_k1, da_lambda_q2, da_lambda_k2, da_norm, w_out, norm_ffn, ffn_w1, ffn_w3, ffn_w2, moe_router, moe_w1, moe_w3, moe_w2, ple_proj, ple_norm, ple_gate, norm_final):
    batch, seq, d = x.shape
    n = batch * seq
    depth = w_in.shape[0]
    h = x.reshape(n, d)
    wq = 3 * DN_WIDTH
    for i in range(depth):
        wi = w_in[i]
        c0 = wq + DN_WIDTH
        c1 = c0 + 2 * DN_HEADS
        pad = jnp.zeros((d, LANES - 2 * DN_HEADS), F32)
        w_a = jnp.concatenate([wi[:, :c1], pad], axis=1).astype(BF16)
        att_scale = DA_HEAD_DIM ** -0.5
        w_qk = jnp.concatenate([wi[:, c1:c1 + DA_WIDTH] * att_scale,
                                wi[:, c1 + DA_WIDTH:c1 + 2 * DA_WIDTH]], axis=1).astype(BF16)
        w_vt = wi[:, c1 + 2 * DA_WIDTH:].T.astype(BF16)
        proj_a = _norm_matmul(h, norm_mix[i], w_a, F32, (c0 + LANES) // 3)
        qk, v_t = _attn_proj(h, norm_mix[i], w_qk, w_vt, min(ATT_TQ, seq))

        o_dn = _deltanet(proj_a, conv_w[i], dn_dt_bias[i], dn_a_log[i], dn_norm[i], batch, seq)
        lam_init = 0.8 - 0.6 * math.exp(-0.3 * i)
        lam_vecs = jnp.stack([da_lambda_q1[i], da_lambda_k1[i], da_lambda_q2[i], da_lambda_k2[i]])
        o_da = _diffattn(qk, v_t, lam_vecs, da_norm[i], lam_init, batch, seq)
        h = _outproj(h, o_dn, o_da, w_out[i].astype(BF16))

        j = i // 2
        last = i == depth - 1
        wp = ple_proj[i].astype(BF16)
        wg = ple_gate[i].astype(BF16)
        if i % 2 == 0:
            h = _ffn(h, norm_ffn[i], ffn_w1[j].astype(BF16), ffn_w3[j].astype(BF16),
                     ffn_w2[j].astype(BF16))
            h = _ple(h, p[i].reshape(n, -1), wp, ple_norm[i], wg, norm_final, last)
        else:
            idx, gates = _router(h, norm_ffn[i], moe_router[j])
            tm = min(MOE_TM, n)
            pos, row_tok, block_e, n_used, n_blocks = _route_plan(idx, n, tm)
            y_rows = _experts(h, norm_ffn[i], moe_w1[j].astype(BF16), moe_w3[j].astype(BF16),
                              moe_w2[j].astype(BF16), block_e, row_tok, n_used, n_blocks, tm)
            h = _ple(h, p[i].reshape(n, -1), wp, ple_norm[i], wg, norm_final, last,
                     moe=(y_rows, pos, gates))
    return h.reshape(batch, seq, d)
```

```python
import functools
import math

import jax
import jax.numpy as jnp
from jax import lax
from jax.experimental import pallas as pl
from jax.experimental.pallas import tpu as pltpu

F32 = jnp.float32
BF16 = jnp.bfloat16
EPS = 1e-6

CHUNK = 64
DN_HEADS = 8
DN_HEAD_DIM = 128
DN_WIDTH = DN_HEADS * DN_HEAD_DIM
CONV_K = 4
DA_HEADS = 8
DA_HEAD_DIM = 64
DA_V_DIM = 2 * DA_HEAD_DIM
DA_WIDTH = DA_HEADS * DA_V_DIM
N_EXPERTS = 8
TOP_K = 2
LANES = 128
SUBLANES = 8
VMEM_LIMIT = 56 * 1024 * 1024

PROJ_TM = 512
FFN_TM = 512
FFN_TF = 512
MOE_TM = 512
MOE_TF = 896
ATT_TQ = 256
ROW_TM = 512
HALO = SUBLANES


def _cparams(sem):
    return pltpu.CompilerParams(dimension_semantics=sem, vmem_limit_bytes=VMEM_LIMIT)


def _sigmoid(x):
    return 1.0 / (1.0 + jnp.exp(-x))


def _silu(x):
    return x * _sigmoid(x)


def _softplus(x):
    return jnp.maximum(x, 0.0) + jnp.log1p(jnp.exp(-jnp.abs(x)))


def _rms(x, gain):
    y = x * lax.rsqrt(jnp.mean(x * x, axis=-1, keepdims=True) + EPS)
    return y * gain


def _mm(a, b):
    return jnp.dot(a.astype(BF16), b.astype(BF16), preferred_element_type=F32)


def _mm_nt(a, b):
    return lax.dot_general(a.astype(BF16), b.astype(BF16), (((1,), (1,)), ((), ())),
                           preferred_element_type=F32)


def _mm_tn(a, b):
    return lax.dot_general(a.astype(BF16), b.astype(BF16), (((0,), (0,)), ((), ())),
                           preferred_element_type=F32)


def _split3(a):
    a1 = a.astype(BF16)
    r1 = a - a1.astype(F32)
    a2 = r1.astype(BF16)
    a3 = (r1 - a2.astype(F32)).astype(BF16)
    return a1, a2, a3


def _norm_matmul_body(x_ref, g_ref, w_ref, o_ref, xn_ref):
    @pl.when(pl.program_id(1) == 0)
    def _():
        xn_ref[...] = _rms(x_ref[...], g_ref[...]).astype(BF16)

    o_ref[...] = jnp.dot(xn_ref[...], w_ref[...], preferred_element_type=F32).astype(o_ref.dtype)


def _norm_matmul(x, gain, w, out_dtype, tn):
    n, d = x.shape
    nout = w.shape[1]
    tm = min(PROJ_TM, n)
    return pl.pallas_call(
        _norm_matmul_body,
        grid=(n // tm, nout // tn),
        in_specs=[pl.BlockSpec((tm, d), lambda i, j: (i, 0)),
                  pl.BlockSpec((1, d), lambda i, j: (0, 0)),
                  pl.BlockSpec((d, tn), lambda i, j: (0, j))],
        out_specs=pl.BlockSpec((tm, tn), lambda i, j: (i, j)),
        out_shape=jax.ShapeDtypeStruct((n, nout), out_dtype),
        scratch_shapes=[pltpu.VMEM((tm, d), BF16)],
        compiler_params=_cparams(("parallel", "arbitrary")),
    )(x, gain.reshape(1, d), w)


def _unit_lower_inverse(ms, r, c):
    eye = (r == c).astype(F32)
    bd16 = (r >> 4) == (c >> 4)
    in32 = ((r >> 5) == (c >> 5)) & ((r >> 4) > (c >> 4))
    in64 = (r >> 5) > (c >> 5)
    xs = [jnp.where(bd16, -m, 0.0) for m in ms]
    ps = [eye + x for x in xs]
    xs = [_mm(x, x) for x in xs]
    for _ in range(2):
        ts = [_mm(jnp.concatenate([p, x], axis=0), x) for p, x in zip(ps, xs)]
        ps = [p + t[0:CHUNK] for p, t in zip(ps, ts)]
        xs = [t[CHUNK:2 * CHUNK] for t in ts]
    ps = [p + _mm(p, x) for p, x in zip(ps, xs)]
    for sel in (in32, in64):
        ls = [_mm(p, jnp.where(sel, m, 0.0)) for p, m in zip(ps, ms)]
        ps = [p - _mm(l, p) for p, l in zip(ps, ls)]
    return ps


def _deltanet_body(qkv_ref, gate_ref, ba_ref, convw_ref, hp_ref, dnorm_ref, o_ref, ext_ref, s_ref):
    ci = pl.program_id(1)

    @pl.when(ci == 0)
    def _():
        ext_ref[0:HALO, :] = jnp.zeros((HALO, 3 * DN_WIDTH), F32)
        s_ref[...] = jnp.zeros_like(s_ref)

    ext_ref[HALO:HALO + CHUNK, :] = qkv_ref[...]
    conv = convw_ref[0:1, :] * ext_ref[HALO - 3:HALO - 3 + CHUNK, :]
    for j in range(1, CONV_K):
        conv = conv + convw_ref[j:j + 1, :] * ext_ref[HALO - 3 + j:HALO - 3 + j + CHUNK, :]
    ext_ref[0:HALO, :] = ext_ref[CHUNK:CHUNK + HALO, :]
    act = _silu(conv)

    ba = ba_ref[...]
    beta_all = _sigmoid(ba)
    g_all = -jnp.exp(hp_ref[1:2, :]) * _softplus(ba + hp_ref[0:1, :])
    r = lax.broadcasted_iota(jnp.int32, (CHUNK, CHUNK), 0)
    c = lax.broadcasted_iota(jnp.int32, (CHUNK, CHUNK), 1)
    incl = r >= c
    strict = r > c
    tri = incl.astype(BF16)
    g1, g2, g3 = _split3(g_all)
    gcum_all = (jnp.dot(tri, g1, preferred_element_type=F32)
                + jnp.dot(tri, g2, preferred_element_type=F32)
                + jnp.dot(tri, g3, preferred_element_type=F32))
    gcum_t = gcum_all.T
    dnorm = dnorm_ref[...]
    scale = DN_HEAD_DIM ** -0.5
    heads = range(DN_HEADS)

    def l2n(x):
        return x * lax.rsqrt(jnp.sum(x * x, axis=-1, keepdims=True) + EPS)

    qs, ks, vs = [], [], []
    for h in heads:
        lo = h * DN_HEAD_DIM
        qs.append(l2n(act[:, lo:lo + DN_HEAD_DIM]) * scale)
        ks.append(l2n(act[:, DN_WIDTH + lo:DN_WIDTH + lo + DN_HEAD_DIM]))
        vs.append(act[:, 2 * DN_WIDTH + lo:2 * DN_WIDTH + lo + DN_HEAD_DIM])
    betas = [beta_all[:, h:h + 1] for h in heads]
    gcs = [gcum_all[:, DN_HEADS + h:DN_HEADS + h + 1] for h in heads]
    g_lasts = [gcum_all[CHUNK - 1:CHUNK, DN_HEADS + h:DN_HEADS + h + 1] for h in heads]
    decays = [jnp.where(incl, jnp.exp(jnp.where(
        incl, gcs[h] - gcum_t[DN_HEADS + h:DN_HEADS + h + 1, :], 0.0)), 0.0) for h in heads]
    kbs = [ks[h] * betas[h] for h in heads]

    kq = [_mm_nt(jnp.concatenate([kbs[h], qs[h]], axis=0), ks[h]) for h in heads]
    ms = [jnp.where(strict, kq[h][0:CHUNK] * decays[h], 0.0) for h in heads]
    attns = [jnp.where(incl, kq[h][CHUNK:2 * CHUNK] * decays[h], 0.0) for h in heads]
    ts = _unit_lower_inverse(ms, r, c)
    uw = [_mm(ts[h], jnp.concatenate([vs[h] * betas[h], kbs[h] * jnp.exp(gcs[h])], axis=1))
          for h in heads]
    states = [s_ref[h] for h in heads]
    ws = [_mm(jnp.concatenate([uw[h][:, DN_HEAD_DIM:], qs[h] * jnp.exp(gcs[h])], axis=0), states[h])
          for h in heads]
    v_news = [uw[h][:, 0:DN_HEAD_DIM] - ws[h][0:CHUNK] for h in heads]
    intra = [_mm(attns[h], v_news[h]) for h in heads]
    upd = [_mm_tn(ks[h] * jnp.exp(g_lasts[h] - gcs[h]), v_news[h]) for h in heads]
    for h in heads:
        lo = h * DN_HEAD_DIM
        s_ref[h] = states[h] * jnp.exp(g_lasts[h]) + upd[h]
        o = ws[h][CHUNK:2 * CHUNK] + intra[h]
        gt = gate_ref[:, lo:lo + DN_HEAD_DIM]
        o_ref[:, lo:lo + DN_HEAD_DIM] = (_rms(o, dnorm) * _silu(gt)).astype(o_ref.dtype)


def _deltanet(proj_a, conv_w, dt_bias, a_log, dn_norm, batch, seq):
    n = batch * seq
    nc = seq // CHUNK
    wq = 3 * DN_WIDTH
    hp = jnp.zeros((SUBLANES, LANES), F32)
    hp = hp.at[0, DN_HEADS:2 * DN_HEADS].set(dt_bias).at[1, DN_HEADS:2 * DN_HEADS].set(a_log)
    row = lambda b, ci: b * nc + ci
    return pl.pallas_call(
        _deltanet_body,
        grid=(batch, nc),
        in_specs=[pl.BlockSpec((CHUNK, wq), lambda b, ci: (row(b, ci), 0)),
                  pl.BlockSpec((CHUNK, DN_WIDTH), lambda b, ci: (row(b, ci), wq // DN_WIDTH)),
                  pl.BlockSpec((CHUNK, LANES), lambda b, ci: (row(b, ci), (wq + DN_WIDTH) // LANES)),
                  pl.BlockSpec((CONV_K, wq), lambda b, ci: (0, 0)),
                  pl.BlockSpec((SUBLANES, LANES), lambda b, ci: (0, 0)),
                  pl.BlockSpec((1, DN_HEAD_DIM), lambda b, ci: (0, 0))],
        out_specs=pl.BlockSpec((CHUNK, DN_WIDTH), lambda b, ci: (row(b, ci), 0)),
        out_shape=jax.ShapeDtypeStruct((n, DN_WIDTH), BF16),
        scratch_shapes=[pltpu.VMEM((CHUNK + HALO, wq), F32),
                        pltpu.VMEM((DN_HEADS, DN_HEAD_DIM, DN_HEAD_DIM), F32)],
        compiler_params=_cparams(("parallel", "arbitrary")),
    )(proj_a, proj_a, proj_a, conv_w, hp, dn_norm.reshape(1, DN_HEAD_DIM))


def _diffattn_body(slopes_ref, q_ref, k_ref, vt_ref, lamv_ref, danorm_ref, o_ref,
                   m_ref, l_ref, acc_ref, nrc_ref, dbias_ref, *, tq, lam_init):
    h = pl.program_id(1)
    qi = pl.program_id(2)
    slope = slopes_ref[h]
    q = q_ref[...]
    lane = lax.broadcasted_iota(jnp.int32, q.shape, 1)
    zero = jnp.zeros_like(q)
    qq = jnp.concatenate([jnp.where(lane < DA_HEAD_DIM, q, zero),
                          jnp.where(lane >= DA_HEAD_DIM, q, zero)], axis=0)
    cols = 2 * tq

    @pl.when(qi == 0)
    def _():
        kr = lax.broadcasted_iota(jnp.int32, (tq, cols), 0)
        qc = lax.broadcasted_iota(jnp.int32, (tq, cols), 1) & (tq - 1)
        rc = (qc - kr).astype(F32)
        nrc_ref[...] = -slope * rc
        dbias_ref[...] = jnp.where((kr >> 6) <= (qc >> 6), -slope * jnp.abs(rc), -jnp.inf)

    def logits_t(j):
        kb = k_ref[pl.ds(pl.multiple_of(j * tq, tq), tq), :]
        return _mm_nt(kb, qq)

    def pv_update(p, a, j):
        acc_ref[...] = a * acc_ref[...] + jnp.dot(vt_ref[j], p, preferred_element_type=F32)

    def softmax_step(s, bias_ref, shift, first):
        z = s + bias_ref[...]
        zmax = jnp.max(z, axis=0, keepdims=True) - shift
        if first:
            m_new = zmax
        else:
            m_old = m_ref[...]
            m_new = jnp.maximum(m_old, zmax)
        p = jnp.exp(z - (m_new + shift))
        psum = jnp.sum(p, axis=0, keepdims=True)
        m_ref[...] = m_new
        if first:
            l_ref[...] = psum
            return p.astype(BF16)
        scale = jnp.exp(m_old - m_new)
        l_ref[...] = scale * l_ref[...] + psum
        return p.astype(BF16), scale

    s_first = logits_t(0)
    p0 = softmax_step(logits_t(qi), dbias_ref, 0.0, True)
    acc_ref[...] = jnp.zeros_like(acc_ref)

    def body(kj, carry):
        s_cur, p_prev, a_prev = carry
        pv_update(p_prev, a_prev, jnp.where(kj == 0, qi, kj - 1))
        s_next = logits_t(jnp.minimum(kj + 1, qi))
        shift = slope * ((qi - kj) * tq).astype(F32)
        pb, a = softmax_step(s_cur, nrc_ref, shift, False)
        return s_next, pb, a

    _, p_last, a_last = lax.fori_loop(0, qi, body, (s_first, p0, jnp.ones((1, cols), F32)))
    pv_update(p_last, a_last, jnp.where(qi == 0, qi, qi - 1))

    lv = lamv_ref[...]
    lam = (jnp.exp(jnp.sum(lv[0:1] * lv[1:2], axis=-1, keepdims=True))
           - jnp.exp(jnp.sum(lv[2:3] * lv[3:4], axis=-1, keepdims=True)) + lam_init)
    o1 = acc_ref[:, 0:tq] / l_ref[:, 0:tq]
    o2 = acc_ref[:, tq:2 * tq] / l_ref[:, tq:2 * tq]
    o = o1 - lam * o2
    o = o * lax.rsqrt(jnp.mean(o * o, axis=0, keepdims=True) + EPS) * danorm_ref[...]
    o_ref[...] = (o * (1.0 - lam_init)).T.astype(o_ref.dtype)


def _diffattn(qk, v_t, lam_vecs, da_norm, lam_init, batch, seq):
    n = batch * seq
    tq = v_t.shape[2]
    nq = seq // tq
    slopes = jnp.exp2(-8.0 * jnp.arange(1, DA_HEADS + 1, dtype=F32) / DA_HEADS)
    grid_spec = pltpu.PrefetchScalarGridSpec(
        num_scalar_prefetch=1,
        grid=(batch, DA_HEADS, nq),
        in_specs=[pl.BlockSpec((tq, DA_V_DIM), lambda b, h, qi, sl: (b * nq + qi, h)),
                  pl.BlockSpec((seq, DA_V_DIM), lambda b, h, qi, sl: (b, DA_HEADS + h)),
                  pl.BlockSpec((nq, DA_V_DIM, tq), lambda b, h, qi, sl: (b, h, 0)),
                  pl.BlockSpec((4, DA_HEAD_DIM), lambda b, h, qi, sl: (0, 0)),
                  pl.BlockSpec((DA_V_DIM, 1), lambda b, h, qi, sl: (0, 0))],
        out_specs=pl.BlockSpec((tq, DA_V_DIM), lambda b, h, qi, sl: (b * nq + qi, h)),
        scratch_shapes=[pltpu.VMEM((1, 2 * tq), F32), pltpu.VMEM((1, 2 * tq), F32),
                        pltpu.VMEM((DA_V_DIM, 2 * tq), F32), pltpu.VMEM((tq, 2 * tq), F32),
                        pltpu.VMEM((tq, 2 * tq), F32)],
    )
    return pl.pallas_call(
        functools.partial(_diffattn_body, tq=tq, lam_init=lam_init),
        grid_spec=grid_spec,
        out_shape=jax.ShapeDtypeStruct((n, DA_WIDTH), BF16),
        compiler_params=_cparams(("parallel", "parallel", "arbitrary")),
    )(slopes, qk, qk, v_t, lam_vecs, da_norm.reshape(DA_V_DIM, 1))


def _attn_proj_body(x_ref, g_ref, wqk_ref, wvt_ref, qk_ref, vt_ref, *, tq):
    xn = _rms(x_ref[...], g_ref[...]).astype(BF16)
    qk_ref[...] = jnp.dot(xn, wqk_ref[...], preferred_element_type=F32).astype(qk_ref.dtype)
    for j in range(vt_ref.shape[0]):
        vt_ref[j] = lax.dot_general(wvt_ref[...], xn[j * tq:(j + 1) * tq], (((1,), (1,)), ((), ())),
                                    preferred_element_type=F32).astype(vt_ref.dtype)


def _attn_proj(x, gain, w_qk, w_vt, tq):
    n, d = x.shape
    tm = min(ROW_TM, n)
    nb = tm // tq
    return pl.pallas_call(
        functools.partial(_attn_proj_body, tq=tq),
        grid=(n // tm,),
        in_specs=[pl.BlockSpec((tm, d), lambda i: (i, 0)),
                  pl.BlockSpec((1, d), lambda i: (0, 0)),
                  pl.BlockSpec((d, 2 * DA_WIDTH), lambda i: (0, 0)),
                  pl.BlockSpec((DA_WIDTH, d), lambda i: (0, 0))],
        out_specs=[pl.BlockSpec((tm, 2 * DA_WIDTH), lambda i: (i, 0)),
                   pl.BlockSpec((nb, DA_WIDTH, tq), lambda i: (i, 0, 0))],
        out_shape=[jax.ShapeDtypeStruct((n, 2 * DA_WIDTH), BF16),
                   jax.ShapeDtypeStruct((n // tq, DA_WIDTH, tq), BF16)],
        compiler_params=_cparams(("parallel",)),
    )(x, gain.reshape(1, d), w_qk, w_vt)


def _outproj_body(h_ref, odn_ref, oda_ref, w_ref, o_ref):
    o_ref[...] = (h_ref[...]
                  + jnp.dot(odn_ref[...], w_ref[0:DN_WIDTH, :], preferred_element_type=F32)
                  + jnp.dot(oda_ref[...], w_ref[DN_WIDTH:DN_WIDTH + DA_WIDTH, :],
                            preferred_element_type=F32))


def _outproj(h, o_dn, o_da, w_out):
    n, d = h.shape
    tm = min(ROW_TM, n)
    return pl.pallas_call(
        _outproj_body,
        grid=(n // tm,),
        in_specs=[pl.BlockSpec((tm, d), lambda i: (i, 0)),
                  pl.BlockSpec((tm, DN_WIDTH), lambda i: (i, 0)),
                  pl.BlockSpec((tm, DA_WIDTH), lambda i: (i, 0)),
                  pl.BlockSpec((DN_WIDTH + DA_WIDTH, d), lambda i: (0, 0))],
        out_specs=pl.BlockSpec((tm, d), lambda i: (i, 0)),
        out_shape=jax.ShapeDtypeStruct((n, d), F32),
        compiler_params=_cparams(("parallel",)),
    )(h, o_dn, o_da, w_out)


def _ffn_body(x_ref, g_ref, w1_ref, w3_ref, w2_ref, o_ref, xn_ref):
    @pl.when(pl.program_id(1) == 0)
    def _():
        x = x_ref[...]
        xn_ref[...] = _rms(x, g_ref[...]).astype(BF16)
        o_ref[...] = x

    xn = xn_ref[...]
    g = jnp.dot(xn, w1_ref[...], preferred_element_type=F32)
    u = jnp.dot(xn, w3_ref[...], preferred_element_type=F32)
    a = (_silu(g) * u).astype(BF16)
    o_ref[...] += jnp.dot(a, w2_ref[...], preferred_element_type=F32)


def _ffn(x, gain, w1, w3, w2):
    n, d = x.shape
    dff = w1.shape[1]
    tm = min(FFN_TM, n)
    tf = FFN_TF
    return pl.pallas_call(
        _ffn_body,
        grid=(n // tm, dff // tf),
        in_specs=[pl.BlockSpec((tm, d), lambda i, f: (i, 0)),
                  pl.BlockSpec((1, d), lambda i, f: (0, 0)),
                  pl.BlockSpec((d, tf), lambda i, f: (0, f)),
                  pl.BlockSpec((d, tf), lambda i, f: (0, f)),
                  pl.BlockSpec((tf, d), lambda i, f: (f, 0))],
        out_specs=pl.BlockSpec((tm, d), lambda i, f: (i, 0)),
        out_shape=jax.ShapeDtypeStruct((n, d), F32),
        scratch_shapes=[pltpu.VMEM((tm, d), BF16)],
        compiler_params=_cparams(("parallel", "arbitrary")),
    )(x, gain.reshape(1, d), w1, w3, w2)


def _router_body(x_ref, g_ref, wt_ref, idx_ref, gate_ref):
    xn = _rms(x_ref[...], g_ref[...])
    x1, x2, x3 = _split3(xn)
    w1, w2, w3 = _split3(wt_ref[...])
    nt = lambda a, b: lax.dot_general(a, b, (((1,), (1,)), ((), ())), preferred_element_type=F32)
    logits = (nt(w3, x1) + nt(w2, x2) + nt(w1, x3)) + (nt(w2, x1) + nt(w1, x2)) + nt(w1, x1)
    e = lax.broadcasted_iota(jnp.int32, logits.shape, 0)
    m0 = jnp.max(logits, axis=0, keepdims=True)
    i0 = jnp.min(jnp.where(logits == m0, e, N_EXPERTS), axis=0, keepdims=True)
    rest = jnp.where(e == i0, -jnp.inf, logits)
    m1 = jnp.max(rest, axis=0, keepdims=True)
    i1 = jnp.min(jnp.where(rest == m1, e, N_EXPERTS), axis=0, keepdims=True)
    t = jnp.exp(m1 - m0)
    idx_ref[0:1, :] = i0
    idx_ref[1:2, :] = i1
    gate_ref[0:1, :] = 1.0 / (1.0 + t)
    gate_ref[1:2, :] = t / (1.0 + t)


def _router(x, gain, w_router):
    n, d = x.shape
    tm = min(ROW_TM, n)
    return pl.pallas_call(
        _router_body,
        grid=(n // tm,),
        in_specs=[pl.BlockSpec((tm, d), lambda i: (i, 0)),
                  pl.BlockSpec((1, d), lambda i: (0, 0)),
                  pl.BlockSpec((N_EXPERTS, d), lambda i: (0, 0))],
        out_specs=[pl.BlockSpec((TOP_K, tm), lambda i: (0, i)),
                   pl.BlockSpec((TOP_K, tm), lambda i: (0, i))],
        out_shape=[jax.ShapeDtypeStruct((TOP_K, n), jnp.int32),
                   jax.ShapeDtypeStruct((TOP_K, n), F32)],
        compiler_params=_cparams(("parallel",)),
    )(x, gain.reshape(1, d), w_router.T)


def _experts_body(be_ref, tok_ref, dst_ref, nb_ref, x_hbm, g_ref, gate_ref, w1_ref, w3_ref, w2_ref,
                  y_hbm, xg_ref, xn_ref, acc_ref, gsem, ssem, *, tm, nf):
    b = pl.program_id(0)
    f = pl.program_id(1)
    nb = nb_ref[0]
    slot = b & 1
    other = 1 - slot
    per_step = tm // nf

    def gather(blk, row, buf):
        return pltpu.make_async_copy(x_hbm.at[pl.ds(tok_ref[blk * tm + row], 1), :],
                                     xg_ref.at[buf, pl.ds(row, 1), :], gsem.at[buf])

    def scatter(blk, row, buf):
        return pltpu.make_async_copy(acc_ref.at[buf, pl.ds(row, 1), :],
                                     y_hbm.at[pl.ds(dst_ref[(blk + 1) * tm + row], 1), :], ssem.at[buf])

    def for_rows(fn):
        def step(i, carry):
            fn(i)
            return carry
        lax.fori_loop(0, tm, step, 0)

    used = b < nb

    @pl.when(jnp.logical_and(f == 0, b == 0))
    def _():
        for_rows(lambda i: gather(0, i, 0).start())
        acc_ref[1] = jnp.zeros((tm, acc_ref.shape[2]), F32)

    @pl.when(jnp.logical_and(f == 0, b <= nb))
    def _():
        for_rows(lambda i: gather(b, i, slot).wait())

    @pl.when(jnp.logical_and(f == 0, jnp.logical_and(b <= nb, b >= 1)))
    def _():
        for_rows(lambda i: scatter(b - 2, i, slot).wait())

    @pl.when(jnp.logical_and(f == 0, used))
    def _():
        xn_ref[...] = _rms(xg_ref[slot], g_ref[...]).astype(BF16)
        acc_ref[slot] = jnp.zeros((tm, acc_ref.shape[2]), F32)

    @pl.when(used)
    def _():
        base = f * per_step
        for t in range(per_step):
            gather(b + 1, base + t, other).start()
            scatter(b - 1, base + t, other).start()
        xn = xn_ref[...]
        g = jnp.dot(xn, w1_ref[...], preferred_element_type=F32)
        u = jnp.dot(xn, w3_ref[...], preferred_element_type=F32)
        a = (_silu(g) * u).astype(BF16)
        acc_ref[slot] += jnp.dot(a, w2_ref[...], preferred_element_type=F32)

    @pl.when(jnp.logical_and(f == nf - 1, used))
    def _():
        acc_ref[slot] = acc_ref[slot] * gate_ref[...]

    @pl.when(jnp.logical_and(f == 0, b == nb))
    def _():
        for_rows(lambda i: scatter(b - 1, i, other).start())
        for_rows(lambda i: scatter(b - 1, i, other).wait())

    @pl.when(jnp.logical_and(f == 0, b >= nb))
    def _():
        acc_ref[0] = jnp.zeros((tm, acc_ref.shape[2]), F32)
        for_rows(lambda i: scatter(b, i, 0).start())
        for_rows(lambda i: scatter(b, i, 0).wait())


def _experts(x, gain, w1, w3, w2, plan, tm):
    block_e, row_tok, row_dst, row_gate, n_used, n_grid = plan
    n, d = x.shape
    dff = w1.shape[2]
    tf = MOE_TF
    nf = dff // tf

    def widx(b, f, be, tok, dst, nb):
        live = b < nb[0]
        last = jnp.maximum(nb[0] - 1, 0)
        return jnp.where(live, be[b], be[last]), jnp.where(live, f, nf - 1)

    def w13_map(b, f, be, tok, dst, nb):
        e, ff = widx(b, f, be, tok, dst, nb)
        return e, 0, ff

    def w2_map(b, f, be, tok, dst, nb):
        e, ff = widx(b, f, be, tok, dst, nb)
        return e, ff, 0

    grid_spec = pltpu.PrefetchScalarGridSpec(
        num_scalar_prefetch=4,
        grid=(n_grid, nf),
        in_specs=[pl.BlockSpec(memory_space=pl.ANY),
                  pl.BlockSpec((1, d), lambda b, f, *_: (0, 0)),
                  pl.BlockSpec((tm, 1), lambda b, f, *_: (b, 0)),
                  pl.BlockSpec((None, d, tf), w13_map),
                  pl.BlockSpec((None, d, tf), w13_map),
                  pl.BlockSpec((None, tf, d), w2_map)],
        out_specs=pl.BlockSpec(memory_space=pl.ANY),
        scratch_shapes=[pltpu.VMEM((2, tm, d), F32), pltpu.VMEM((tm, d), BF16),
                        pltpu.VMEM((2, tm, d), F32),
                        pltpu.SemaphoreType.DMA((2,)), pltpu.SemaphoreType.DMA((2,))],
    )
    n_y = (n_grid + 1) * tm
    return pl.pallas_call(
        functools.partial(_experts_body, tm=tm, nf=nf),
        grid_spec=grid_spec,
        out_shape=jax.ShapeDtypeStruct((n_y, d), F32),
        compiler_params=_cparams(("arbitrary", "arbitrary")),
    )(block_e, row_tok, row_dst, n_used, x, gain.reshape(1, d), row_gate, w1, w3, w2)


def _route_plan(idx, gates, n, tm):
    n_assign = n * TOP_K
    flat_e = idx.T.reshape(n_assign)
    onehot = (flat_e[:, None] == jnp.arange(N_EXPERTS, dtype=jnp.int32)[None, :]).astype(jnp.int32)
    csum = jnp.cumsum(onehot, axis=0)
    counts = csum[-1]
    rank = jnp.sum((csum - onehot) * onehot, axis=1)
    padded = (counts + tm - 1) // tm * tm
    pad_end = jnp.cumsum(padded)
    pad_start = pad_end - padded
    dest = pad_start[flat_e] + rank
    n_blocks = -(-n_assign // tm) + N_EXPERTS
    n_grid = n_blocks + 1
    n_rows = n_grid * tm
    a = jnp.arange(n_assign, dtype=jnp.int32)
    row_tok = jnp.zeros((n_rows,), jnp.int32).at[dest].set(a // TOP_K)
    row_gate = jnp.zeros((n_rows,), F32).at[dest].set(gates.T.reshape(n_assign))
    real_dst = jnp.full((n_rows + tm,), -1, jnp.int32).at[tm + dest].set((a % TOP_K) * n + a // TOP_K)
    is_pad = real_dst < 0
    row_dst = jnp.where(is_pad, n_assign + jnp.cumsum(is_pad.astype(jnp.int32)) - 1, real_dst)
    block_e = jnp.minimum(
        jnp.searchsorted(pad_end, jnp.arange(n_grid, dtype=jnp.int32) * tm, side='right'),
        N_EXPERTS - 1).astype(jnp.int32)
    n_used = (pad_end[-1] // tm).astype(jnp.int32).reshape(1)
    return block_e, row_tok, row_dst, row_gate.reshape(n_rows, 1), n_used, n_grid


def _ple_body(*refs, final, combine):
    if combine:
        h_ref, y0_ref, y1_ref, p_ref, wp_ref, pn_ref, wg_ref, nf_ref, o_ref = refs
        h = h_ref[...] + (y0_ref[...] + y1_ref[...])
    else:
        h_ref, p_ref, wp_ref, pn_ref, wg_ref, nf_ref, o_ref = refs
        h = h_ref[...]
    e = _rms(_mm(p_ref[...], wp_ref[...]), pn_ref[...])
    out = h + _sigmoid(_mm(h, wg_ref[...])) * e
    if final:
        out = _rms(out, nf_ref[...])
    o_ref[...] = out


def _ple(h, p, w_proj, p_norm, w_gate, norm_final, final, y=None):
    n, d = h.shape
    dp = p.shape[1]
    tm = min(ROW_TM, n)
    row = lambda i: (i, 0)
    fixed = lambda i: (0, 0)
    specs = [pl.BlockSpec((tm, d), row)]
    args = [h]
    if y is not None:
        specs += [pl.BlockSpec((tm, d), row), pl.BlockSpec((tm, d), lambda i: (n // tm + i, 0))]
        args += [y, y]
    specs += [pl.BlockSpec((tm, dp), row), pl.BlockSpec((dp, d), fixed), pl.BlockSpec((1, d), fixed),
              pl.BlockSpec((d, d), fixed), pl.BlockSpec((1, d), fixed)]
    args += [p, w_proj, p_norm.reshape(1, d), w_gate, norm_final.reshape(1, d)]
    return pl.pallas_call(
        functools.partial(_ple_body, final=final, combine=y is not None),
        grid=(n // tm,),
        in_specs=specs,
        out_specs=pl.BlockSpec((tm, d), row),
        out_shape=jax.ShapeDtypeStruct((n, d), F32),
        compiler_params=_cparams(("parallel",)),
    )(*args)


def kernel(x, p, norm_mix, w_in, conv_w, dn_a_log, dn_dt_bias, dn_norm, da_lambda_q1, da_lambda_k1, da_lambda_q2, da_lambda_k2, da_norm, w_out, norm_ffn, ffn_w1, ffn_w3, ffn_w2, moe_router, moe_w1, moe_w3, moe_w2, ple_proj, ple_norm, ple_gate, norm_final):
    batch, seq, d = x.shape
    n = batch * seq
    depth = w_in.shape[0]
    h = x.reshape(n, d)
    wq = 3 * DN_WIDTH
    for i in range(depth):
        wi = w_in[i]
        c0 = wq + DN_WIDTH
        c1 = c0 + 2 * DN_HEADS
        pad = jnp.zeros((d, LANES - 2 * DN_HEADS), F32)
        w_a = jnp.concatenate([wi[:, :c1], pad], axis=1).astype(BF16)
        att_scale = DA_HEAD_DIM ** -0.5
        w_qk = jnp.concatenate([wi[:, c1:c1 + DA_WIDTH] * att_scale,
                                wi[:, c1 + DA_WIDTH:c1 + 2 * DA_WIDTH]], axis=1).astype(BF16)
        w_vt = wi[:, c1 + 2 * DA_WIDTH:].T.astype(BF16)
        proj_a = _norm_matmul(h, norm_mix[i], w_a, F32, (c0 + LANES) // 3)
        qk, v_t = _attn_proj(h, norm_mix[i], w_qk, w_vt, min(ATT_TQ, seq))

        o_dn = _deltanet(proj_a, conv_w[i], dn_dt_bias[i], dn_a_log[i], dn_norm[i], batch, seq)
        lam_init = 0.8 - 0.6 * math.exp(-0.3 * i)
        lam_vecs = jnp.stack([da_lambda_q1[i], da_lambda_k1[i], da_lambda_q2[i], da_lambda_k2[i]])
        o_da = _diffattn(qk, v_t, lam_vecs, da_norm[i], lam_init, batch, seq)
        h = _outproj(h, o_dn, o_da, w_out[i].astype(BF16))

        j = i // 2
        last = i == depth - 1
        wp = ple_proj[i].astype(BF16)
        wg = ple_gate[i].astype(BF16)
        if i % 2 == 0:
            h = _ffn(h, norm_ffn[i], ffn_w1[j].astype(BF16), ffn_w3[j].astype(BF16),
                     ffn_w2[j].astype(BF16))
            h = _ple(h, p[i].reshape(n, -1), wp, ple_norm[i], wg, norm_final, last)
        else:
            idx, gates = _router(h, norm_ffn[i], moe_router[j])
            tm = min(MOE_TM, n)
            plan = _route_plan(idx, gates, n, tm)
            y = _experts(h, norm_ffn[i], moe_w1[j].astype(BF16), moe_w3[j].astype(BF16),
                         moe_w2[j].astype(BF16), plan, tm)
            h = _ple(h, p[i].reshape(n, -1), wp, ple_norm[i], wg, norm_final, last, y=y)
    return h.reshape(batch, seq, d)
```

```python
import functools
import math

import jax
import jax.numpy as jnp
from jax import lax
from jax.experimental import pallas as pl
from jax.experimental.pallas import tpu as pltpu

F32 = jnp.float32
BF16 = jnp.bfloat16
EPS = 1e-6

CHUNK = 64
DN_HEADS = 8
DN_HEAD_DIM = 128
DN_WIDTH = DN_HEADS * DN_HEAD_DIM
CONV_K = 4
DA_HEADS = 8
DA_HEAD_DIM = 64
DA_V_DIM = 2 * DA_HEAD_DIM
DA_WIDTH = DA_HEADS * DA_V_DIM
N_EXPERTS = 8
TOP_K = 2
LANES = 128
SUBLANES = 8
VMEM_LIMIT = 56 * 1024 * 1024

PROJ_TM = 512
FFN_TM = 512
FFN_TF = 512
MOE_TM = 512
MOE_TF = 1024
ATT_TQ = 256
ROW_TM = 512
HALO = SUBLANES


def _cparams(sem):
    return pltpu.CompilerParams(dimension_semantics=sem, vmem_limit_bytes=VMEM_LIMIT)


def _sigmoid(x):
    return 1.0 / (1.0 + jnp.exp(-x))


def _silu(x):
    return x * _sigmoid(x)


def _softplus(x):
    return jnp.maximum(x, 0.0) + jnp.log1p(jnp.exp(-jnp.abs(x)))


def _rms(x, gain):
    y = x * lax.rsqrt(jnp.mean(x * x, axis=-1, keepdims=True) + EPS)
    return y * gain


def _mm(a, b):
    return jnp.dot(a.astype(BF16), b.astype(BF16), preferred_element_type=F32)


def _mm_nt(a, b):
    return lax.dot_general(a.astype(BF16), b.astype(BF16), (((1,), (1,)), ((), ())),
                           preferred_element_type=F32)


def _mm_tn(a, b):
    return lax.dot_general(a.astype(BF16), b.astype(BF16), (((0,), (0,)), ((), ())),
                           preferred_element_type=F32)


def _split3(a):
    a1 = a.astype(BF16)
    r1 = a - a1.astype(F32)
    a2 = r1.astype(BF16)
    a3 = (r1 - a2.astype(F32)).astype(BF16)
    return a1, a2, a3


def _norm_matmul_body(x_ref, g_ref, w_ref, o_ref, xn_ref):
    @pl.when(pl.program_id(1) == 0)
    def _():
        xn_ref[...] = _rms(x_ref[...], g_ref[...]).astype(BF16)

    o_ref[...] = jnp.dot(xn_ref[...], w_ref[...], preferred_element_type=F32).astype(o_ref.dtype)


def _norm_matmul(x, gain, w, out_dtype, tn):
    n, d = x.shape
    nout = w.shape[1]
    tm = min(PROJ_TM, n)
    return pl.pallas_call(
        _norm_matmul_body,
        grid=(n // tm, nout // tn),
        in_specs=[pl.BlockSpec((tm, d), lambda i, j: (i, 0)),
                  pl.BlockSpec((1, d), lambda i, j: (0, 0)),
                  pl.BlockSpec((d, tn), lambda i, j: (0, j))],
        out_specs=pl.BlockSpec((tm, tn), lambda i, j: (i, j)),
        out_shape=jax.ShapeDtypeStruct((n, nout), out_dtype),
        scratch_shapes=[pltpu.VMEM((tm, d), BF16)],
        compiler_params=_cparams(("parallel", "arbitrary")),
    )(x, gain.reshape(1, d), w)


def _unit_lower_inverse(ms, r, c):
    eye = (r == c).astype(F32)
    bd16 = (r >> 4) == (c >> 4)
    in32 = ((r >> 5) == (c >> 5)) & ((r >> 4) > (c >> 4))
    in64 = (r >> 5) > (c >> 5)
    xs = [jnp.where(bd16, -m, 0.0) for m in ms]
    ps = [eye + x for x in xs]
    xs = [_mm(x, x) for x in xs]
    for _ in range(2):
        ts = [_mm(jnp.concatenate([p, x], axis=0), x) for p, x in zip(ps, xs)]
        ps = [p + t[0:CHUNK] for p, t in zip(ps, ts)]
        xs = [t[CHUNK:2 * CHUNK] for t in ts]
    ps = [p + _mm(p, x) for p, x in zip(ps, xs)]
    for sel in (in32, in64):
        ls = [_mm(p, jnp.where(sel, m, 0.0)) for p, m in zip(ps, ms)]
        ps = [p - _mm(l, p) for p, l in zip(ps, ls)]
    return ps


def _deltanet_body(qkv_ref, gate_ref, ba_ref, convw_ref, hp_ref, dnorm_ref, o_ref, ext_ref, s_ref):
    ci = pl.program_id(1)

    @pl.when(ci == 0)
    def _():
        ext_ref[0:HALO, :] = jnp.zeros((HALO, 3 * DN_WIDTH), F32)
        s_ref[...] = jnp.zeros_like(s_ref)

    ext_ref[HALO:HALO + CHUNK, :] = qkv_ref[...]
    conv = convw_ref[0:1, :] * ext_ref[HALO - 3:HALO - 3 + CHUNK, :]
    for j in range(1, CONV_K):
        conv = conv + convw_ref[j:j + 1, :] * ext_ref[HALO - 3 + j:HALO - 3 + j + CHUNK, :]
    ext_ref[0:HALO, :] = ext_ref[CHUNK:CHUNK + HALO, :]
    act = _silu(conv)

    ba = ba_ref[...]
    beta_all = _sigmoid(ba)
    g_all = -jnp.exp(hp_ref[1:2, :]) * _softplus(ba + hp_ref[0:1, :])
    r = lax.broadcasted_iota(jnp.int32, (CHUNK, CHUNK), 0)
    c = lax.broadcasted_iota(jnp.int32, (CHUNK, CHUNK), 1)
    incl = r >= c
    strict = r > c
    tri = incl.astype(BF16)
    g1, g2, g3 = _split3(g_all)
    gcum_all = (jnp.dot(tri, g1, preferred_element_type=F32)
                + jnp.dot(tri, g2, preferred_element_type=F32)
                + jnp.dot(tri, g3, preferred_element_type=F32))
    gcum_t = gcum_all.T
    dnorm = dnorm_ref[...]
    scale = DN_HEAD_DIM ** -0.5
    heads = range(DN_HEADS)

    def l2n(x):
        return x * lax.rsqrt(jnp.sum(x * x, axis=-1, keepdims=True) + EPS)

    qs, ks, vs = [], [], []
    for h in heads:
        lo = h * DN_HEAD_DIM
        qs.append(l2n(act[:, lo:lo + DN_HEAD_DIM]) * scale)
        ks.append(l2n(act[:, DN_WIDTH + lo:DN_WIDTH + lo + DN_HEAD_DIM]))
        vs.append(act[:, 2 * DN_WIDTH + lo:2 * DN_WIDTH + lo + DN_HEAD_DIM])
    betas = [beta_all[:, h:h + 1] for h in heads]
    gcs = [gcum_all[:, DN_HEADS + h:DN_HEADS + h + 1] for h in heads]
    g_lasts = [gcum_all[CHUNK - 1:CHUNK, DN_HEADS + h:DN_HEADS + h + 1] for h in heads]
    decays = [jnp.where(incl, jnp.exp(jnp.where(
        incl, gcs[h] - gcum_t[DN_HEADS + h:DN_HEADS + h + 1, :], 0.0)), 0.0) for h in heads]
    kbs = [ks[h] * betas[h] for h in heads]

    kq = [_mm_nt(jnp.concatenate([kbs[h], qs[h]], axis=0), ks[h]) for h in heads]
    ms = [jnp.where(strict, kq[h][0:CHUNK] * decays[h], 0.0) for h in heads]
    attns = [jnp.where(incl, kq[h][CHUNK:2 * CHUNK] * decays[h], 0.0) for h in heads]
    ts = _unit_lower_inverse(ms, r, c)
    uw = [_mm(ts[h], jnp.concatenate([vs[h] * betas[h], kbs[h] * jnp.exp(gcs[h])], axis=1))
          for h in heads]
    states = [s_ref[h] for h in heads]
    ws = [_mm(jnp.concatenate([uw[h][:, DN_HEAD_DIM:], qs[h] * jnp.exp(gcs[h])], axis=0), states[h])
          for h in heads]
    v_news = [uw[h][:, 0:DN_HEAD_DIM] - ws[h][0:CHUNK] for h in heads]
    intra = [_mm(attns[h], v_news[h]) for h in heads]
    upd = [_mm_tn(ks[h] * jnp.exp(g_lasts[h] - gcs[h]), v_news[h]) for h in heads]
    for h in heads:
        lo = h * DN_HEAD_DIM
        s_ref[h] = states[h] * jnp.exp(g_lasts[h]) + upd[h]
        o = ws[h][CHUNK:2 * CHUNK] + intra[h]
        gt = gate_ref[:, lo:lo + DN_HEAD_DIM]
        o_ref[:, lo:lo + DN_HEAD_DIM] = (_rms(o, dnorm) * _silu(gt)).astype(o_ref.dtype)


def _deltanet(proj_a, conv_w, dt_bias, a_log, dn_norm, batch, seq):
    n = batch * seq
    nc = seq // CHUNK
    wq = 3 * DN_WIDTH
    hp = jnp.zeros((SUBLANES, LANES), F32)
    hp = hp.at[0, DN_HEADS:2 * DN_HEADS].set(dt_bias).at[1, DN_HEADS:2 * DN_HEADS].set(a_log)
    row = lambda b, ci: b * nc + ci
    return pl.pallas_call(
        _deltanet_body,
        grid=(batch, nc),
        in_specs=[pl.BlockSpec((CHUNK, wq), lambda b, ci: (row(b, ci), 0)),
                  pl.BlockSpec((CHUNK, DN_WIDTH), lambda b, ci: (row(b, ci), wq // DN_WIDTH)),
                  pl.BlockSpec((CHUNK, LANES), lambda b, ci: (row(b, ci), (wq + DN_WIDTH) // LANES)),
                  pl.BlockSpec((CONV_K, wq), lambda b, ci: (0, 0)),
                  pl.BlockSpec((SUBLANES, LANES), lambda b, ci: (0, 0)),
                  pl.BlockSpec((1, DN_HEAD_DIM), lambda b, ci: (0, 0))],
        out_specs=pl.BlockSpec((CHUNK, DN_WIDTH), lambda b, ci: (row(b, ci), 0)),
        out_shape=jax.ShapeDtypeStruct((n, DN_WIDTH), BF16),
        scratch_shapes=[pltpu.VMEM((CHUNK + HALO, wq), F32),
                        pltpu.VMEM((DN_HEADS, DN_HEAD_DIM, DN_HEAD_DIM), F32)],
        compiler_params=_cparams(("parallel", "arbitrary")),
    )(proj_a, proj_a, proj_a, conv_w, hp, dn_norm.reshape(1, DN_HEAD_DIM))


def _diffattn_body(slopes_ref, q_ref, k_ref, vt_ref, lamv_ref, danorm_ref, o_ref,
                   m_ref, l_ref, acc_ref, nrc_ref, dbias_ref, *, tq, lam_init):
    h = pl.program_id(1)
    qi = pl.program_id(2)
    slope = slopes_ref[h]
    q = q_ref[...]
    lane = lax.broadcasted_iota(jnp.int32, q.shape, 1)
    zero = jnp.zeros_like(q)
    qq = jnp.concatenate([jnp.where(lane < DA_HEAD_DIM, q, zero),
                          jnp.where(lane >= DA_HEAD_DIM, q, zero)], axis=0)
    cols = 2 * tq

    @pl.when(qi == 0)
    def _():
        kr = lax.broadcasted_iota(jnp.int32, (tq, cols), 0)
        qc = lax.broadcasted_iota(jnp.int32, (tq, cols), 1) & (tq - 1)
        rc = (qc - kr).astype(F32)
        nrc_ref[...] = -slope * rc
        dbias_ref[...] = jnp.where((kr >> 6) <= (qc >> 6), -slope * jnp.abs(rc), -jnp.inf)

    def logits_t(j):
        kb = k_ref[pl.ds(pl.multiple_of(j * tq, tq), tq), :]
        return _mm_nt(kb, qq)

    def pv_update(p, a, j):
        acc_ref[...] = a * acc_ref[...] + jnp.dot(vt_ref[j], p, preferred_element_type=F32)

    def biased(j, bias_ref):
        z = logits_t(j) + bias_ref[...]
        return z, jnp.max(z, axis=0, keepdims=True)

    first = biased(0, nrc_ref)
    z0, m0 = biased(qi, dbias_ref)
    p0 = jnp.exp(z0 - m0)
    m_ref[...] = m0
    l_ref[...] = jnp.sum(p0, axis=0, keepdims=True)
    acc_ref[...] = jnp.zeros_like(acc_ref)

    def body(kj, carry):
        z_cur, zmax_cur, p_prev, a_prev = carry
        pv_update(p_prev, a_prev, jnp.where(kj == 0, qi, kj - 1))
        z_next, zmax_next = biased(jnp.minimum(kj + 1, qi), nrc_ref)
        shift = slope * ((qi - kj) * tq).astype(F32)
        m_old = m_ref[...]
        m_new = jnp.maximum(m_old, zmax_cur - shift)
        p = jnp.exp(z_cur - (m_new + shift))
        scale = jnp.exp(m_old - m_new)
        l_ref[...] = scale * l_ref[...] + jnp.sum(p, axis=0, keepdims=True)
        m_ref[...] = m_new
        return z_next, zmax_next, p.astype(BF16), scale

    _, _, p_last, a_last = lax.fori_loop(
        0, qi, body, first + (p0.astype(BF16), jnp.ones((1, cols), F32)))
    pv_update(p_last, a_last, jnp.where(qi == 0, qi, qi - 1))

    lv = lamv_ref[...]
    lam = (jnp.exp(jnp.sum(lv[0:1] * lv[1:2], axis=-1, keepdims=True))
           - jnp.exp(jnp.sum(lv[2:3] * lv[3:4], axis=-1, keepdims=True)) + lam_init)
    o1 = acc_ref[:, 0:tq] / l_ref[:, 0:tq]
    o2 = acc_ref[:, tq:2 * tq] / l_ref[:, tq:2 * tq]
    o = o1 - lam * o2
    o = o * lax.rsqrt(jnp.mean(o * o, axis=0, keepdims=True) + EPS) * danorm_ref[...]
    o_ref[...] = (o * (1.0 - lam_init)).T.astype(o_ref.dtype)


def _diffattn(qk, v_t, lam_vecs, da_norm, lam_init, batch, seq):
    n = batch * seq
    tq = v_t.shape[2]
    nq = seq // tq
    slopes = jnp.exp2(-8.0 * jnp.arange(1, DA_HEADS + 1, dtype=F32) / DA_HEADS)
    grid_spec = pltpu.PrefetchScalarGridSpec(
        num_scalar_prefetch=1,
        grid=(batch, DA_HEADS, nq),
        in_specs=[pl.BlockSpec((tq, DA_V_DIM), lambda b, h, qi, sl: (b * nq + qi, h)),
                  pl.BlockSpec((seq, DA_V_DIM), lambda b, h, qi, sl: (b, DA_HEADS + h)),
                  pl.BlockSpec((nq, DA_V_DIM, tq), lambda b, h, qi, sl: (b, h, 0)),
                  pl.BlockSpec((4, DA_HEAD_DIM), lambda b, h, qi, sl: (0, 0)),
                  pl.BlockSpec((DA_V_DIM, 1), lambda b, h, qi, sl: (0, 0))],
        out_specs=pl.BlockSpec((tq, DA_V_DIM), lambda b, h, qi, sl: (b * nq + qi, h)),
        scratch_shapes=[pltpu.VMEM((1, 2 * tq), F32), pltpu.VMEM((1, 2 * tq), F32),
                        pltpu.VMEM((DA_V_DIM, 2 * tq), F32), pltpu.VMEM((tq, 2 * tq), F32),
                        pltpu.VMEM((tq, 2 * tq), F32)],
    )
    return pl.pallas_call(
        functools.partial(_diffattn_body, tq=tq, lam_init=lam_init),
        grid_spec=grid_spec,
        out_shape=jax.ShapeDtypeStruct((n, DA_WIDTH), BF16),
        compiler_params=_cparams(("parallel", "parallel", "arbitrary")),
    )(slopes, qk, qk, v_t, lam_vecs, da_norm.reshape(DA_V_DIM, 1))


def _attn_proj_body(x_ref, g_ref, wqk_ref, wvt_ref, qk_ref, vt_ref, *, tq):
    xn = _rms(x_ref[...], g_ref[...]).astype(BF16)
    qk_ref[...] = jnp.dot(xn, wqk_ref[...], preferred_element_type=F32).astype(qk_ref.dtype)
    for j in range(vt_ref.shape[0]):
        vt_ref[j] = lax.dot_general(wvt_ref[...], xn[j * tq:(j + 1) * tq], (((1,), (1,)), ((), ())),
                                    preferred_element_type=F32).astype(vt_ref.dtype)


def _attn_proj(x, gain, w_qk, w_vt, tq):
    n, d = x.shape
    tm = min(ROW_TM, n)
    nb = tm // tq
    return pl.pallas_call(
        functools.partial(_attn_proj_body, tq=tq),
        grid=(n // tm,),
        in_specs=[pl.BlockSpec((tm, d), lambda i: (i, 0)),
                  pl.BlockSpec((1, d), lambda i: (0, 0)),
                  pl.BlockSpec((d, 2 * DA_WIDTH), lambda i: (0, 0)),
                  pl.BlockSpec((DA_WIDTH, d), lambda i: (0, 0))],
        out_specs=[pl.BlockSpec((tm, 2 * DA_WIDTH), lambda i: (i, 0)),
                   pl.BlockSpec((nb, DA_WIDTH, tq), lambda i: (i, 0, 0))],
        out_shape=[jax.ShapeDtypeStruct((n, 2 * DA_WIDTH), BF16),
                   jax.ShapeDtypeStruct((n // tq, DA_WIDTH, tq), BF16)],
        compiler_params=_cparams(("parallel",)),
    )(x, gain.reshape(1, d), w_qk, w_vt)


def _outproj_body(h_ref, odn_ref, oda_ref, w_ref, o_ref):
    o_ref[...] = (h_ref[...]
                  + jnp.dot(odn_ref[...], w_ref[0:DN_WIDTH, :], preferred_element_type=F32)
                  + jnp.dot(oda_ref[...], w_ref[DN_WIDTH:DN_WIDTH + DA_WIDTH, :],
                            preferred_element_type=F32))


def _outproj(h, o_dn, o_da, w_out):
    n, d = h.shape
    tm = min(ROW_TM, n)
    return pl.pallas_call(
        _outproj_body,
        grid=(n // tm,),
        in_specs=[pl.BlockSpec((tm, d), lambda i: (i, 0)),
                  pl.BlockSpec((tm, DN_WIDTH), lambda i: (i, 0)),
                  pl.BlockSpec((tm, DA_WIDTH), lambda i: (i, 0)),
                  pl.BlockSpec((DN_WIDTH + DA_WIDTH, d), lambda i: (0, 0))],
        out_specs=pl.BlockSpec((tm, d), lambda i: (i, 0)),
        out_shape=jax.ShapeDtypeStruct((n, d), F32),
        compiler_params=_cparams(("parallel",)),
    )(h, o_dn, o_da, w_out)


def _ffn_body(x_ref, g_ref, w1_ref, w3_ref, w2_ref, o_ref, xn_ref):
    @pl.when(pl.program_id(1) == 0)
    def _():
        x = x_ref[...]
        xn_ref[...] = _rms(x, g_ref[...]).astype(BF16)
        o_ref[...] = x

    xn = xn_ref[...]
    g = jnp.dot(xn, w1_ref[...], preferred_element_type=F32)
    u = jnp.dot(xn, w3_ref[...], preferred_element_type=F32)
    a = (_silu(g) * u).astype(BF16)
    o_ref[...] += jnp.dot(a, w2_ref[...], preferred_element_type=F32)


def _ffn(x, gain, w1, w3, w2):
    n, d = x.shape
    dff = w1.shape[1]
    tm = min(FFN_TM, n)
    tf = FFN_TF
    return pl.pallas_call(
        _ffn_body,
        grid=(n // tm, dff // tf),
        in_specs=[pl.BlockSpec((tm, d), lambda i, f: (i, 0)),
                  pl.BlockSpec((1, d), lambda i, f: (0, 0)),
                  pl.BlockSpec((d, tf), lambda i, f: (0, f)),
                  pl.BlockSpec((d, tf), lambda i, f: (0, f)),
                  pl.BlockSpec((tf, d), lambda i, f: (f, 0))],
        out_specs=pl.BlockSpec((tm, d), lambda i, f: (i, 0)),
        out_shape=jax.ShapeDtypeStruct((n, d), F32),
        scratch_shapes=[pltpu.VMEM((tm, d), BF16)],
        compiler_params=_cparams(("parallel", "arbitrary")),
    )(x, gain.reshape(1, d), w1, w3, w2)


def _router_body(x_ref, g_ref, wt_ref, idx_ref, gate_ref):
    xn = _rms(x_ref[...], g_ref[...])
    x1, x2, x3 = _split3(xn)
    w1, w2, w3 = _split3(wt_ref[...])
    nt = lambda a, b: lax.dot_general(a, b, (((1,), (1,)), ((), ())), preferred_element_type=F32)
    logits = (nt(w3, x1) + nt(w2, x2) + nt(w1, x3)) + (nt(w2, x1) + nt(w1, x2)) + nt(w1, x1)
    e = lax.broadcasted_iota(jnp.int32, logits.shape, 0)
    m0 = jnp.max(logits, axis=0, keepdims=True)
    i0 = jnp.min(jnp.where(logits == m0, e, N_EXPERTS), axis=0, keepdims=True)
    rest = jnp.where(e == i0, -jnp.inf, logits)
    m1 = jnp.max(rest, axis=0, keepdims=True)
    i1 = jnp.min(jnp.where(rest == m1, e, N_EXPERTS), axis=0, keepdims=True)
    t = jnp.exp(m1 - m0)
    idx_ref[0:1, :] = i0
    idx_ref[1:2, :] = i1
    gate_ref[0:1, :] = 1.0 / (1.0 + t)
    gate_ref[1:2, :] = t / (1.0 + t)


def _router(x, gain, w_router):
    n, d = x.shape
    tm = min(ROW_TM, n)
    return pl.pallas_call(
        _router_body,
        grid=(n // tm,),
        in_specs=[pl.BlockSpec((tm, d), lambda i: (i, 0)),
                  pl.BlockSpec((1, d), lambda i: (0, 0)),
                  pl.BlockSpec((N_EXPERTS, d), lambda i: (0, 0))],
        out_specs=[pl.BlockSpec((TOP_K, tm), lambda i: (0, i)),
                   pl.BlockSpec((TOP_K, tm), lambda i: (0, i))],
        out_shape=[jax.ShapeDtypeStruct((TOP_K, n), jnp.int32),
                   jax.ShapeDtypeStruct((TOP_K, n), F32)],
        compiler_params=_cparams(("parallel",)),
    )(x, gain.reshape(1, d), w_router.T)


def _experts_body(be_ref, tok_ref, dst_ref, nb_ref, x_hbm, g_ref, gate_ref, w1_ref, w3_ref, w2_ref,
                  y_hbm, xg_ref, xn_ref, acc_ref, gsem, ssem, *, tm, nf):
    b = pl.program_id(0)
    f = pl.program_id(1)
    nb = nb_ref[0]
    slot = b & 1
    other = 1 - slot
    per_step, extra = divmod(tm, nf)

    def gather(blk, row, buf):
        return pltpu.make_async_copy(x_hbm.at[pl.ds(tok_ref[blk * tm + row], 1), :],
                                     xg_ref.at[buf, pl.ds(row, 1), :], gsem.at[buf])

    def scatter(blk, row, buf):
        return pltpu.make_async_copy(acc_ref.at[buf, pl.ds(row, 1), :],
                                     y_hbm.at[pl.ds(dst_ref[(blk + 1) * tm + row], 1), :], ssem.at[buf])

    def for_rows(fn):
        def step(i, carry):
            fn(i)
            return carry
        lax.fori_loop(0, tm, step, 0)

    def wait_gather(buf):
        pltpu.make_async_copy(x_hbm.at[pl.ds(0, tm), :], xg_ref.at[buf], gsem.at[buf]).wait()

    def wait_scatter(buf):
        pltpu.make_async_copy(acc_ref.at[buf], y_hbm.at[pl.ds(0, tm), :], ssem.at[buf]).wait()

    used = b < nb

    @pl.when(jnp.logical_and(f == 0, b == 0))
    def _():
        for_rows(lambda i: gather(0, i, 0).start())
        acc_ref[1] = jnp.zeros((tm, acc_ref.shape[2]), F32)

    @pl.when(jnp.logical_and(f == 0, b <= nb))
    def _():
        wait_gather(slot)

    @pl.when(jnp.logical_and(f == 0, jnp.logical_and(b <= nb, b >= 1)))
    def _():
        wait_scatter(slot)

    @pl.when(jnp.logical_and(f == 0, used))
    def _():
        xn_ref[...] = _rms(xg_ref[slot], g_ref[...]).astype(BF16)
        acc_ref[slot] = jnp.zeros((tm, acc_ref.shape[2]), F32)

    base = f * per_step + jnp.minimum(f, extra)

    @pl.when(used)
    def _():
        for t in range(per_step):
            gather(b + 1, base + t, other).start()
            scatter(b - 1, base + t, other).start()
        xn = xn_ref[...]
        g = jnp.dot(xn, w1_ref[...], preferred_element_type=F32)
        u = jnp.dot(xn, w3_ref[...], preferred_element_type=F32)
        a = (_silu(g) * u).astype(BF16)
        acc_ref[slot] += jnp.dot(a, w2_ref[...], preferred_element_type=F32)

    @pl.when(jnp.logical_and(f < extra, used))
    def _():
        gather(b + 1, base + per_step, other).start()
        scatter(b - 1, base + per_step, other).start()

    @pl.when(jnp.logical_and(f == nf - 1, used))
    def _():
        acc_ref[slot] = acc_ref[slot] * gate_ref[...]

    @pl.when(jnp.logical_and(f == 0, b == nb))
    def _():
        for_rows(lambda i: scatter(b - 1, i, other).start())
        wait_scatter(other)

    @pl.when(jnp.logical_and(f == 0, b >= nb))
    def _():
        acc_ref[0] = jnp.zeros((tm, acc_ref.shape[2]), F32)
        for_rows(lambda i: scatter(b, i, 0).start())
        wait_scatter(0)


def _experts(x, gain, w1, w3, w2, plan, tm):
    block_e, row_tok, row_dst, row_gate, n_used, n_grid = plan
    n, d = x.shape
    dff = w1.shape[2]
    tf = MOE_TF
    nf = dff // tf

    def widx(b, f, be, tok, dst, nb):
        live = b < nb[0]
        last = jnp.maximum(nb[0] - 1, 0)
        return jnp.where(live, be[b], be[last]), jnp.where(live, f, nf - 1)

    def w13_map(b, f, be, tok, dst, nb):
        e, ff = widx(b, f, be, tok, dst, nb)
        return e, 0, ff

    def w2_map(b, f, be, tok, dst, nb):
        e, ff = widx(b, f, be, tok, dst, nb)
        return e, ff, 0

    grid_spec = pltpu.PrefetchScalarGridSpec(
        num_scalar_prefetch=4,
        grid=(n_grid, nf),
        in_specs=[pl.BlockSpec(memory_space=pl.ANY),
                  pl.BlockSpec((1, d), lambda b, f, *_: (0, 0)),
                  pl.BlockSpec((tm, 1), lambda b, f, *_: (b, 0)),
                  pl.BlockSpec((None, d, tf), w13_map),
                  pl.BlockSpec((None, d, tf), w13_map),
                  pl.BlockSpec((None, tf, d), w2_map)],
        out_specs=pl.BlockSpec(memory_space=pl.ANY),
        scratch_shapes=[pltpu.VMEM((2, tm, d), F32), pltpu.VMEM((tm, d), BF16),
                        pltpu.VMEM((2, tm, d), F32),
                        pltpu.SemaphoreType.DMA((2,)), pltpu.SemaphoreType.DMA((2,))],
    )
    n_y = (n_grid + 1) * tm
    return pl.pallas_call(
        functools.partial(_experts_body, tm=tm, nf=nf),
        grid_spec=grid_spec,
        out_shape=jax.ShapeDtypeStruct((n_y, d), F32),
        compiler_params=_cparams(("arbitrary", "arbitrary")),
    )(block_e, row_tok, row_dst, n_used, x, gain.reshape(1, d), row_gate, w1, w3, w2)


def _route_plan(idx, gates, n, tm):
    n_assign = n * TOP_K
    flat_e = idx.T.reshape(n_assign)
    onehot = (flat_e[:, None] == jnp.arange(N_EXPERTS, dtype=jnp.int32)[None, :]).astype(jnp.int32)
    csum = jnp.cumsum(onehot, axis=0)
    counts = csum[-1]
    rank = jnp.sum((csum - onehot) * onehot, axis=1)
    padded = (counts + tm - 1) // tm * tm
    pad_end = jnp.cumsum(padded)
    pad_start = pad_end - padded
    dest = pad_start[flat_e] + rank
    n_blocks = -(-n_assign // tm) + N_EXPERTS
    n_grid = n_blocks + 1
    n_rows = n_grid * tm
    row_a = jnp.full((n_rows + tm,), -1, jnp.int32).at[tm + dest].set(
        jnp.arange(n_assign, dtype=jnp.int32))
    is_pad = row_a < 0
    a = jnp.maximum(row_a, 0)
    row_tok = (a // TOP_K)[tm:]
    row_gate = jnp.where(is_pad, 0.0, gates.T.reshape(n_assign)[a])[tm:]
    row_dst = jnp.where(is_pad, n_assign + jnp.cumsum(is_pad.astype(jnp.int32)) - 1,
                        (a % TOP_K) * n + a // TOP_K)
    block_e = jnp.minimum(
        jnp.searchsorted(pad_end, jnp.arange(n_grid, dtype=jnp.int32) * tm, side='right'),
        N_EXPERTS - 1).astype(jnp.int32)
    n_used = (pad_end[-1] // tm).astype(jnp.int32).reshape(1)
    return block_e, row_tok, row_dst, row_gate.reshape(n_rows, 1), n_used, n_grid


def _ple_body(*refs, final, combine):
    if combine:
        h_ref, y0_ref, y1_ref, p_ref, wp_ref, pn_ref, wg_ref, nf_ref, o_ref = refs
        h = h_ref[...] + (y0_ref[...] + y1_ref[...])
    else:
        h_ref, p_ref, wp_ref, pn_ref, wg_ref, nf_ref, o_ref = refs
        h = h_ref[...]
    e = _rms(_mm(p_ref[...], wp_ref[...]), pn_ref[...])
    out = h + _sigmoid(_mm(h, wg_ref[...])) * e
    if final:
        out = _rms(out, nf_ref[...])
    o_ref[...] = out


def _ple(h, p, w_proj, p_norm, w_gate, norm_final, final, y=None):
    n, d = h.shape
    dp = p.shape[1]
    tm = min(ROW_TM, n)
    row = lambda i: (i, 0)
    fixed = lambda i: (0, 0)
    specs = [pl.BlockSpec((tm, d), row)]
    args = [h]
    if y is not None:
        specs += [pl.BlockSpec((tm, d), row), pl.BlockSpec((tm, d), lambda i: (n // tm + i, 0))]
        args += [y, y]
    specs += [pl.BlockSpec((tm, dp), row), pl.BlockSpec((dp, d), fixed), pl.BlockSpec((1, d), fixed),
              pl.BlockSpec((d, d), fixed), pl.BlockSpec((1, d), fixed)]
    args += [p, w_proj, p_norm.reshape(1, d), w_gate, norm_final.reshape(1, d)]
    return pl.pallas_call(
        functools.partial(_ple_body, final=final, combine=y is not None),
        grid=(n // tm,),
        in_specs=specs,
        out_specs=pl.BlockSpec((tm, d), row),
        out_shape=jax.ShapeDtypeStruct((n, d), F32),
        compiler_params=_cparams(("parallel",)),
    )(*args)


def kernel(x, p, norm_mix, w_in, conv_w, dn_a_log, dn_dt_bias, dn_norm, da_lambda_q1, da_lambda_k1, da_lambda_q2, da_lambda_k2, da_norm, w_out, norm_ffn, ffn_w1, ffn_w3, ffn_w2, moe_router, moe_w1, moe_w3, moe_w2, ple_proj, ple_norm, ple_gate, norm_final):
    batch, seq, d = x.shape
    n = batch * seq
    depth = w_in.shape[0]
    h = x.reshape(n, d)
    wq = 3 * DN_WIDTH
    for i in range(depth):
        wi = w_in[i]
        c0 = wq + DN_WIDTH
        c1 = c0 + 2 * DN_HEADS
        pad = jnp.zeros((d, LANES - 2 * DN_HEADS), F32)
        w_a = jnp.concatenate([wi[:, :c1], pad], axis=1).astype(BF16)
        att_scale = DA_HEAD_DIM ** -0.5
        w_qk = jnp.concatenate([wi[:, c1:c1 + DA_WIDTH] * att_scale,
                                wi[:, c1 + DA_WIDTH:c1 + 2 * DA_WIDTH]], axis=1).astype(BF16)
        w_vt = wi[:, c1 + 2 * DA_WIDTH:].T.astype(BF16)
        proj_a = _norm_matmul(h, norm_mix[i], w_a, F32, (c0 + LANES) // 3)
        qk, v_t = _attn_proj(h, norm_mix[i], w_qk, w_vt, min(ATT_TQ, seq))

        o_dn = _deltanet(proj_a, conv_w[i], dn_dt_bias[i], dn_a_log[i], dn_norm[i], batch, seq)
        lam_init = 0.8 - 0.6 * math.exp(-0.3 * i)
        lam_vecs = jnp.stack([da_lambda_q1[i], da_lambda_k1[i], da_lambda_q2[i], da_lambda_k2[i]])
        o_da = _diffattn(qk, v_t, lam_vecs, da_norm[i], lam_init, batch, seq)
        h = _outproj(h, o_dn, o_da, w_out[i].astype(BF16))

        j = i // 2
        last = i == depth - 1
        wp = ple_proj[i].astype(BF16)
        wg = ple_gate[i].astype(BF16)
        if i % 2 == 0:
            h = _ffn(h, norm_ffn[i], ffn_w1[j].astype(BF16), ffn_w3[j].astype(BF16),
                     ffn_w2[j].astype(BF16))
            h = _ple(h, p[i].reshape(n, -1), wp, ple_norm[i], wg, norm_final, last)
        else:
            idx, gates = _router(h, norm_ffn[i], moe_router[j])
            tm = min(MOE_TM, n)
            plan = _route_plan(idx, gates, n, tm)
            y = _experts(h, norm_ffn[i], moe_w1[j].astype(BF16), moe_w3[j].astype(BF16),
                         moe_w2[j].astype(BF16), plan, tm)
            h = _ple(h, p[i].reshape(n, -1), wp, ple_norm[i], wg, norm_final, last, y=y)
    return h.reshape(batch, seq, d)
```

```python
import functools
import math

import jax
import jax.numpy as jnp
from jax import lax
from jax.experimental import pallas as pl
from jax.experimental.pallas import tpu as pltpu

F32 = jnp.float32
BF16 = jnp.bfloat16
EPS = 1e-6

CHUNK = 64
DN_HEADS = 8
DN_HEAD_DIM = 128
DN_WIDTH = DN_HEADS * DN_HEAD_DIM
CONV_K = 4
DA_HEADS = 8
DA_HEAD_DIM = 64
DA_V_DIM = 2 * DA_HEAD_DIM
DA_WIDTH = DA_HEADS * DA_V_DIM
N_EXPERTS = 8
TOP_K = 2
LANES = 128
SUBLANES = 8
VMEM_LIMIT = 56 * 1024 * 1024

PROJ_TM = 512
FFN_TM = 512
FFN_TF = 512
MOE_TM = 512
MOE_TF = 1024
ATT_TQ = 256
ROW_TM = 512
HALO = SUBLANES


def _cparams(sem):
    return pltpu.CompilerParams(dimension_semantics=sem, vmem_limit_bytes=VMEM_LIMIT)


def _sigmoid(x):
    return 1.0 / (1.0 + jnp.exp(-x))


def _silu(x):
    return x * _sigmoid(x)


def _softplus(x):
    return jnp.maximum(x, 0.0) + jnp.log1p(jnp.exp(-jnp.abs(x)))


def _rms(x, gain):
    y = x * lax.rsqrt(jnp.mean(x * x, axis=-1, keepdims=True) + EPS)
    return y * gain


def _mm(a, b):
    return jnp.dot(a.astype(BF16), b.astype(BF16), preferred_element_type=F32)


def _mm_nt(a, b):
    return lax.dot_general(a.astype(BF16), b.astype(BF16), (((1,), (1,)), ((), ())),
                           preferred_element_type=F32)


def _mm_tn(a, b):
    return lax.dot_general(a.astype(BF16), b.astype(BF16), (((0,), (0,)), ((), ())),
                           preferred_element_type=F32)


def _split3(a):
    a1 = a.astype(BF16)
    r1 = a - a1.astype(F32)
    a2 = r1.astype(BF16)
    a3 = (r1 - a2.astype(F32)).astype(BF16)
    return a1, a2, a3


def _dn_proj_body(x_ref, g_ref, w_ref, wba_ref, o_ref, ba_ref, xn_ref):
    @pl.when(pl.program_id(1) == 0)
    def _():
        xn_ref[...] = _rms(x_ref[...], g_ref[...]).astype(BF16)
        ba_ref[...] = jnp.dot(xn_ref[...], wba_ref[...], preferred_element_type=F32)

    o_ref[...] = jnp.dot(xn_ref[...], w_ref[...], preferred_element_type=F32)


def _dn_proj(x, gain, w, w_ba, tn):
    n, d = x.shape
    nout = w.shape[1]
    tm = min(PROJ_TM, n)
    return pl.pallas_call(
        _dn_proj_body,
        grid=(n // tm, nout // tn),
        in_specs=[pl.BlockSpec((tm, d), lambda i, j: (i, 0)),
                  pl.BlockSpec((1, d), lambda i, j: (0, 0)),
                  pl.BlockSpec((d, tn), lambda i, j: (0, j)),
                  pl.BlockSpec((d, LANES), lambda i, j: (0, 0))],
        out_specs=[pl.BlockSpec((tm, tn), lambda i, j: (i, j)),
                   pl.BlockSpec((tm, LANES), lambda i, j: (i, 0))],
        out_shape=[jax.ShapeDtypeStruct((n, nout), F32), jax.ShapeDtypeStruct((n, LANES), F32)],
        scratch_shapes=[pltpu.VMEM((tm, d), BF16)],
        compiler_params=_cparams(("parallel", "arbitrary")),
    )(x, gain.reshape(1, d), w, w_ba)


def _unit_lower_inverse(ms, r, c):
    eye = (r == c).astype(F32)
    bd16 = (r >> 4) == (c >> 4)
    in32 = ((r >> 5) == (c >> 5)) & ((r >> 4) > (c >> 4))
    in64 = (r >> 5) > (c >> 5)
    xs = [jnp.where(bd16, -m, 0.0) for m in ms]
    ps = [eye + x for x in xs]
    xs = [_mm(x, x) for x in xs]
    for _ in range(2):
        ts = [_mm(jnp.concatenate([p, x], axis=0), x) for p, x in zip(ps, xs)]
        ps = [p + t[0:CHUNK] for p, t in zip(ps, ts)]
        xs = [t[CHUNK:2 * CHUNK] for t in ts]
    ps = [p + _mm(p, x) for p, x in zip(ps, xs)]
    for sel in (in32, in64):
        ls = [_mm(p, jnp.where(sel, m, 0.0)) for p, m in zip(ps, ms)]
        ps = [p - _mm(l, p) for p, l in zip(ps, ls)]
    return ps


def _deltanet_body(qkv_ref, gate_ref, ba_ref, convw_ref, hp_ref, dnorm_ref, o_ref, ext_ref, s_ref):
    ci = pl.program_id(1)

    @pl.when(ci == 0)
    def _():
        ext_ref[0:HALO, :] = jnp.zeros((HALO, 3 * DN_WIDTH), F32)
        s_ref[...] = jnp.zeros_like(s_ref)

    ext_ref[HALO:HALO + CHUNK, :] = qkv_ref[...]
    conv = convw_ref[0:1, :] * ext_ref[HALO - 3:HALO - 3 + CHUNK, :]
    for j in range(1, CONV_K):
        conv = conv + convw_ref[j:j + 1, :] * ext_ref[HALO - 3 + j:HALO - 3 + j + CHUNK, :]
    ext_ref[0:HALO, :] = ext_ref[CHUNK:CHUNK + HALO, :]
    act = _silu(conv)

    ba = ba_ref[...]
    beta_all = _sigmoid(ba)
    g_all = -jnp.exp(hp_ref[1:2, :]) * _softplus(ba + hp_ref[0:1, :])
    r = lax.broadcasted_iota(jnp.int32, (CHUNK, CHUNK), 0)
    c = lax.broadcasted_iota(jnp.int32, (CHUNK, CHUNK), 1)
    incl = r >= c
    strict = r > c
    tri = incl.astype(BF16)
    g1, g2, g3 = _split3(g_all)
    gcum_all = (jnp.dot(tri, g1, preferred_element_type=F32)
                + jnp.dot(tri, g2, preferred_element_type=F32)
                + jnp.dot(tri, g3, preferred_element_type=F32))
    gcum_t = gcum_all.T
    dnorm = dnorm_ref[...]
    scale = DN_HEAD_DIM ** -0.5
    heads = range(DN_HEADS)

    def l2n(x):
        return x * lax.rsqrt(jnp.sum(x * x, axis=-1, keepdims=True) + EPS)

    qs, ks, vs = [], [], []
    for h in heads:
        lo = h * DN_HEAD_DIM
        qs.append(l2n(act[:, lo:lo + DN_HEAD_DIM]) * scale)
        ks.append(l2n(act[:, DN_WIDTH + lo:DN_WIDTH + lo + DN_HEAD_DIM]))
        vs.append(act[:, 2 * DN_WIDTH + lo:2 * DN_WIDTH + lo + DN_HEAD_DIM])
    betas = [beta_all[:, h:h + 1] for h in heads]
    gcs = [gcum_all[:, DN_HEADS + h:DN_HEADS + h + 1] for h in heads]
    g_lasts = [gcum_all[CHUNK - 1:CHUNK, DN_HEADS + h:DN_HEADS + h + 1] for h in heads]
    decays = [jnp.where(incl, jnp.exp(jnp.where(
        incl, gcs[h] - gcum_t[DN_HEADS + h:DN_HEADS + h + 1, :], 0.0)), 0.0) for h in heads]
    kbs = [ks[h] * betas[h] for h in heads]

    kq = [_mm_nt(jnp.concatenate([kbs[h], qs[h]], axis=0), ks[h]) for h in heads]
    ms = [jnp.where(strict, kq[h][0:CHUNK] * decays[h], 0.0) for h in heads]
    attns = [jnp.where(incl, kq[h][CHUNK:2 * CHUNK] * decays[h], 0.0) for h in heads]
    ts = _unit_lower_inverse(ms, r, c)
    uw = [_mm(ts[h], jnp.concatenate([vs[h] * betas[h], kbs[h] * jnp.exp(gcs[h])], axis=1))
          for h in heads]
    states = [s_ref[h] for h in heads]
    ws = [_mm(jnp.concatenate([uw[h][:, DN_HEAD_DIM:], qs[h] * jnp.exp(gcs[h])], axis=0), states[h])
          for h in heads]
    v_news = [uw[h][:, 0:DN_HEAD_DIM] - ws[h][0:CHUNK] for h in heads]
    intra = [_mm(attns[h], v_news[h]) for h in heads]
    upd = [_mm_tn(ks[h] * jnp.exp(g_lasts[h] - gcs[h]), v_news[h]) for h in heads]
    for h in heads:
        lo = h * DN_HEAD_DIM
        s_ref[h] = states[h] * jnp.exp(g_lasts[h]) + upd[h]
        o = ws[h][CHUNK:2 * CHUNK] + intra[h]
        gt = gate_ref[:, lo:lo + DN_HEAD_DIM]
        o_ref[:, lo:lo + DN_HEAD_DIM] = (_rms(o, dnorm) * _silu(gt)).astype(o_ref.dtype)


def _deltanet(proj, ba, conv_w, dt_bias, a_log, dn_norm, batch, seq):
    n = batch * seq
    nc = seq // CHUNK
    wq = 3 * DN_WIDTH
    hp = jnp.zeros((SUBLANES, LANES), F32)
    hp = hp.at[0, DN_HEADS:2 * DN_HEADS].set(dt_bias).at[1, DN_HEADS:2 * DN_HEADS].set(a_log)
    row = lambda b, ci: b * nc + ci
    return pl.pallas_call(
        _deltanet_body,
        grid=(batch, nc),
        in_specs=[pl.BlockSpec((CHUNK, wq), lambda b, ci: (row(b, ci), 0)),
                  pl.BlockSpec((CHUNK, DN_WIDTH), lambda b, ci: (row(b, ci), wq // DN_WIDTH)),
                  pl.BlockSpec((CHUNK, LANES), lambda b, ci: (row(b, ci), 0)),
                  pl.BlockSpec((CONV_K, wq), lambda b, ci: (0, 0)),
                  pl.BlockSpec((SUBLANES, LANES), lambda b, ci: (0, 0)),
                  pl.BlockSpec((1, DN_HEAD_DIM), lambda b, ci: (0, 0))],
        out_specs=pl.BlockSpec((CHUNK, DN_WIDTH), lambda b, ci: (row(b, ci), 0)),
        out_shape=jax.ShapeDtypeStruct((n, DN_WIDTH), BF16),
        scratch_shapes=[pltpu.VMEM((CHUNK + HALO, wq), F32),
                        pltpu.VMEM((DN_HEADS, DN_HEAD_DIM, DN_HEAD_DIM), F32)],
        compiler_params=_cparams(("parallel", "arbitrary")),
    )(proj, proj, ba, conv_w, hp, dn_norm.reshape(1, DN_HEAD_DIM))


def _diffattn_body(slopes_ref, q_ref, k_ref, vt_ref, lamv_ref, danorm_ref, o_ref,
                   m_ref, l_ref, acc_ref, nrc_ref, dbias_ref, *, tq, lam_init):
    slope = slopes_ref[pl.program_id(1)]
    nq = q_ref.shape[0] // tq
    cols = 2 * tq

    kr = lax.broadcasted_iota(jnp.int32, (tq, cols), 0)
    qc = lax.broadcasted_iota(jnp.int32, (tq, cols), 1) & (tq - 1)
    rc = (qc - kr).astype(F32)
    nrc_ref[...] = -slope * rc
    dbias_ref[...] = jnp.where((kr >> 6) <= (qc >> 6), -slope * jnp.abs(rc), -jnp.inf)

    lv = lamv_ref[...]
    lam = (jnp.exp(jnp.sum(lv[0:1] * lv[1:2], axis=-1, keepdims=True))
           - jnp.exp(jnp.sum(lv[2:3] * lv[3:4], axis=-1, keepdims=True)) + lam_init)

    def rows(j):
        return pl.ds(pl.multiple_of(j * tq, tq), tq)

    def query_block(qi, carry):
        q = q_ref[rows(qi), :]
        lane = lax.broadcasted_iota(jnp.int32, q.shape, 1)
        zero = jnp.zeros_like(q)
        qq = jnp.concatenate([jnp.where(lane < DA_HEAD_DIM, q, zero),
                              jnp.where(lane >= DA_HEAD_DIM, q, zero)], axis=0)

        def pv_update(p, a, j):
            acc_ref[...] = a * acc_ref[...] + jnp.dot(vt_ref[j], p, preferred_element_type=F32)

        def biased(j, bias_ref):
            z = _mm_nt(k_ref[rows(j), :], qq) + bias_ref[...]
            return z, jnp.max(z, axis=0, keepdims=True)

        first = biased(0, nrc_ref)
        z0, m0 = biased(qi, dbias_ref)
        p0 = jnp.exp(z0 - m0)
        m_ref[...] = m0
        l_ref[...] = jnp.sum(p0, axis=0, keepdims=True)
        acc_ref[...] = jnp.zeros_like(acc_ref)

        def body(kj, carry):
            z_cur, zmax_cur, p_prev, a_prev = carry
            pv_update(p_prev, a_prev, jnp.where(kj == 0, qi, kj - 1))
            z_next, zmax_next = biased(jnp.minimum(kj + 1, qi), nrc_ref)
            shift = slope * ((qi - kj) * tq).astype(F32)
            m_old = m_ref[...]
            m_new = jnp.maximum(m_old, zmax_cur - shift)
            p = jnp.exp(z_cur - (m_new + shift))
            scale = jnp.exp(m_old - m_new)
            l_ref[...] = scale * l_ref[...] + jnp.sum(p, axis=0, keepdims=True)
            m_ref[...] = m_new
            return z_next, zmax_next, p.astype(BF16), scale

        _, _, p_last, a_last = lax.fori_loop(
            0, qi, body, first + (p0.astype(BF16), jnp.ones((1, cols), F32)))
        pv_update(p_last, a_last, jnp.where(qi == 0, qi, qi - 1))

        o1 = acc_ref[:, 0:tq] / l_ref[:, 0:tq]
        o2 = acc_ref[:, tq:2 * tq] / l_ref[:, tq:2 * tq]
        o = o1 - lam * o2
        o = o * lax.rsqrt(jnp.mean(o * o, axis=0, keepdims=True) + EPS) * danorm_ref[...]
        o_ref[rows(qi), :] = (o * (1.0 - lam_init)).T.astype(o_ref.dtype)
        return carry

    lax.fori_loop(0, nq, query_block, 0)


def _diffattn(qk, v_t, lam_vecs, da_norm, lam_init, batch, seq):
    n = batch * seq
    tq = v_t.shape[2]
    nq = seq // tq
    slopes = jnp.exp2(-8.0 * jnp.arange(1, DA_HEADS + 1, dtype=F32) / DA_HEADS)
    grid_spec = pltpu.PrefetchScalarGridSpec(
        num_scalar_prefetch=1,
        grid=(batch, DA_HEADS),
        in_specs=[pl.BlockSpec((seq, DA_V_DIM), lambda b, h, sl: (b, h)),
                  pl.BlockSpec((seq, DA_V_DIM), lambda b, h, sl: (b, DA_HEADS + h)),
                  pl.BlockSpec((nq, DA_V_DIM, tq), lambda b, h, sl: (b, h, 0)),
                  pl.BlockSpec((4, DA_HEAD_DIM), lambda b, h, sl: (0, 0)),
                  pl.BlockSpec((DA_V_DIM, 1), lambda b, h, sl: (0, 0))],
        out_specs=pl.BlockSpec((seq, DA_V_DIM), lambda b, h, sl: (b, h)),
        scratch_shapes=[pltpu.VMEM((1, 2 * tq), F32), pltpu.VMEM((1, 2 * tq), F32),
                        pltpu.VMEM((DA_V_DIM, 2 * tq), F32), pltpu.VMEM((tq, 2 * tq), F32),
                        pltpu.VMEM((tq, 2 * tq), F32)],
    )
    return pl.pallas_call(
        functools.partial(_diffattn_body, tq=tq, lam_init=lam_init),
        grid_spec=grid_spec,
        out_shape=jax.ShapeDtypeStruct((n, DA_WIDTH), BF16),
        compiler_params=_cparams(("parallel", "parallel")),
    )(slopes, qk, qk, v_t, lam_vecs, da_norm.reshape(DA_V_DIM, 1))


def _attn_proj_body(x_ref, g_ref, wqk_ref, wvt_ref, qk_ref, vt_ref, *, tq):
    xn = _rms(x_ref[...], g_ref[...]).astype(BF16)
    qk_ref[...] = jnp.dot(xn, wqk_ref[...], preferred_element_type=F32).astype(qk_ref.dtype)
    for j in range(vt_ref.shape[0]):
        vt_ref[j] = lax.dot_general(wvt_ref[...], xn[j * tq:(j + 1) * tq], (((1,), (1,)), ((), ())),
                                    preferred_element_type=F32).astype(vt_ref.dtype)


def _attn_proj(x, gain, w_qk, w_vt, tq):
    n, d = x.shape
    tm = min(ROW_TM, n)
    nb = tm // tq
    return pl.pallas_call(
        functools.partial(_attn_proj_body, tq=tq),
        grid=(n // tm,),
        in_specs=[pl.BlockSpec((tm, d), lambda i: (i, 0)),
                  pl.BlockSpec((1, d), lambda i: (0, 0)),
                  pl.BlockSpec((d, 2 * DA_WIDTH), lambda i: (0, 0)),
                  pl.BlockSpec((DA_WIDTH, d), lambda i: (0, 0))],
        out_specs=[pl.BlockSpec((tm, 2 * DA_WIDTH), lambda i: (i, 0)),
                   pl.BlockSpec((nb, DA_WIDTH, tq), lambda i: (i, 0, 0))],
        out_shape=[jax.ShapeDtypeStruct((n, 2 * DA_WIDTH), BF16),
                   jax.ShapeDtypeStruct((n // tq, DA_WIDTH, tq), BF16)],
        compiler_params=_cparams(("parallel",)),
    )(x, gain.reshape(1, d), w_qk, w_vt)


def _outproj_body(h_ref, odn_ref, oda_ref, w_ref, o_ref):
    o_ref[...] = (h_ref[...]
                  + jnp.dot(odn_ref[...], w_ref[0:DN_WIDTH, :], preferred_element_type=F32)
                  + jnp.dot(oda_ref[...], w_ref[DN_WIDTH:DN_WIDTH + DA_WIDTH, :],
                            preferred_element_type=F32))


def _outproj(h, o_dn, o_da, w_out):
    n, d = h.shape
    tm = min(ROW_TM, n)
    return pl.pallas_call(
        _outproj_body,
        grid=(n // tm,),
        in_specs=[pl.BlockSpec((tm, d), lambda i: (i, 0)),
                  pl.BlockSpec((tm, DN_WIDTH), lambda i: (i, 0)),
                  pl.BlockSpec((tm, DA_WIDTH), lambda i: (i, 0)),
                  pl.BlockSpec((DN_WIDTH + DA_WIDTH, d), lambda i: (0, 0))],
        out_specs=pl.BlockSpec((tm, d), lambda i: (i, 0)),
        out_shape=jax.ShapeDtypeStruct((n, d), F32),
        compiler_params=_cparams(("parallel",)),
    )(h, o_dn, o_da, w_out)


def _ffn_body(x_ref, g_ref, w1_ref, w3_ref, w2_ref, o_ref, xn_ref):
    @pl.when(pl.program_id(1) == 0)
    def _():
        x = x_ref[...]
        xn_ref[...] = _rms(x, g_ref[...]).astype(BF16)
        o_ref[...] = x

    xn = xn_ref[...]
    g = jnp.dot(xn, w1_ref[...], preferred_element_type=F32)
    u = jnp.dot(xn, w3_ref[...], preferred_element_type=F32)
    a = (_silu(g) * u).astype(BF16)
    o_ref[...] += jnp.dot(a, w2_ref[...], preferred_element_type=F32)


def _ffn(x, gain, w1, w3, w2):
    n, d = x.shape
    dff = w1.shape[1]
    tm = min(FFN_TM, n)
    tf = FFN_TF
    return pl.pallas_call(
        _ffn_body,
        grid=(n // tm, dff // tf),
        in_specs=[pl.BlockSpec((tm, d), lambda i, f: (i, 0)),
                  pl.BlockSpec((1, d), lambda i, f: (0, 0)),
                  pl.BlockSpec((d, tf), lambda i, f: (0, f)),
                  pl.BlockSpec((d, tf), lambda i, f: (0, f)),
                  pl.BlockSpec((tf, d), lambda i, f: (f, 0))],
        out_specs=pl.BlockSpec((tm, d), lambda i, f: (i, 0)),
        out_shape=jax.ShapeDtypeStruct((n, d), F32),
        scratch_shapes=[pltpu.VMEM((tm, d), BF16)],
        compiler_params=_cparams(("parallel", "arbitrary")),
    )(x, gain.reshape(1, d), w1, w3, w2)


def _router_body(x_ref, g_ref, wt_ref, idx_ref, gate_ref):
    xn = _rms(x_ref[...], g_ref[...])
    x1, x2, x3 = _split3(xn)
    w1, w2, w3 = _split3(wt_ref[...])
    nt = lambda a, b: lax.dot_general(a, b, (((1,), (1,)), ((), ())), preferred_element_type=F32)
    logits = (nt(w3, x1) + nt(w2, x2) + nt(w1, x3)) + (nt(w2, x1) + nt(w1, x2)) + nt(w1, x1)
    e = lax.broadcasted_iota(jnp.int32, logits.shape, 0)
    m0 = jnp.max(logits, axis=0, keepdims=True)
    i0 = jnp.min(jnp.where(logits == m0, e, N_EXPERTS), axis=0, keepdims=True)
    rest = jnp.where(e == i0, -jnp.inf, logits)
    m1 = jnp.max(rest, axis=0, keepdims=True)
    i1 = jnp.min(jnp.where(rest == m1, e, N_EXPERTS), axis=0, keepdims=True)
    t = jnp.exp(m1 - m0)
    idx_ref[0:1, :] = i0
    idx_ref[1:2, :] = i1
    gate_ref[0:1, :] = 1.0 / (1.0 + t)
    gate_ref[1:2, :] = t / (1.0 + t)


def _router(x, gain, w_router):
    n, d = x.shape
    tm = min(ROW_TM, n)
    return pl.pallas_call(
        _router_body,
        grid=(n // tm,),
        in_specs=[pl.BlockSpec((tm, d), lambda i: (i, 0)),
                  pl.BlockSpec((1, d), lambda i: (0, 0)),
                  pl.BlockSpec((N_EXPERTS, d), lambda i: (0, 0))],
        out_specs=[pl.BlockSpec((TOP_K, tm), lambda i: (0, i)),
                   pl.BlockSpec((TOP_K, tm), lambda i: (0, i))],
        out_shape=[jax.ShapeDtypeStruct((TOP_K, n), jnp.int32),
                   jax.ShapeDtypeStruct((TOP_K, n), F32)],
        compiler_params=_cparams(("parallel",)),
    )(x, gain.reshape(1, d), w_router.T)


def _experts_body(be_ref, tok_ref, dst_ref, nb_ref, x_hbm, g_ref, gate_ref, w1_ref, w3_ref, w2_ref,
                  y_hbm, xg_ref, xn_ref, acc_ref, gsem, ssem, *, tm, nf):
    b = pl.program_id(0)
    f = pl.program_id(1)
    nb = nb_ref[0]
    slot = b & 1
    other = 1 - slot
    per_step, extra = divmod(tm, nf)

    def gather(blk, row, buf):
        return pltpu.make_async_copy(x_hbm.at[pl.ds(tok_ref[blk * tm + row], 1), :],
                                     xg_ref.at[buf, pl.ds(row, 1), :], gsem.at[buf])

    def scatter(blk, row, buf):
        return pltpu.make_async_copy(acc_ref.at[buf, pl.ds(row, 1), :],
                                     y_hbm.at[pl.ds(dst_ref[(blk + 1) * tm + row], 1), :], ssem.at[buf])

    def for_rows(fn):
        def step(i, carry):
            fn(i)
            return carry
        lax.fori_loop(0, tm, step, 0)

    def wait_gather(buf):
        pltpu.make_async_copy(x_hbm.at[pl.ds(0, tm), :], xg_ref.at[buf], gsem.at[buf]).wait()

    def wait_scatter(buf):
        pltpu.make_async_copy(acc_ref.at[buf], y_hbm.at[pl.ds(0, tm), :], ssem.at[buf]).wait()

    used = b < nb

    @pl.when(jnp.logical_and(f == 0, b == 0))
    def _():
        for_rows(lambda i: gather(0, i, 0).start())
        acc_ref[1] = jnp.zeros((tm, acc_ref.shape[2]), F32)

    @pl.when(jnp.logical_and(f == 0, b <= nb))
    def _():
        wait_gather(slot)

    @pl.when(jnp.logical_and(f == 0, jnp.logical_and(b <= nb, b >= 1)))
    def _():
        wait_scatter(slot)

    @pl.when(jnp.logical_and(f == 0, used))
    def _():
        xn_ref[...] = _rms(xg_ref[slot], g_ref[...]).astype(BF16)
        acc_ref[slot] = jnp.zeros((tm, acc_ref.shape[2]), F32)

    base = f * per_step + jnp.minimum(f, extra)

    @pl.when(used)
    def _():
        for t in range(per_step):
            gather(b + 1, base + t, other).start()
            scatter(b - 1, base + t, other).start()
        xn = xn_ref[...]
        g = jnp.dot(xn, w1_ref[...], preferred_element_type=F32)
        u = jnp.dot(xn, w3_ref[...], preferred_element_type=F32)
        a = (_silu(g) * u).astype(BF16)
        acc_ref[slot] += jnp.dot(a, w2_ref[...], preferred_element_type=F32)

    @pl.when(jnp.logical_and(f < extra, used))
    def _():
        gather(b + 1, base + per_step, other).start()
        scatter(b - 1, base + per_step, other).start()

    @pl.when(jnp.logical_and(f == nf - 1, used))
    def _():
        acc_ref[slot] = acc_ref[slot] * gate_ref[...]

    @pl.when(jnp.logical_and(f == 0, b == nb))
    def _():
        for_rows(lambda i: scatter(b - 1, i, other).start())
        wait_scatter(other)

    @pl.when(jnp.logical_and(f == 0, b >= nb))
    def _():
        acc_ref[0] = jnp.zeros((tm, acc_ref.shape[2]), F32)
        for_rows(lambda i: scatter(b, i, 0).start())
        wait_scatter(0)


def _experts(x, gain, w1, w3, w2, plan, tm):
    block_e, row_tok, row_dst, row_gate, n_used, n_grid = plan
    n, d = x.shape
    dff = w1.shape[2]
    tf = MOE_TF
    nf = dff // tf

    def widx(b, f, be, tok, dst, nb):
        live = b < nb[0]
        last = jnp.maximum(nb[0] - 1, 0)
        return jnp.where(live, be[b], be[last]), jnp.where(live, f, nf - 1)

    def w13_map(b, f, be, tok, dst, nb):
        e, ff = widx(b, f, be, tok, dst, nb)
        return e, 0, ff

    def w2_map(b, f, be, tok, dst, nb):
        e, ff = widx(b, f, be, tok, dst, nb)
        return e, ff, 0

    grid_spec = pltpu.PrefetchScalarGridSpec(
        num_scalar_prefetch=4,
        grid=(n_grid, nf),
        in_specs=[pl.BlockSpec(memory_space=pl.ANY),
                  pl.BlockSpec((1, d), lambda b, f, *_: (0, 0)),
                  pl.BlockSpec((tm, 1), lambda b, f, *_: (b, 0)),
                  pl.BlockSpec((None, d, tf), w13_map),
                  pl.BlockSpec((None, d, tf), w13_map),
                  pl.BlockSpec((None, tf, d), w2_map)],
        out_specs=pl.BlockSpec(memory_space=pl.ANY),
        scratch_shapes=[pltpu.VMEM((2, tm, d), F32), pltpu.VMEM((tm, d), BF16),
                        pltpu.VMEM((2, tm, d), F32),
                        pltpu.SemaphoreType.DMA((2,)), pltpu.SemaphoreType.DMA((2,))],
    )
    n_y = (n_grid + 1) * tm
    return pl.pallas_call(
        functools.partial(_experts_body, tm=tm, nf=nf),
        grid_spec=grid_spec,
        out_shape=jax.ShapeDtypeStruct((n_y, d), F32),
        compiler_params=_cparams(("arbitrary", "arbitrary")),
    )(block_e, row_tok, row_dst, n_used, x, gain.reshape(1, d), row_gate, w1, w3, w2)


def _route_plan(idx, gates, n, tm):
    n_assign = n * TOP_K
    flat_e = idx.T.reshape(n_assign)
    onehot = (flat_e[:, None] == jnp.arange(N_EXPERTS, dtype=jnp.int32)[None, :]).astype(jnp.int32)
    csum = jnp.cumsum(onehot, axis=0)
    counts = csum[-1]
    rank = jnp.sum((csum - onehot) * onehot, axis=1)
    padded = (counts + tm - 1) // tm * tm
    pad_end = jnp.cumsum(padded)
    pad_start = pad_end - padded
    dest = pad_start[flat_e] + rank
    n_blocks = -(-n_assign // tm) + N_EXPERTS
    n_grid = n_blocks + 1
    n_rows = n_grid * tm
    row_a = jnp.full((n_rows + tm,), -1, jnp.int32).at[tm + dest].set(
        jnp.arange(n_assign, dtype=jnp.int32))
    is_pad = row_a < 0
    a = jnp.maximum(row_a, 0)
    row_tok = (a // TOP_K)[tm:]
    row_gate = jnp.where(is_pad, 0.0, gates.T.reshape(n_assign)[a])[tm:]
    row_dst = jnp.where(is_pad, n_assign + jnp.cumsum(is_pad.astype(jnp.int32)) - 1,
                        (a % TOP_K) * n + a // TOP_K)
    block_e = jnp.minimum(
        jnp.searchsorted(pad_end, jnp.arange(n_grid, dtype=jnp.int32) * tm, side='right'),
        N_EXPERTS - 1).astype(jnp.int32)
    n_used = (pad_end[-1] // tm).astype(jnp.int32).reshape(1)
    return block_e, row_tok, row_dst, row_gate.reshape(n_rows, 1), n_used, n_grid


def _ple_body(*refs, final, combine):
    if combine:
        h_ref, y0_ref, y1_ref, p_ref, wp_ref, pn_ref, wg_ref, nf_ref, o_ref = refs
        h = h_ref[...] + (y0_ref[...] + y1_ref[...])
    else:
        h_ref, p_ref, wp_ref, pn_ref, wg_ref, nf_ref, o_ref = refs
        h = h_ref[...]
    e = _rms(_mm(p_ref[...], wp_ref[...]), pn_ref[...])
    out = h + _sigmoid(_mm(h, wg_ref[...])) * e
    if final:
        out = _rms(out, nf_ref[...])
    o_ref[...] = out


def _ple(h, p, w_proj, p_norm, w_gate, norm_final, final, y=None):
    n, d = h.shape
    dp = p.shape[1]
    tm = min(ROW_TM, n)
    row = lambda i: (i, 0)
    fixed = lambda i: (0, 0)
    specs = [pl.BlockSpec((tm, d), row)]
    args = [h]
    if y is not None:
        specs += [pl.BlockSpec((tm, d), row), pl.BlockSpec((tm, d), lambda i: (n // tm + i, 0))]
        args += [y, y]
    specs += [pl.BlockSpec((tm, dp), row), pl.BlockSpec((dp, d), fixed), pl.BlockSpec((1, d), fixed),
              pl.BlockSpec((d, d), fixed), pl.BlockSpec((1, d), fixed)]
    args += [p, w_proj, p_norm.reshape(1, d), w_gate, norm_final.reshape(1, d)]
    return pl.pallas_call(
        functools.partial(_ple_body, final=final, combine=y is not None),
        grid=(n // tm,),
        in_specs=specs,
        out_specs=pl.BlockSpec((tm, d), row),
        out_shape=jax.ShapeDtypeStruct((n, d), F32),
        compiler_params=_cparams(("parallel",)),
    )(*args)


def kernel(x, p, norm_mix, w_in, conv_w, dn_a_log, dn_dt_bias, dn_norm, da_lambda_q1, da_lambda_k1, da_lambda_q2, da_lambda_k2, da_norm, w_out, norm_ffn, ffn_w1, ffn_w3, ffn_w2, moe_router, moe_w1, moe_w3, moe_w2, ple_proj, ple_norm, ple_gate, norm_final):
    batch, seq, d = x.shape
    n = batch * seq
    depth = w_in.shape[0]
    h = x.reshape(n, d)
    wq = 3 * DN_WIDTH
    for i in range(depth):
        wi = w_in[i]
        c0 = wq + DN_WIDTH
        c1 = c0 + 2 * DN_HEADS
        w_dn = wi[:, :c0].astype(BF16)
        w_ba = jnp.pad(wi[:, c0:c1], ((0, 0), (0, LANES - 2 * DN_HEADS))).astype(BF16)
        att_scale = DA_HEAD_DIM ** -0.5
        w_qk = jnp.concatenate([wi[:, c1:c1 + DA_WIDTH] * att_scale,
                                wi[:, c1 + DA_WIDTH:c1 + 2 * DA_WIDTH]], axis=1).astype(BF16)
        w_vt = wi[:, c1 + 2 * DA_WIDTH:].T.astype(BF16)
        proj_dn, ba = _dn_proj(h, norm_mix[i], w_dn, w_ba, DN_WIDTH)
        qk, v_t = _attn_proj(h, norm_mix[i], w_qk, w_vt, min(ATT_TQ, seq))

        o_dn = _deltanet(proj_dn, ba, conv_w[i], dn_dt_bias[i], dn_a_log[i], dn_norm[i], batch, seq)
        lam_init = 0.8 - 0.6 * math.exp(-0.3 * i)
        lam_vecs = jnp.stack([da_lambda_q1[i], da_lambda_k1[i], da_lambda_q2[i], da_lambda_k2[i]])
        o_da = _diffattn(qk, v_t, lam_vecs, da_norm[i], lam_init, batch, seq)
        h = _outproj(h, o_dn, o_da, w_out[i].astype(BF16))

        j = i // 2
        last = i == depth - 1
        wp = ple_proj[i].astype(BF16)
        wg = ple_gate[i].astype(BF16)
        if i % 2 == 0:
            h = _ffn(h, norm_ffn[i], ffn_w1[j].astype(BF16), ffn_w3[j].astype(BF16),
                     ffn_w2[j].astype(BF16))
            h = _ple(h, p[i].reshape(n, -1), wp, ple_norm[i], wg, norm_final, last)
        else:
            idx, gates = _router(h, norm_ffn[i], moe_router[j])
            tm = min(MOE_TM, n)
            plan = _route_plan(idx, gates, n, tm)
            y = _experts(h, norm_ffn[i], moe_w1[j].astype(BF16), moe_w3[j].astype(BF16),
                         moe_w2[j].astype(BF16), plan, tm)
            h = _ple(h, p[i].reshape(n, -1), wp, ple_norm[i], wg, norm_final, last, y=y)
    return h.reshape(batch, seq, d)
```

```python
import functools
import math

import jax
import jax.numpy as jnp
from jax import lax
from jax.experimental import pallas as pl
from jax.experimental.pallas import tpu as pltpu

F32 = jnp.float32
BF16 = jnp.bfloat16
EPS = 1e-6

CHUNK = 64
DN_HEADS = 8
DN_HEAD_DIM = 128
DN_WIDTH = DN_HEADS * DN_HEAD_DIM
CONV_K = 4
DA_HEADS = 8
DA_HEAD_DIM = 64
DA_V_DIM = 2 * DA_HEAD_DIM
DA_WIDTH = DA_HEADS * DA_V_DIM
N_EXPERTS = 8
TOP_K = 2
LANES = 128
SUBLANES = 8
VMEM_LIMIT = 56 * 1024 * 1024

PROJ_TM = 512
FFN_TM = 512
FFN_TF = 512
MOE_TM = 512
MOE_TF = 1024
ATT_TQ = 256
ROW_TM = 512
HALO = SUBLANES


def _cparams(sem):
    return pltpu.CompilerParams(dimension_semantics=sem, vmem_limit_bytes=VMEM_LIMIT)


def _sigmoid(x):
    return 1.0 / (1.0 + jnp.exp(-x))


def _silu(x):
    return x * _sigmoid(x)


def _softplus(x):
    return jnp.maximum(x, 0.0) + jnp.log1p(jnp.exp(-jnp.abs(x)))


def _rms(x, gain):
    y = x * lax.rsqrt(jnp.mean(x * x, axis=-1, keepdims=True) + EPS)
    return y * gain


def _mm(a, b):
    return jnp.dot(a.astype(BF16), b.astype(BF16), preferred_element_type=F32)


def _mm_nt(a, b):
    return lax.dot_general(a.astype(BF16), b.astype(BF16), (((1,), (1,)), ((), ())),
                           preferred_element_type=F32)


def _mm_tn(a, b):
    return lax.dot_general(a.astype(BF16), b.astype(BF16), (((0,), (0,)), ((), ())),
                           preferred_element_type=F32)


def _split3(a):
    a1 = a.astype(BF16)
    r1 = a - a1.astype(F32)
    a2 = r1.astype(BF16)
    a3 = (r1 - a2.astype(F32)).astype(BF16)
    return a1, a2, a3


def _dn_proj_body(x_ref, g_ref, w_ref, wba_ref, o_ref, ba_ref, xn_ref):
    @pl.when(pl.program_id(1) == 0)
    def _():
        xn_ref[...] = _rms(x_ref[...], g_ref[...]).astype(BF16)
        ba_ref[...] = jnp.dot(xn_ref[...], wba_ref[...], preferred_element_type=F32)

    o_ref[...] = jnp.dot(xn_ref[...], w_ref[...], preferred_element_type=F32)


def _dn_proj(x, gain, w, w_ba, tn):
    n, d = x.shape
    nout = w.shape[1]
    tm = min(PROJ_TM, n)
    return pl.pallas_call(
        _dn_proj_body,
        grid=(n // tm, nout // tn),
        in_specs=[pl.BlockSpec((tm, d), lambda i, j: (i, 0)),
                  pl.BlockSpec((1, d), lambda i, j: (0, 0)),
                  pl.BlockSpec((d, tn), lambda i, j: (0, j)),
                  pl.BlockSpec((d, LANES), lambda i, j: (0, 0))],
        out_specs=[pl.BlockSpec((tm, tn), lambda i, j: (i, j)),
                   pl.BlockSpec((tm, LANES), lambda i, j: (i, 0))],
        out_shape=[jax.ShapeDtypeStruct((n, nout), F32), jax.ShapeDtypeStruct((n, LANES), F32)],
        scratch_shapes=[pltpu.VMEM((tm, d), BF16)],
        compiler_params=_cparams(("parallel", "arbitrary")),
    )(x, gain.reshape(1, d), w, w_ba)


def _unit_lower_inverse(ms, r, c):
    eye = (r == c).astype(F32)
    bd16 = (r >> 4) == (c >> 4)
    in32 = ((r >> 5) == (c >> 5)) & ((r >> 4) > (c >> 4))
    in64 = (r >> 5) > (c >> 5)
    xs = [jnp.where(bd16, -m, 0.0) for m in ms]
    ps = [eye + x for x in xs]
    xs = [_mm(x, x) for x in xs]
    for _ in range(2):
        ts = [_mm(jnp.concatenate([p, x], axis=0), x) for p, x in zip(ps, xs)]
        ps = [p + t[0:CHUNK] for p, t in zip(ps, ts)]
        xs = [t[CHUNK:2 * CHUNK] for t in ts]
    ps = [p + _mm(p, x) for p, x in zip(ps, xs)]
    for sel in (in32, in64):
        ls = [_mm(p, jnp.where(sel, m, 0.0)) for p, m in zip(ps, ms)]
        ps = [p - _mm(l, p) for p, l in zip(ps, ls)]
    return ps


def _deltanet_body(qkv_ref, gate_ref, ba_ref, convw_ref, hp_ref, dnorm_ref, o_ref, ext_ref, s_ref):
    ci = pl.program_id(1)

    @pl.when(ci == 0)
    def _():
        ext_ref[0:HALO, :] = jnp.zeros((HALO, 3 * DN_WIDTH), F32)
        s_ref[...] = jnp.zeros_like(s_ref)

    ext_ref[HALO:HALO + CHUNK, :] = qkv_ref[...]
    conv = convw_ref[0:1, :] * ext_ref[HALO - 3:HALO - 3 + CHUNK, :]
    for j in range(1, CONV_K):
        conv = conv + convw_ref[j:j + 1, :] * ext_ref[HALO - 3 + j:HALO - 3 + j + CHUNK, :]
    ext_ref[0:HALO, :] = ext_ref[CHUNK:CHUNK + HALO, :]
    act = _silu(conv)

    ba = ba_ref[...]
    beta_all = _sigmoid(ba)
    g_all = -jnp.exp(hp_ref[1:2, :]) * _softplus(ba + hp_ref[0:1, :])
    r = lax.broadcasted_iota(jnp.int32, (CHUNK, CHUNK), 0)
    c = lax.broadcasted_iota(jnp.int32, (CHUNK, CHUNK), 1)
    incl = r >= c
    strict = r > c
    tri = incl.astype(BF16)
    g1, g2, g3 = _split3(g_all)
    gcum_all = (jnp.dot(tri, g1, preferred_element_type=F32)
                + jnp.dot(tri, g2, preferred_element_type=F32)
                + jnp.dot(tri, g3, preferred_element_type=F32))
    gcum_t = gcum_all.T
    dnorm = dnorm_ref[...]
    scale = DN_HEAD_DIM ** -0.5
    heads = range(DN_HEADS)

    def l2n(x):
        return x * lax.rsqrt(jnp.sum(x * x, axis=-1, keepdims=True) + EPS)

    qs, ks, vs = [], [], []
    for h in heads:
        lo = h * DN_HEAD_DIM
        qs.append(l2n(act[:, lo:lo + DN_HEAD_DIM]) * scale)
        ks.append(l2n(act[:, DN_WIDTH + lo:DN_WIDTH + lo + DN_HEAD_DIM]))
        vs.append(act[:, 2 * DN_WIDTH + lo:2 * DN_WIDTH + lo + DN_HEAD_DIM])
    betas = [beta_all[:, h:h + 1] for h in heads]
    gcs = [gcum_all[:, DN_HEADS + h:DN_HEADS + h + 1] for h in heads]
    g_lasts = [gcum_all[CHUNK - 1:CHUNK, DN_HEADS + h:DN_HEADS + h + 1] for h in heads]
    decays = [jnp.where(incl, jnp.exp(jnp.where(
        incl, gcs[h] - gcum_t[DN_HEADS + h:DN_HEADS + h + 1, :], 0.0)), 0.0) for h in heads]
    kbs = [ks[h] * betas[h] for h in heads]

    kq = [_mm_nt(jnp.concatenate([kbs[h], qs[h]], axis=0), ks[h]) for h in heads]
    ms = [jnp.where(strict, kq[h][0:CHUNK] * decays[h], 0.0) for h in heads]
    attns = [jnp.where(incl, kq[h][CHUNK:2 * CHUNK] * decays[h], 0.0) for h in heads]
    ts = _unit_lower_inverse(ms, r, c)
    uw = [_mm(ts[h], jnp.concatenate([vs[h] * betas[h], kbs[h] * jnp.exp(gcs[h])], axis=1))
          for h in heads]
    states = [s_ref[h] for h in heads]
    ws = [_mm(jnp.concatenate([uw[h][:, DN_HEAD_DIM:], qs[h] * jnp.exp(gcs[h])], axis=0), states[h])
          for h in heads]
    v_news = [uw[h][:, 0:DN_HEAD_DIM] - ws[h][0:CHUNK] for h in heads]
    intra = [_mm(attns[h], v_news[h]) for h in heads]
    upd = [_mm_tn(ks[h] * jnp.exp(g_lasts[h] - gcs[h]), v_news[h]) for h in heads]
    for h in heads:
        lo = h * DN_HEAD_DIM
        s_ref[h] = states[h] * jnp.exp(g_lasts[h]) + upd[h]
        o = ws[h][CHUNK:2 * CHUNK] + intra[h]
        gt = gate_ref[:, lo:lo + DN_HEAD_DIM]
        o_ref[:, lo:lo + DN_HEAD_DIM] = (_rms(o, dnorm) * _silu(gt)).astype(o_ref.dtype)


def _deltanet(proj, ba, conv_w, dt_bias, a_log, dn_norm, batch, seq):
    n = batch * seq
    nc = seq // CHUNK
    wq = 3 * DN_WIDTH
    hp = jnp.zeros((SUBLANES, LANES), F32)
    hp = hp.at[0, DN_HEADS:2 * DN_HEADS].set(dt_bias).at[1, DN_HEADS:2 * DN_HEADS].set(a_log)
    row = lambda b, ci: b * nc + ci
    return pl.pallas_call(
        _deltanet_body,
        grid=(batch, nc),
        in_specs=[pl.BlockSpec((CHUNK, wq), lambda b, ci: (row(b, ci), 0)),
                  pl.BlockSpec((CHUNK, DN_WIDTH), lambda b, ci: (row(b, ci), wq // DN_WIDTH)),
                  pl.BlockSpec((CHUNK, LANES), lambda b, ci: (row(b, ci), 0)),
                  pl.BlockSpec((CONV_K, wq), lambda b, ci: (0, 0)),
                  pl.BlockSpec((SUBLANES, LANES), lambda b, ci: (0, 0)),
                  pl.BlockSpec((1, DN_HEAD_DIM), lambda b, ci: (0, 0))],
        out_specs=pl.BlockSpec((CHUNK, DN_WIDTH), lambda b, ci: (row(b, ci), 0)),
        out_shape=jax.ShapeDtypeStruct((n, DN_WIDTH), BF16),
        scratch_shapes=[pltpu.VMEM((CHUNK + HALO, wq), F32),
                        pltpu.VMEM((DN_HEADS, DN_HEAD_DIM, DN_HEAD_DIM), F32)],
        compiler_params=_cparams(("parallel", "arbitrary")),
    )(proj, proj, ba, conv_w, hp, dn_norm.reshape(1, DN_HEAD_DIM))


def _diffattn_body(slopes_ref, q_ref, k_ref, vt_ref, lamv_ref, danorm_ref, o_ref,
                   m_ref, l_ref, a_ref, acc_ref, nrc_ref, dbias_ref, za_ref, zb_ref, zmaxa_ref, zmaxb_ref,
                   p_ref, *, tq, lam_init):
    slope = slopes_ref[pl.program_id(1)]
    nq = q_ref.shape[0] // tq
    cols = 2 * tq

    kr = lax.broadcasted_iota(jnp.int32, (tq, cols), 0)
    qc = lax.broadcasted_iota(jnp.int32, (tq, cols), 1) & (tq - 1)
    rc = (qc - kr).astype(F32)
    nrc_ref[...] = -slope * rc
    dbias_ref[...] = jnp.where((kr >> 6) <= (qc >> 6), -slope * jnp.abs(rc), -jnp.inf)

    lv = lamv_ref[...]
    lam = (jnp.exp(jnp.sum(lv[0:1] * lv[1:2], axis=-1, keepdims=True))
           - jnp.exp(jnp.sum(lv[2:3] * lv[3:4], axis=-1, keepdims=True)) + lam_init)

    def rows(j):
        return pl.ds(pl.multiple_of(j * tq, tq), tq)

    def query_block(qi, carry):
        q = q_ref[rows(qi), :]
        lane = lax.broadcasted_iota(jnp.int32, q.shape, 1)
        zero = jnp.zeros_like(q)
        qq = jnp.concatenate([jnp.where(lane < DA_HEAD_DIM, q, zero),
                              jnp.where(lane >= DA_HEAD_DIM, q, zero)], axis=0)

        def pv_update(j):
            acc_ref[...] = a_ref[...] * acc_ref[...] + jnp.dot(vt_ref[j], p_ref[...],
                                                               preferred_element_type=F32)

        def biased(j, bias_ref):
            z = _mm_nt(k_ref[rows(j), :], qq) + bias_ref[...]
            return z, jnp.max(z, axis=0, keepdims=True)

        za_ref[...], zmaxa_ref[...] = biased(0, nrc_ref)
        z0, m0 = biased(qi, dbias_ref)
        p0 = jnp.exp(z0 - m0)
        m_ref[...] = m0
        l_ref[...] = jnp.sum(p0, axis=0, keepdims=True)
        acc_ref[...] = jnp.zeros_like(acc_ref)
        p_ref[...] = p0.astype(BF16)
        a_ref[...] = jnp.ones_like(a_ref)

        def step(kj, zin_ref, zmaxin_ref, zout_ref, zmaxout_ref):
            pv_update(jnp.where(kj == 0, qi, kj - 1))
            zout_ref[...], zmaxout_ref[...] = biased(jnp.minimum(kj + 1, qi), nrc_ref)
            shift = slope * lax.convert_element_type((qi - kj) * tq, F32)
            m_old = m_ref[...]
            m_new = jnp.maximum(m_old, zmaxin_ref[...] - shift)
            p = jnp.exp(zin_ref[...] - (m_new + shift))
            scale = jnp.exp(m_old - m_new)
            l_ref[...] = scale * l_ref[...] + jnp.sum(p, axis=0, keepdims=True)
            m_ref[...] = m_new
            p_ref[...] = p.astype(BF16)
            a_ref[...] = scale

        def pair(i, carry):
            step(2 * i, za_ref, zmaxa_ref, zb_ref, zmaxb_ref)
            step(2 * i + 1, zb_ref, zmaxb_ref, za_ref, zmaxa_ref)
            return carry

        lax.fori_loop(0, qi >> 1, pair, 0)

        @pl.when((qi & 1) == 1)
        def _():
            step(qi - 1, za_ref, zmaxa_ref, zb_ref, zmaxb_ref)

        pv_update(jnp.where(qi == 0, qi, qi - 1))

        o1 = acc_ref[:, 0:tq] / l_ref[:, 0:tq]
        o2 = acc_ref[:, tq:2 * tq] / l_ref[:, tq:2 * tq]
        o = o1 - lam * o2
        o = o * lax.rsqrt(jnp.mean(o * o, axis=0, keepdims=True) + EPS) * danorm_ref[...]
        o_ref[rows(qi), :] = (o * (1.0 - lam_init)).T.astype(o_ref.dtype)
        return carry

    lax.fori_loop(0, nq, query_block, 0)


def _diffattn(qk, v_t, lam_vecs, da_norm, lam_init, batch, seq):
    n = batch * seq
    tq = v_t.shape[2]
    nq = seq // tq
    slopes = jnp.exp2(-8.0 * jnp.arange(1, DA_HEADS + 1, dtype=F32) / DA_HEADS)
    stat = pltpu.VMEM((1, 2 * tq), F32)
    tile = pltpu.VMEM((tq, 2 * tq), F32)
    grid_spec = pltpu.PrefetchScalarGridSpec(
        num_scalar_prefetch=1,
        grid=(batch, DA_HEADS),
        in_specs=[pl.BlockSpec((seq, DA_V_DIM), lambda b, h, sl: (b, h)),
                  pl.BlockSpec((seq, DA_V_DIM), lambda b, h, sl: (b, DA_HEADS + h)),
                  pl.BlockSpec((nq, DA_V_DIM, tq), lambda b, h, sl: (b, h, 0)),
                  pl.BlockSpec((4, DA_HEAD_DIM), lambda b, h, sl: (0, 0)),
                  pl.BlockSpec((DA_V_DIM, 1), lambda b, h, sl: (0, 0))],
        out_specs=pl.BlockSpec((seq, DA_V_DIM), lambda b, h, sl: (b, h)),
        scratch_shapes=[stat, stat, stat, pltpu.VMEM((DA_V_DIM, 2 * tq), F32), tile, tile, tile, tile,
                        stat, stat, pltpu.VMEM((tq, 2 * tq), BF16)],
    )
    return pl.pallas_call(
        functools.partial(_diffattn_body, tq=tq, lam_init=lam_init),
        grid_spec=grid_spec,
        out_shape=jax.ShapeDtypeStruct((n, DA_WIDTH), BF16),
        compiler_params=_cparams(("parallel", "parallel")),
    )(slopes, qk, qk, v_t, lam_vecs, da_norm.reshape(DA_V_DIM, 1))


def _attn_proj_body(x_ref, g_ref, wqk_ref, wvt_ref, qk_ref, vt_ref, *, tq):
    xn = _rms(x_ref[...], g_ref[...]).astype(BF16)
    qk_ref[...] = jnp.dot(xn, wqk_ref[...], preferred_element_type=F32).astype(qk_ref.dtype)
    for j in range(vt_ref.shape[0]):
        vt_ref[j] = lax.dot_general(wvt_ref[...], xn[j * tq:(j + 1) * tq], (((1,), (1,)), ((), ())),
                                    preferred_element_type=F32).astype(vt_ref.dtype)


def _attn_proj(x, gain, w_qk, w_vt, tq):
    n, d = x.shape
    tm = min(ROW_TM, n)
    nb = tm // tq
    return pl.pallas_call(
        functools.partial(_attn_proj_body, tq=tq),
        grid=(n // tm,),
        in_specs=[pl.BlockSpec((tm, d), lambda i: (i, 0)),
                  pl.BlockSpec((1, d), lambda i: (0, 0)),
                  pl.BlockSpec((d, 2 * DA_WIDTH), lambda i: (0, 0)),
                  pl.BlockSpec((DA_WIDTH, d), lambda i: (0, 0))],
        out_specs=[pl.BlockSpec((tm, 2 * DA_WIDTH), lambda i: (i, 0)),
                   pl.BlockSpec((nb, DA_WIDTH, tq), lambda i: (i, 0, 0))],
        out_shape=[jax.ShapeDtypeStruct((n, 2 * DA_WIDTH), BF16),
                   jax.ShapeDtypeStruct((n // tq, DA_WIDTH, tq), BF16)],
        compiler_params=_cparams(("parallel",)),
    )(x, gain.reshape(1, d), w_qk, w_vt)


def _outproj_body(h_ref, odn_ref, oda_ref, w_ref, o_ref):
    o_ref[...] = (h_ref[...]
                  + jnp.dot(odn_ref[...], w_ref[0:DN_WIDTH, :], preferred_element_type=F32)
                  + jnp.dot(oda_ref[...], w_ref[DN_WIDTH:DN_WIDTH + DA_WIDTH, :],
                            preferred_element_type=F32))


def _outproj(h, o_dn, o_da, w_out):
    n, d = h.shape
    tm = min(ROW_TM, n)
    return pl.pallas_call(
        _outproj_body,
        grid=(n // tm,),
        in_specs=[pl.BlockSpec((tm, d), lambda i: (i, 0)),
                  pl.BlockSpec((tm, DN_WIDTH), lambda i: (i, 0)),
                  pl.BlockSpec((tm, DA_WIDTH), lambda i: (i, 0)),
                  pl.BlockSpec((DN_WIDTH + DA_WIDTH, d), lambda i: (0, 0))],
        out_specs=pl.BlockSpec((tm, d), lambda i: (i, 0)),
        out_shape=jax.ShapeDtypeStruct((n, d), F32),
        compiler_params=_cparams(("parallel",)),
    )(h, o_dn, o_da, w_out)


def _ffn_body(x_ref, g_ref, w1_ref, w3_ref, w2_ref, o_ref, xn_ref):
    @pl.when(pl.program_id(1) == 0)
    def _():
        x = x_ref[...]
        xn_ref[...] = _rms(x, g_ref[...]).astype(BF16)
        o_ref[...] = x

    xn = xn_ref[...]
    g = jnp.dot(xn, w1_ref[...], preferred_element_type=F32)
    u = jnp.dot(xn, w3_ref[...], preferred_element_type=F32)
    a = (_silu(g) * u).astype(BF16)
    o_ref[...] += jnp.dot(a, w2_ref[...], preferred_element_type=F32)


def _ffn(x, gain, w1, w3, w2):
    n, d = x.shape
    dff = w1.shape[1]
    tm = min(FFN_TM, n)
    tf = FFN_TF
    return pl.pallas_call(
        _ffn_body,
        grid=(n // tm, dff // tf),
        in_specs=[pl.BlockSpec((tm, d), lambda i, f: (i, 0)),
                  pl.BlockSpec((1, d), lambda i, f: (0, 0)),
                  pl.BlockSpec((d, tf), lambda i, f: (0, f)),
                  pl.BlockSpec((d, tf), lambda i, f: (0, f)),
                  pl.BlockSpec((tf, d), lambda i, f: (f, 0))],
        out_specs=pl.BlockSpec((tm, d), lambda i, f: (i, 0)),
        out_shape=jax.ShapeDtypeStruct((n, d), F32),
        scratch_shapes=[pltpu.VMEM((tm, d), BF16)],
        compiler_params=_cparams(("parallel", "arbitrary")),
    )(x, gain.reshape(1, d), w1, w3, w2)


def _router_body(x_ref, g_ref, wt_ref, idx_ref, gate_ref):
    xn = _rms(x_ref[...], g_ref[...])
    x1, x2, x3 = _split3(xn)
    w1, w2, w3 = _split3(wt_ref[...])
    nt = lambda a, b: lax.dot_general(a, b, (((1,), (1,)), ((), ())), preferred_element_type=F32)
    logits = (nt(w3, x1) + nt(w2, x2) + nt(w1, x3)) + (nt(w2, x1) + nt(w1, x2)) + nt(w1, x1)
    e = lax.broadcasted_iota(jnp.int32, logits.shape, 0)
    m0 = jnp.max(logits, axis=0, keepdims=True)
    i0 = jnp.min(jnp.where(logits == m0, e, N_EXPERTS), axis=0, keepdims=True)
    rest = jnp.where(e == i0, -jnp.inf, logits)
    m1 = jnp.max(rest, axis=0, keepdims=True)
    i1 = jnp.min(jnp.where(rest == m1, e, N_EXPERTS), axis=0, keepdims=True)
    t = jnp.exp(m1 - m0)
    idx_ref[0:1, :] = i0
    idx_ref[1:2, :] = i1
    gate_ref[0:1, :] = 1.0 / (1.0 + t)
    gate_ref[1:2, :] = t / (1.0 + t)


def _router(x, gain, w_router):
    n, d = x.shape
    tm = min(ROW_TM, n)
    return pl.pallas_call(
        _router_body,
        grid=(n // tm,),
        in_specs=[pl.BlockSpec((tm, d), lambda i: (i, 0)),
                  pl.BlockSpec((1, d), lambda i: (0, 0)),
                  pl.BlockSpec((N_EXPERTS, d), lambda i: (0, 0))],
        out_specs=[pl.BlockSpec((TOP_K, tm), lambda i: (0, i)),
                   pl.BlockSpec((TOP_K, tm), lambda i: (0, i))],
        out_shape=[jax.ShapeDtypeStruct((TOP_K, n), jnp.int32),
                   jax.ShapeDtypeStruct((TOP_K, n), F32)],
        compiler_params=_cparams(("parallel",)),
    )(x, gain.reshape(1, d), w_router.T)


def _experts_body(be_ref, tok_ref, dst_ref, nb_ref, x_hbm, g_ref, gate_ref, w1_ref, w3_ref, w2_ref,
                  y_hbm, xg_ref, xn_ref, acc_ref, gsem, ssem, *, tm, nf):
    b = pl.program_id(0)
    f = pl.program_id(1)
    nb = nb_ref[0]
    slot = b & 1
    other = 1 - slot
    per_step, extra = divmod(tm, nf)

    def gather(blk, row, buf):
        return pltpu.make_async_copy(x_hbm.at[pl.ds(tok_ref[blk * tm + row], 1), :],
                                     xg_ref.at[buf, pl.ds(row, 1), :], gsem.at[buf])

    def scatter(blk, row, buf):
        return pltpu.make_async_copy(acc_ref.at[buf, pl.ds(row, 1), :],
                                     y_hbm.at[pl.ds(dst_ref[(blk + 1) * tm + row], 1), :], ssem.at[buf])

    def for_rows(fn):
        def step(i, carry):
            fn(i)
            return carry
        lax.fori_loop(0, tm, step, 0)

    def wait_gather(buf):
        pltpu.make_async_copy(x_hbm.at[pl.ds(0, tm), :], xg_ref.at[buf], gsem.at[buf]).wait()

    def wait_scatter(buf):
        pltpu.make_async_copy(acc_ref.at[buf], y_hbm.at[pl.ds(0, tm), :], ssem.at[buf]).wait()

    used = b < nb

    @pl.when(jnp.logical_and(f == 0, b == 0))
    def _():
        for_rows(lambda i: gather(0, i, 0).start())
        acc_ref[1] = jnp.zeros((tm, acc_ref.shape[2]), F32)

    @pl.when(jnp.logical_and(f == 0, b <= nb))
    def _():
        wait_gather(slot)

    @pl.when(jnp.logical_and(f == 0, jnp.logical_and(b <= nb, b >= 1)))
    def _():
        wait_scatter(slot)

    @pl.when(jnp.logical_and(f == 0, used))
    def _():
        xn_ref[...] = _rms(xg_ref[slot], g_ref[...]).astype(BF16)
        acc_ref[slot] = jnp.zeros((tm, acc_ref.shape[2]), F32)

    base = f * per_step + jnp.minimum(f, extra)

    @pl.when(used)
    def _():
        for t in range(per_step):
            gather(b + 1, base + t, other).start()
            scatter(b - 1, base + t, other).start()
        xn = xn_ref[...]
        g = jnp.dot(xn, w1_ref[...], preferred_element_type=F32)
        u = jnp.dot(xn, w3_ref[...], preferred_element_type=F32)
        a = (_silu(g) * u).astype(BF16)
        acc_ref[slot] += jnp.dot(a, w2_ref[...], preferred_element_type=F32)

    @pl.when(jnp.logical_and(f < extra, used))
    def _():
        gather(b + 1, base + per_step, other).start()
        scatter(b - 1, base + per_step, other).start()

    @pl.when(jnp.logical_and(f == nf - 1, used))
    def _():
        acc_ref[slot] = acc_ref[slot] * gate_ref[...]

    @pl.when(jnp.logical_and(f == 0, b == nb))
    def _():
        for_rows(lambda i: scatter(b - 1, i, other).start())
        wait_scatter(other)

    @pl.when(jnp.logical_and(f == 0, b >= nb))
    def _():
        acc_ref[0] = jnp.zeros((tm, acc_ref.shape[2]), F32)
        for_rows(lambda i: scatter(b, i, 0).start())
        wait_scatter(0)


def _experts(x, gain, w1, w3, w2, plan, tm):
    block_e, row_tok, row_dst, row_gate, n_used, n_grid = plan
    n, d = x.shape
    dff = w1.shape[2]
    tf = MOE_TF
    nf = dff // tf

    def widx(b, f, be, tok, dst, nb):
        live = b < nb[0]
        last = jnp.maximum(nb[0] - 1, 0)
        return jnp.where(live, be[b], be[last]), jnp.where(live, f, nf - 1)

    def w13_map(b, f, be, tok, dst, nb):
        e, ff = widx(b, f, be, tok, dst, nb)
        return e, 0, ff

    def w2_map(b, f, be, tok, dst, nb):
        e, ff = widx(b, f, be, tok, dst, nb)
        return e, ff, 0

    grid_spec = pltpu.PrefetchScalarGridSpec(
        num_scalar_prefetch=4,
        grid=(n_grid, nf),
        in_specs=[pl.BlockSpec(memory_space=pl.ANY),
                  pl.BlockSpec((1, d), lambda b, f, *_: (0, 0)),
                  pl.BlockSpec((tm, 1), lambda b, f, *_: (b, 0)),
                  pl.BlockSpec((None, d, tf), w13_map),
                  pl.BlockSpec((None, d, tf), w13_map),
                  pl.BlockSpec((None, tf, d), w2_map)],
        out_specs=pl.BlockSpec(memory_space=pl.ANY),
        scratch_shapes=[pltpu.VMEM((2, tm, d), F32), pltpu.VMEM((tm, d), BF16),
                        pltpu.VMEM((2, tm, d), F32),
                        pltpu.SemaphoreType.DMA((2,)), pltpu.SemaphoreType.DMA((2,))],
    )
    n_y = (n_grid + 1) * tm
    return pl.pallas_call(
        functools.partial(_experts_body, tm=tm, nf=nf),
        grid_spec=grid_spec,
        out_shape=jax.ShapeDtypeStruct((n_y, d), F32),
        compiler_params=_cparams(("arbitrary", "arbitrary")),
    )(block_e, row_tok, row_dst, n_used, x, gain.reshape(1, d), row_gate, w1, w3, w2)


def _route_plan(idx, gates, n, tm):
    n_assign = n * TOP_K
    flat_e = idx.T.reshape(n_assign)
    onehot = (flat_e[:, None] == jnp.arange(N_EXPERTS, dtype=jnp.int32)[None, :]).astype(jnp.int32)
    csum = jnp.cumsum(onehot, axis=0)
    counts = csum[-1]
    rank = jnp.sum((csum - onehot) * onehot, axis=1)
    padded = (counts + tm - 1) // tm * tm
    pad_end = jnp.cumsum(padded)
    pad_start = pad_end - padded
    dest = pad_start[flat_e] + rank
    n_blocks = -(-n_assign // tm) + N_EXPERTS
    n_grid = n_blocks + 1
    n_rows = n_grid * tm
    row_a = jnp.full((n_rows + tm,), -1, jnp.int32).at[tm + dest].set(
        jnp.arange(n_assign, dtype=jnp.int32))
    is_pad = row_a < 0
    a = jnp.maximum(row_a, 0)
    row_tok = (a // TOP_K)[tm:]
    row_gate = jnp.where(is_pad, 0.0, gates.T.reshape(n_assign)[a])[tm:]
    row_dst = jnp.where(is_pad, n_assign + jnp.cumsum(is_pad.astype(jnp.int32)) - 1,
                        (a % TOP_K) * n + a // TOP_K)
    block_e = jnp.minimum(
        jnp.searchsorted(pad_end, jnp.arange(n_grid, dtype=jnp.int32) * tm, side='right'),
        N_EXPERTS - 1).astype(jnp.int32)
    n_used = (pad_end[-1] // tm).astype(jnp.int32).reshape(1)
    return block_e, row_tok, row_dst, row_gate.reshape(n_rows, 1), n_used, n_grid


def _ple_body(*refs, final, combine):
    if combine:
        h_ref, y0_ref, y1_ref, p_ref, wp_ref, pn_ref, wg_ref, nf_ref, o_ref = refs
        h = h_ref[...] + (y0_ref[...] + y1_ref[...])
    else:
        h_ref, p_ref, wp_ref, pn_ref, wg_ref, nf_ref, o_ref = refs
        h = h_ref[...]
    e = _rms(_mm(p_ref[...], wp_ref[...]), pn_ref[...])
    out = h + _sigmoid(_mm(h, wg_ref[...])) * e
    if final:
        out = _rms(out, nf_ref[...])
    o_ref[...] = out


def _ple(h, p, w_proj, p_norm, w_gate, norm_final, final, y=None):
    n, d = h.shape
    dp = p.shape[1]
    tm = min(ROW_TM, n)
    row = lambda i: (i, 0)
    fixed = lambda i: (0, 0)
    specs = [pl.BlockSpec((tm, d), row)]
    args = [h]
    if y is not None:
        specs += [pl.BlockSpec((tm, d), row), pl.BlockSpec((tm, d), lambda i: (n // tm + i, 0))]
        args += [y, y]
    specs += [pl.BlockSpec((tm, dp), row), pl.BlockSpec((dp, d), fixed), pl.BlockSpec((1, d), fixed),
              pl.BlockSpec((d, d), fixed), pl.BlockSpec((1, d), fixed)]
    args += [p, w_proj, p_norm.reshape(1, d), w_gate, norm_final.reshape(1, d)]
    return pl.pallas_call(
        functools.partial(_ple_body, final=final, combine=y is not None),
        grid=(n // tm,),
        in_specs=specs,
        out_specs=pl.BlockSpec((tm, d), row),
        out_shape=jax.ShapeDtypeStruct((n, d), F32),
        compiler_params=_cparams(("parallel",)),
    )(*args)


def kernel(x, p, norm_mix, w_in, conv_w, dn_a_log, dn_dt_bias, dn_norm, da_lambda_q1, da_lambda_k1, da_lambda_q2, da_lambda_k2, da_norm, w_out, norm_ffn, ffn_w1, ffn_w3, ffn_w2, moe_router, moe_w1, moe_w3, moe_w2, ple_proj, ple_norm, ple_gate, norm_final):
    batch, seq, d = x.shape
    n = batch * seq
    depth = w_in.shape[0]
    h = x.reshape(n, d)
    wq = 3 * DN_WIDTH
    for i in range(depth):
        wi = w_in[i]
        c0 = wq + DN_WIDTH
        c1 = c0 + 2 * DN_HEADS
        w_dn = wi[:, :c0].astype(BF16)
        w_ba = jnp.pad(wi[:, c0:c1], ((0, 0), (0, LANES - 2 * DN_HEADS))).astype(BF16)
        att_scale = DA_HEAD_DIM ** -0.5
        w_qk = jnp.concatenate([wi[:, c1:c1 + DA_WIDTH] * att_scale,
                                wi[:, c1 + DA_WIDTH:c1 + 2 * DA_WIDTH]], axis=1).astype(BF16)
        w_vt = wi[:, c1 + 2 * DA_WIDTH:].T.astype(BF16)
        proj_dn, ba = _dn_proj(h, norm_mix[i], w_dn, w_ba, 2 * DN_WIDTH)
        qk, v_t = _attn_proj(h, norm_mix[i], w_qk, w_vt, min(ATT_TQ, seq))

        o_dn = _deltanet(proj_dn, ba, conv_w[i], dn_dt_bias[i], dn_a_log[i], dn_norm[i], batch, seq)
        lam_init = 0.8 - 0.6 * math.exp(-0.3 * i)
        lam_vecs = jnp.stack([da_lambda_q1[i], da_lambda_k1[i], da_lambda_q2[i], da_lambda_k2[i]])
        o_da = _diffattn(qk, v_t, lam_vecs, da_norm[i], lam_init, batch, seq)
        h = _outproj(h, o_dn, o_da, w_out[i].astype(BF16))

        j = i // 2
        last = i == depth - 1
        wp = ple_proj[i].astype(BF16)
        wg = ple_gate[i].astype(BF16)
        if i % 2 == 0:
            h = _ffn(h, norm_ffn[i], ffn_w1[j].astype(BF16), ffn_w3[j].astype(BF16),
                     ffn_w2[j].astype(BF16))
            h = _ple(h, p[i].reshape(n, -1), wp, ple_norm[i], wg, norm_final, last)
        else:
            idx, gates = _router(h, norm_ffn[i], moe_router[j])
            tm = min(MOE_TM, n)
            plan = _route_plan(idx, gates, n, tm)
            y = _experts(h, norm_ffn[i], moe_w1[j].astype(BF16), moe_w3[j].astype(BF16),
                         moe_w2[j].astype(BF16), plan, tm)
            h = _ple(h, p[i].reshape(n, -1), wp, ple_norm[i], wg, norm_final, last, y=y)
    return h.reshape(batch, seq, d)
```

```python
import functools
import math

import jax
import jax.numpy as jnp
from jax import lax
from jax.experimental import pallas as pl
from jax.experimental.pallas import tpu as pltpu

F32 = jnp.float32
BF16 = jnp.bfloat16
EPS = 1e-6

CHUNK = 64
DN_HEADS = 8
DN_HEAD_DIM = 128
DN_WIDTH = DN_HEADS * DN_HEAD_DIM
CONV_K = 4
DN_CPS = 2
DA_HEADS = 8
DA_HEAD_DIM = 64
DA_V_DIM = 2 * DA_HEAD_DIM
DA_WIDTH = DA_HEADS * DA_V_DIM
N_EXPERTS = 8
TOP_K = 2
LANES = 128
SUBLANES = 8
VMEM_LIMIT = 56 * 1024 * 1024

PROJ_TM = 512
FFN_TM = 512
FFN_TF = 512
MOE_TM = 512
MOE_TF = 1024
ATT_TQ = 256
ROW_TM = 512
HALO = SUBLANES


def _cparams(sem):
    return pltpu.CompilerParams(dimension_semantics=sem, vmem_limit_bytes=VMEM_LIMIT)


def _sigmoid(x):
    return 1.0 / (1.0 + jnp.exp(-x))


def _silu(x):
    return x * _sigmoid(x)


def _softplus(x):
    return jnp.maximum(x, 0.0) + jnp.log1p(jnp.exp(-jnp.abs(x)))


def _rms(x, gain):
    y = x * lax.rsqrt(jnp.mean(x * x, axis=-1, keepdims=True) + EPS)
    return y * gain


def _mm(a, b):
    return jnp.dot(a.astype(BF16), b.astype(BF16), preferred_element_type=F32)


def _mm_nt(a, b):
    return lax.dot_general(a.astype(BF16), b.astype(BF16), (((1,), (1,)), ((), ())),
                           preferred_element_type=F32)


def _mm_tn(a, b):
    return lax.dot_general(a.astype(BF16), b.astype(BF16), (((0,), (0,)), ((), ())),
                           preferred_element_type=F32)


def _split3(a):
    a1 = a.astype(BF16)
    r1 = a - a1.astype(F32)
    a2 = r1.astype(BF16)
    a3 = (r1 - a2.astype(F32)).astype(BF16)
    return a1, a2, a3


def _dn_proj_body(x_ref, g_ref, w_ref, wba_ref, o_ref, ba_ref, xn_ref):
    @pl.when(pl.program_id(1) == 0)
    def _():
        xn_ref[...] = _rms(x_ref[...], g_ref[...]).astype(BF16)
        ba_ref[...] = jnp.dot(xn_ref[...], wba_ref[...], preferred_element_type=F32)

    o_ref[...] = jnp.dot(xn_ref[...], w_ref[...], preferred_element_type=F32)


def _dn_proj(x, gain, w, w_ba, tn):
    n, d = x.shape
    nout = w.shape[1]
    tm = min(PROJ_TM, n)
    return pl.pallas_call(
        _dn_proj_body,
        grid=(n // tm, nout // tn),
        in_specs=[pl.BlockSpec((tm, d), lambda i, j: (i, 0)),
                  pl.BlockSpec((1, d), lambda i, j: (0, 0)),
                  pl.BlockSpec((d, tn), lambda i, j: (0, j)),
                  pl.BlockSpec((d, LANES), lambda i, j: (0, 0))],
        out_specs=[pl.BlockSpec((tm, tn), lambda i, j: (i, j)),
                   pl.BlockSpec((tm, LANES), lambda i, j: (i, 0))],
        out_shape=[jax.ShapeDtypeStruct((n, nout), F32), jax.ShapeDtypeStruct((n, LANES), F32)],
        scratch_shapes=[pltpu.VMEM((tm, d), BF16)],
        compiler_params=_cparams(("parallel", "arbitrary")),
    )(x, gain.reshape(1, d), w, w_ba)


def _unit_lower_inverse(ms, r, c):
    eye = (r == c).astype(F32)
    bd16 = (r >> 4) == (c >> 4)
    in32 = ((r >> 5) == (c >> 5)) & ((r >> 4) > (c >> 4))
    in64 = (r >> 5) > (c >> 5)
    xs = [jnp.where(bd16, -m, 0.0) for m in ms]
    ps = [eye + x for x in xs]
    xs = [_mm(x, x) for x in xs]
    for _ in range(2):
        ts = [_mm(jnp.concatenate([p, x], axis=0), x) for p, x in zip(ps, xs)]
        ps = [p + t[0:CHUNK] for p, t in zip(ps, ts)]
        xs = [t[CHUNK:2 * CHUNK] for t in ts]
    ps = [p + _mm(p, x) for p, x in zip(ps, xs)]
    for sel in (in32, in64):
        ls = [_mm(p, jnp.where(sel, m, 0.0)) for p, m in zip(ps, ms)]
        ps = [p - _mm(l, p) for p, l in zip(ps, ls)]
    return ps


def _deltanet_body(qkv_ref, gate_ref, ba_ref, convw_ref, hp_ref, dnorm_ref, o_ref, ext_ref, s_ref):
    ci = pl.program_id(1)
    rows = DN_CPS * CHUNK

    @pl.when(ci == 0)
    def _():
        ext_ref[0:HALO, :] = jnp.zeros((HALO, 3 * DN_WIDTH), F32)
        s_ref[...] = jnp.zeros_like(s_ref)

    ext_ref[HALO:HALO + rows, :] = qkv_ref[...]
    conv = convw_ref[0:1, :] * ext_ref[HALO - 3:HALO - 3 + rows, :]
    for j in range(1, CONV_K):
        conv = conv + convw_ref[j:j + 1, :] * ext_ref[HALO - 3 + j:HALO - 3 + j + rows, :]
    ext_ref[0:HALO, :] = ext_ref[rows:rows + HALO, :]
    act = _silu(conv)

    ba = ba_ref[...]
    beta_all = _sigmoid(ba)
    g_all = -jnp.exp(hp_ref[1:2, :]) * _softplus(ba + hp_ref[0:1, :])
    r = lax.broadcasted_iota(jnp.int32, (CHUNK, CHUNK), 0)
    c = lax.broadcasted_iota(jnp.int32, (CHUNK, CHUNK), 1)
    incl = r >= c
    strict = r > c
    tri = incl.astype(BF16)
    dnorm = dnorm_ref[...]
    scale = DN_HEAD_DIM ** -0.5
    heads = range(DN_HEADS)
    subs = range(DN_CPS)
    chains = [(sc, h) for sc in subs for h in heads]
    span = lambda sc: slice(sc * CHUNK, (sc + 1) * CHUNK)

    def cumulative(g):
        g1, g2, g3 = _split3(g)
        return (jnp.dot(tri, g1, preferred_element_type=F32) + jnp.dot(tri, g2, preferred_element_type=F32)
                + jnp.dot(tri, g3, preferred_element_type=F32))

    gcum = [cumulative(g_all[span(sc)]) for sc in subs]
    gcum_t = [g.T for g in gcum]

    def l2n(x):
        return x * lax.rsqrt(jnp.sum(x * x, axis=-1, keepdims=True) + EPS)

    qs, ks, vs, betas, gcs, g_lasts, decays = {}, {}, {}, {}, {}, {}, {}
    for sc, h in chains:
        lo = h * DN_HEAD_DIM
        qs[sc, h] = l2n(act[span(sc), lo:lo + DN_HEAD_DIM]) * scale
        ks[sc, h] = l2n(act[span(sc), DN_WIDTH + lo:DN_WIDTH + lo + DN_HEAD_DIM])
        vs[sc, h] = act[span(sc), 2 * DN_WIDTH + lo:2 * DN_WIDTH + lo + DN_HEAD_DIM]
        betas[sc, h] = beta_all[span(sc), h:h + 1]
        gcs[sc, h] = gcum[sc][:, DN_HEADS + h:DN_HEADS + h + 1]
        g_lasts[sc, h] = gcum[sc][CHUNK - 1:CHUNK, DN_HEADS + h:DN_HEADS + h + 1]
        decays[sc, h] = jnp.where(incl, jnp.exp(jnp.where(
            incl, gcs[sc, h] - gcum_t[sc][DN_HEADS + h:DN_HEADS + h + 1, :], 0.0)), 0.0)
    kbs = {ch: ks[ch] * betas[ch] for ch in chains}

    kq = {ch: _mm_nt(jnp.concatenate([kbs[ch], qs[ch]], axis=0), ks[ch]) for ch in chains}
    ms = [jnp.where(strict, kq[ch][0:CHUNK] * decays[ch], 0.0) for ch in chains]
    attns = {ch: jnp.where(incl, kq[ch][CHUNK:2 * CHUNK] * decays[ch], 0.0) for ch in chains}
    ts = dict(zip(chains, _unit_lower_inverse(ms, r, c)))
    uw = {ch: _mm(ts[ch], jnp.concatenate([vs[ch] * betas[ch], kbs[ch] * jnp.exp(gcs[ch])], axis=1))
          for ch in chains}
    states = [s_ref[h] for h in heads]
    for sc in subs:
        ws = [_mm(jnp.concatenate([uw[sc, h][:, DN_HEAD_DIM:], qs[sc, h] * jnp.exp(gcs[sc, h])], axis=0),
                  states[h]) for h in heads]
        v_news = [uw[sc, h][:, 0:DN_HEAD_DIM] - ws[h][0:CHUNK] for h in heads]
        intra = [_mm(attns[sc, h], v_news[h]) for h in heads]
        upd = [_mm_tn(ks[sc, h] * jnp.exp(g_lasts[sc, h] - gcs[sc, h]), v_news[h]) for h in heads]
        states = [states[h] * jnp.exp(g_lasts[sc, h]) + upd[h] for h in heads]
        for h in heads:
            lo = h * DN_HEAD_DIM
            o = ws[h][CHUNK:2 * CHUNK] + intra[h]
            gt = gate_ref[span(sc), lo:lo + DN_HEAD_DIM]
            o_ref[span(sc), lo:lo + DN_HEAD_DIM] = (_rms(o, dnorm) * _silu(gt)).astype(o_ref.dtype)
    for h in heads:
        s_ref[h] = states[h]


def _deltanet(proj, ba, conv_w, dt_bias, a_log, dn_norm, batch, seq):
    n = batch * seq
    rows = DN_CPS * CHUNK
    nc = seq // rows
    wq = 3 * DN_WIDTH
    hp = jnp.zeros((SUBLANES, LANES), F32)
    hp = hp.at[0, DN_HEADS:2 * DN_HEADS].set(dt_bias).at[1, DN_HEADS:2 * DN_HEADS].set(a_log)
    row = lambda b, ci: b * nc + ci
    return pl.pallas_call(
        _deltanet_body,
        grid=(batch, nc),
        in_specs=[pl.BlockSpec((rows, wq), lambda b, ci: (row(b, ci), 0)),
                  pl.BlockSpec((rows, DN_WIDTH), lambda b, ci: (row(b, ci), wq // DN_WIDTH)),
                  pl.BlockSpec((rows, LANES), lambda b, ci: (row(b, ci), 0)),
                  pl.BlockSpec((CONV_K, wq), lambda b, ci: (0, 0)),
                  pl.BlockSpec((SUBLANES, LANES), lambda b, ci: (0, 0)),
                  pl.BlockSpec((1, DN_HEAD_DIM), lambda b, ci: (0, 0))],
        out_specs=pl.BlockSpec((rows, DN_WIDTH), lambda b, ci: (row(b, ci), 0)),
        out_shape=jax.ShapeDtypeStruct((n, DN_WIDTH), BF16),
        scratch_shapes=[pltpu.VMEM((rows + HALO, wq), F32),
                        pltpu.VMEM((DN_HEADS, DN_HEAD_DIM, DN_HEAD_DIM), F32)],
        compiler_params=_cparams(("parallel", "arbitrary")),
    )(proj, proj, ba, conv_w, hp, dn_norm.reshape(1, DN_HEAD_DIM))


def _diffattn_body(slopes_ref, q_ref, k_ref, vt_ref, lamv_ref, danorm_ref, o_ref,
                   m_ref, l_ref, a_ref, acc_ref, nrc_ref, dbias_ref, za_ref, zb_ref, zmaxa_ref, zmaxb_ref,
                   p_ref, *, tq, lam_init):
    slope = slopes_ref[pl.program_id(1)]
    nq = q_ref.shape[0] // tq
    cols = 2 * tq

    kr = lax.broadcasted_iota(jnp.int32, (tq, cols), 0)
    qc = lax.broadcasted_iota(jnp.int32, (tq, cols), 1) & (tq - 1)
    rc = (qc - kr).astype(F32)
    nrc_ref[...] = -slope * rc
    dbias_ref[...] = jnp.where((kr >> 6) <= (qc >> 6), -slope * jnp.abs(rc), -jnp.inf)

    lv = lamv_ref[...]
    lam = (jnp.exp(jnp.sum(lv[0:1] * lv[1:2], axis=-1, keepdims=True))
           - jnp.exp(jnp.sum(lv[2:3] * lv[3:4], axis=-1, keepdims=True)) + lam_init)

    def rows(j):
        return pl.ds(pl.multiple_of(j * tq, tq), tq)

    def query_block(qi, carry):
        q = q_ref[rows(qi), :]
        lane = lax.broadcasted_iota(jnp.int32, q.shape, 1)
        zero = jnp.zeros_like(q)
        qq = jnp.concatenate([jnp.where(lane < DA_HEAD_DIM, q, zero),
                              jnp.where(lane >= DA_HEAD_DIM, q, zero)], axis=0)

        def pv_update(j):
            acc_ref[...] = a_ref[...] * acc_ref[...] + jnp.dot(vt_ref[j], p_ref[...],
                                                               preferred_element_type=F32)

        def biased(j, bias_ref):
            z = _mm_nt(k_ref[rows(j), :], qq) + bias_ref[...]
            return z, jnp.max(z, axis=0, keepdims=True)

        za_ref[...], zmaxa_ref[...] = biased(0, nrc_ref)
        z0, m0 = biased(qi, dbias_ref)
        p0 = jnp.exp(z0 - m0)
        m_ref[...] = m0
        l_ref[...] = jnp.sum(p0, axis=0, keepdims=True)
        acc_ref[...] = jnp.zeros_like(acc_ref)
        p_ref[...] = p0.astype(BF16)
        a_ref[...] = jnp.ones_like(a_ref)

        def step(kj, zin_ref, zmaxin_ref, zout_ref, zmaxout_ref):
            pv_update(jnp.where(kj == 0, qi, kj - 1))
            zout_ref[...], zmaxout_ref[...] = biased(jnp.minimum(kj + 1, qi), nrc_ref)
            shift = slope * lax.convert_element_type((qi - kj) * tq, F32)
            m_old = m_ref[...]
            m_new = jnp.maximum(m_old, zmaxin_ref[...] - shift)
            p = jnp.exp(zin_ref[...] - (m_new + shift))
            scale = jnp.exp(m_old - m_new)
            l_ref[...] = scale * l_ref[...] + jnp.sum(p, axis=0, keepdims=True)
            m_ref[...] = m_new
            p_ref[...] = p.astype(BF16)
            a_ref[...] = scale

        def pair(i, carry):
            step(2 * i, za_ref, zmaxa_ref, zb_ref, zmaxb_ref)
            step(2 * i + 1, zb_ref, zmaxb_ref, za_ref, zmaxa_ref)
            return carry

        lax.fori_loop(0, qi >> 1, pair, 0)

        @pl.when((qi & 1) == 1)
        def _():
            step(qi - 1, za_ref, zmaxa_ref, zb_ref, zmaxb_ref)

        pv_update(jnp.where(qi == 0, qi, qi - 1))

        o1 = acc_ref[:, 0:tq] / l_ref[:, 0:tq]
        o2 = acc_ref[:, tq:2 * tq] / l_ref[:, tq:2 * tq]
        o = o1 - lam * o2
        o = o * lax.rsqrt(jnp.mean(o * o, axis=0, keepdims=True) + EPS) * danorm_ref[...]
        o_ref[rows(qi), :] = (o * (1.0 - lam_init)).T.astype(o_ref.dtype)
        return carry

    lax.fori_loop(0, nq, query_block, 0)


def _diffattn(qk, v_t, lam_vecs, da_norm, lam_init, batch, seq):
    n = batch * seq
    tq = v_t.shape[2]
    nq = seq // tq
    slopes = jnp.exp2(-8.0 * jnp.arange(1, DA_HEADS + 1, dtype=F32) / DA_HEADS)
    stat = pltpu.VMEM((1, 2 * tq), F32)
    tile = pltpu.VMEM((tq, 2 * tq), F32)
    grid_spec = pltpu.PrefetchScalarGridSpec(
        num_scalar_prefetch=1,
        grid=(batch, DA_HEADS),
        in_specs=[pl.BlockSpec((seq, DA_V_DIM), lambda b, h, sl: (b, h)),
                  pl.BlockSpec((seq, DA_V_DIM), lambda b, h, sl: (b, DA_HEADS + h)),
                  pl.BlockSpec((nq, DA_V_DIM, tq), lambda b, h, sl: (b, h, 0)),
                  pl.BlockSpec((4, DA_HEAD_DIM), lambda b, h, sl: (0, 0)),
                  pl.BlockSpec((DA_V_DIM, 1), lambda b, h, sl: (0, 0))],
        out_specs=pl.BlockSpec((seq, DA_V_DIM), lambda b, h, sl: (b, h)),
        scratch_shapes=[stat, stat, stat, pltpu.VMEM((DA_V_DIM, 2 * tq), F32), tile, tile, tile, tile,
                        stat, stat, pltpu.VMEM((tq, 2 * tq), BF16)],
    )
    return pl.pallas_call(
        functools.partial(_diffattn_body, tq=tq, lam_init=lam_init),
        grid_spec=grid_spec,
        out_shape=jax.ShapeDtypeStruct((n, DA_WIDTH), BF16),
        compiler_params=_cparams(("parallel", "parallel")),
    )(slopes, qk, qk, v_t, lam_vecs, da_norm.reshape(DA_V_DIM, 1))


def _attn_proj_body(x_ref, g_ref, wqk_ref, wvt_ref, qk_ref, vt_ref, *, tq):
    xn = _rms(x_ref[...], g_ref[...]).astype(BF16)
    qk_ref[...] = jnp.dot(xn, wqk_ref[...], preferred_element_type=F32).astype(qk_ref.dtype)
    for j in range(vt_ref.shape[0]):
        vt_ref[j] = lax.dot_general(wvt_ref[...], xn[j * tq:(j + 1) * tq], (((1,), (1,)), ((), ())),
                                    preferred_element_type=F32).astype(vt_ref.dtype)


def _attn_proj(x, gain, w_qk, w_vt, tq):
    n, d = x.shape
    tm = min(ROW_TM, n)
    nb = tm // tq
    return pl.pallas_call(
        functools.partial(_attn_proj_body, tq=tq),
        grid=(n // tm,),
        in_specs=[pl.BlockSpec((tm, d), lambda i: (i, 0)),
                  pl.BlockSpec((1, d), lambda i: (0, 0)),
                  pl.BlockSpec((d, 2 * DA_WIDTH), lambda i: (0, 0)),
                  pl.BlockSpec((DA_WIDTH, d), lambda i: (0, 0))],
        out_specs=[pl.BlockSpec((tm, 2 * DA_WIDTH), lambda i: (i, 0)),
                   pl.BlockSpec((nb, DA_WIDTH, tq), lambda i: (i, 0, 0))],
        out_shape=[jax.ShapeDtypeStruct((n, 2 * DA_WIDTH), BF16),
                   jax.ShapeDtypeStruct((n // tq, DA_WIDTH, tq), BF16)],
        compiler_params=_cparams(("parallel",)),
    )(x, gain.reshape(1, d), w_qk, w_vt)


def _outproj_body(h_ref, odn_ref, oda_ref, w_ref, o_ref):
    o_ref[...] = (h_ref[...]
                  + jnp.dot(odn_ref[...], w_ref[0:DN_WIDTH, :], preferred_element_type=F32)
                  + jnp.dot(oda_ref[...], w_ref[DN_WIDTH:DN_WIDTH + DA_WIDTH, :],
                            preferred_element_type=F32))


def _outproj(h, o_dn, o_da, w_out):
    n, d = h.shape
    tm = min(ROW_TM, n)
    return pl.pallas_call(
        _outproj_body,
        grid=(n // tm,),
        in_specs=[pl.BlockSpec((tm, d), lambda i: (i, 0)),
                  pl.BlockSpec((tm, DN_WIDTH), lambda i: (i, 0)),
                  pl.BlockSpec((tm, DA_WIDTH), lambda i: (i, 0)),
                  pl.BlockSpec((DN_WIDTH + DA_WIDTH, d), lambda i: (0, 0))],
        out_specs=pl.BlockSpec((tm, d), lambda i: (i, 0)),
        out_shape=jax.ShapeDtypeStruct((n, d), F32),
        compiler_params=_cparams(("parallel",)),
    )(h, o_dn, o_da, w_out)


def _ffn_body(x_ref, g_ref, w1_ref, w3_ref, w2_ref, o_ref, xn_ref):
    @pl.when(pl.program_id(1) == 0)
    def _():
        x = x_ref[...]
        xn_ref[...] = _rms(x, g_ref[...]).astype(BF16)
        o_ref[...] = x

    xn = xn_ref[...]
    g = jnp.dot(xn, w1_ref[...], preferred_element_type=F32)
    u = jnp.dot(xn, w3_ref[...], preferred_element_type=F32)
    a = (_silu(g) * u).astype(BF16)
    o_ref[...] += jnp.dot(a, w2_ref[...], preferred_element_type=F32)


def _ffn(x, gain, w1, w3, w2):
    n, d = x.shape
    dff = w1.shape[1]
    tm = min(FFN_TM, n)
    tf = FFN_TF
    return pl.pallas_call(
        _ffn_body,
        grid=(n // tm, dff // tf),
        in_specs=[pl.BlockSpec((tm, d), lambda i, f: (i, 0)),
                  pl.BlockSpec((1, d), lambda i, f: (0, 0)),
                  pl.BlockSpec((d, tf), lambda i, f: (0, f)),
                  pl.BlockSpec((d, tf), lambda i, f: (0, f)),
                  pl.BlockSpec((tf, d), lambda i, f: (f, 0))],
        out_specs=pl.BlockSpec((tm, d), lambda i, f: (i, 0)),
        out_shape=jax.ShapeDtypeStruct((n, d), F32),
        scratch_shapes=[pltpu.VMEM((tm, d), BF16)],
        compiler_params=_cparams(("parallel", "arbitrary")),
    )(x, gain.reshape(1, d), w1, w3, w2)


def _router_body(x_ref, g_ref, wt_ref, idx_ref, gate_ref):
    xn = _rms(x_ref[...], g_ref[...])
    x1, x2, x3 = _split3(xn)
    w1, w2, w3 = _split3(wt_ref[...])
    nt = lambda a, b: lax.dot_general(a, b, (((1,), (1,)), ((), ())), preferred_element_type=F32)
    logits = (nt(w3, x1) + nt(w2, x2) + nt(w1, x3)) + (nt(w2, x1) + nt(w1, x2)) + nt(w1, x1)
    e = lax.broadcasted_iota(jnp.int32, logits.shape, 0)
    m0 = jnp.max(logits, axis=0, keepdims=True)
    i0 = jnp.min(jnp.where(logits == m0, e, N_EXPERTS), axis=0, keepdims=True)
    rest = jnp.where(e == i0, -jnp.inf, logits)
    m1 = jnp.max(rest, axis=0, keepdims=True)
    i1 = jnp.min(jnp.where(rest == m1, e, N_EXPERTS), axis=0, keepdims=True)
    t = jnp.exp(m1 - m0)
    idx_ref[0:1, :] = i0
    idx_ref[1:2, :] = i1
    gate_ref[0:1, :] = 1.0 / (1.0 + t)
    gate_ref[1:2, :] = t / (1.0 + t)


def _router(x, gain, w_router):
    n, d = x.shape
    tm = min(ROW_TM, n)
    return pl.pallas_call(
        _router_body,
        grid=(n // tm,),
        in_specs=[pl.BlockSpec((tm, d), lambda i: (i, 0)),
                  pl.BlockSpec((1, d), lambda i: (0, 0)),
                  pl.BlockSpec((N_EXPERTS, d), lambda i: (0, 0))],
        out_specs=[pl.BlockSpec((TOP_K, tm), lambda i: (0, i)),
                   pl.BlockSpec((TOP_K, tm), lambda i: (0, i))],
        out_shape=[jax.ShapeDtypeStruct((TOP_K, n), jnp.int32),
                   jax.ShapeDtypeStruct((TOP_K, n), F32)],
        compiler_params=_cparams(("parallel",)),
    )(x, gain.reshape(1, d), w_router.T)


def _experts_body(be_ref, tok_ref, dst_ref, nb_ref, x_hbm, g_ref, gate_ref, w1_ref, w3_ref, w2_ref,
                  y_hbm, xg_ref, xn_ref, acc_ref, gsem, ssem, *, tm, nf):
    b = pl.program_id(0)
    f = pl.program_id(1)
    nb = nb_ref[0]
    slot = b & 1
    other = 1 - slot
    per_step, extra = divmod(tm, nf)

    def gather(blk, row, buf):
        return pltpu.make_async_copy(x_hbm.at[pl.ds(tok_ref[blk * tm + row], 1), :],
                                     xg_ref.at[buf, pl.ds(row, 1), :], gsem.at[buf])

    def scatter(blk, row, buf):
        return pltpu.make_async_copy(acc_ref.at[buf, pl.ds(row, 1), :],
                                     y_hbm.at[pl.ds(dst_ref[(blk + 1) * tm + row], 1), :], ssem.at[buf])

    def for_rows(fn):
        def step(i, carry):
            fn(i)
            return carry
        lax.fori_loop(0, tm, step, 0)

    def wait_gather(buf):
        pltpu.make_async_copy(x_hbm.at[pl.ds(0, tm), :], xg_ref.at[buf], gsem.at[buf]).wait()

    def wait_scatter(buf):
        pltpu.make_async_copy(acc_ref.at[buf], y_hbm.at[pl.ds(0, tm), :], ssem.at[buf]).wait()

    used = b < nb

    @pl.when(jnp.logical_and(f == 0, b == 0))
    def _():
        for_rows(lambda i: gather(0, i, 0).start())
        acc_ref[1] = jnp.zeros((tm, acc_ref.shape[2]), F32)

    @pl.when(jnp.logical_and(f == 0, b <= nb))
    def _():
        wait_gather(slot)

    @pl.when(jnp.logical_and(f == 0, jnp.logical_and(b <= nb, b >= 1)))
    def _():
        wait_scatter(slot)

    @pl.when(jnp.logical_and(f == 0, used))
    def _():
        xn_ref[...] = _rms(xg_ref[slot], g_ref[...]).astype(BF16)
        acc_ref[slot] = jnp.zeros((tm, acc_ref.shape[2]), F32)

    base = f * per_step + jnp.minimum(f, extra)

    @pl.when(used)
    def _():
        for t in range(per_step):
            gather(b + 1, base + t, other).start()
            scatter(b - 1, base + t, other).start()
        xn = xn_ref[...]
        g = jnp.dot(xn, w1_ref[...], preferred_element_type=F32)
        u = jnp.dot(xn, w3_ref[...], preferred_element_type=F32)
        a = (_silu(g) * u).astype(BF16)
        acc_ref[slot] += jnp.dot(a, w2_ref[...], preferred_element_type=F32)

    @pl.when(jnp.logical_and(f < extra, used))
    def _():
        gather(b + 1, base + per_step, other).start()
        scatter(b - 1, base + per_step, other).start()

    @pl.when(jnp.logical_and(f == nf - 1, used))
    def _():
        acc_ref[slot] = acc_ref[slot] * gate_ref[...]

    @pl.when(jnp.logical_and(f == 0, b == nb))
    def _():
        for_rows(lambda i: scatter(b - 1, i, other).start())
        wait_scatter(other)

    @pl.when(jnp.logical_and(f == 0, b >= nb))
    def _():
        acc_ref[0] = jnp.zeros((tm, acc_ref.shape[2]), F32)
        for_rows(lambda i: scatter(b, i, 0).start())
        wait_scatter(0)


def _experts(x, gain, w1, w3, w2, plan, tm):
    block_e, row_tok, row_dst, row_gate, n_used, n_grid = plan
    n, d = x.shape
    dff = w1.shape[2]
    tf = MOE_TF
    nf = dff // tf

    def widx(b, f, be, tok, dst, nb):
        live = b < nb[0]
        last = jnp.maximum(nb[0] - 1, 0)
        return jnp.where(live, be[b], be[last]), jnp.where(live, f, nf - 1)

    def w13_map(b, f, be, tok, dst, nb):
        e, ff = widx(b, f, be, tok, dst, nb)
        return e, 0, ff

    def w2_map(b, f, be, tok, dst, nb):
        e, ff = widx(b, f, be, tok, dst, nb)
        return e, ff, 0

    grid_spec = pltpu.PrefetchScalarGridSpec(
        num_scalar_prefetch=4,
        grid=(n_grid, nf),
        in_specs=[pl.BlockSpec(memory_space=pl.ANY),
                  pl.BlockSpec((1, d), lambda b, f, *_: (0, 0)),
                  pl.BlockSpec((tm, 1), lambda b, f, *_: (b, 0)),
                  pl.BlockSpec((None, d, tf), w13_map),
                  pl.BlockSpec((None, d, tf), w13_map),
                  pl.BlockSpec((None, tf, d), w2_map)],
        out_specs=pl.BlockSpec(memory_space=pl.ANY),
        scratch_shapes=[pltpu.VMEM((2, tm, d), F32), pltpu.VMEM((tm, d), BF16),
                        pltpu.VMEM((2, tm, d), F32),
                        pltpu.SemaphoreType.DMA((2,)), pltpu.SemaphoreType.DMA((2,))],
    )
    n_y = (n_grid + 1) * tm
    return pl.pallas_call(
        functools.partial(_experts_body, tm=tm, nf=nf),
        grid_spec=grid_spec,
        out_shape=jax.ShapeDtypeStruct((n_y, d), F32),
        compiler_params=_cparams(("arbitrary", "arbitrary")),
    )(block_e, row_tok, row_dst, n_used, x, gain.reshape(1, d), row_gate, w1, w3, w2)


def _route_plan(idx, gates, n, tm):
    n_assign = n * TOP_K
    flat_e = idx.T.reshape(n_assign)
    onehot = (flat_e[:, None] == jnp.arange(N_EXPERTS, dtype=jnp.int32)[None, :]).astype(jnp.int32)
    csum = jnp.cumsum(onehot, axis=0)
    counts = csum[-1]
    rank = jnp.sum((csum - onehot) * onehot, axis=1)
    padded = (counts + tm - 1) // tm * tm
    pad_end = jnp.cumsum(padded)
    pad_start = pad_end - padded
    dest = pad_start[flat_e] + rank
    n_blocks = -(-n_assign // tm) + N_EXPERTS
    n_grid = n_blocks + 1
    n_rows = n_grid * tm
    row_a = jnp.full((n_rows + tm,), -1, jnp.int32).at[tm + dest].set(
        jnp.arange(n_assign, dtype=jnp.int32))
    is_pad = row_a < 0
    a = jnp.maximum(row_a, 0)
    row_tok = (a // TOP_K)[tm:]
    row_gate = jnp.where(is_pad, 0.0, gates.T.reshape(n_assign)[a])[tm:]
    row_dst = jnp.where(is_pad, n_assign + jnp.cumsum(is_pad.astype(jnp.int32)) - 1,
                        (a % TOP_K) * n + a // TOP_K)
    block_e = jnp.minimum(
        jnp.searchsorted(pad_end, jnp.arange(n_grid, dtype=jnp.int32) * tm, side='right'),
        N_EXPERTS - 1).astype(jnp.int32)
    n_used = (pad_end[-1] // tm).astype(jnp.int32).reshape(1)
    return block_e, row_tok, row_dst, row_gate.reshape(n_rows, 1), n_used, n_grid


def _ple_body(*refs, final, combine):
    if combine:
        h_ref, y0_ref, y1_ref, p_ref, wp_ref, pn_ref, wg_ref, nf_ref, o_ref = refs
        h = h_ref[...] + (y0_ref[...] + y1_ref[...])
    else:
        h_ref, p_ref, wp_ref, pn_ref, wg_ref, nf_ref, o_ref = refs
        h = h_ref[...]
    e = _rms(_mm(p_ref[...], wp_ref[...]), pn_ref[...])
    out = h + _sigmoid(_mm(h, wg_ref[...])) * e
    if final:
        out = _rms(out, nf_ref[...])
    o_ref[...] = out


def _ple(h, p, w_proj, p_norm, w_gate, norm_final, final, y=None):
    n, d = h.shape
    dp = p.shape[1]
    tm = min(ROW_TM, n)
    row = lambda i: (i, 0)
    fixed = lambda i: (0, 0)
    specs = [pl.BlockSpec((tm, d), row)]
    args = [h]
    if y is not None:
        specs += [pl.BlockSpec((tm, d), row), pl.BlockSpec((tm, d), lambda i: (n // tm + i, 0))]
        args += [y, y]
    specs += [pl.BlockSpec((tm, dp), row), pl.BlockSpec((dp, d), fixed), pl.BlockSpec((1, d), fixed),
              pl.BlockSpec((d, d), fixed), pl.BlockSpec((1, d), fixed)]
    args += [p, w_proj, p_norm.reshape(1, d), w_gate, norm_final.reshape(1, d)]
    return pl.pallas_call(
        functools.partial(_ple_body, final=final, combine=y is not None),
        grid=(n // tm,),
        in_specs=specs,
        out_specs=pl.BlockSpec((tm, d), row),
        out_shape=jax.ShapeDtypeStruct((n, d), F32),
        compiler_params=_cparams(("parallel",)),
    )(*args)


def kernel(x, p, norm_mix, w_in, conv_w, dn_a_log, dn_dt_bias, dn_norm, da_lambda_q1, da_lambda_k1, da_lambda_q2, da_lambda_k2, da_norm, w_out, norm_ffn, ffn_w1, ffn_w3, ffn_w2, moe_router, moe_w1, moe_w3, moe_w2, ple_proj, ple_norm, ple_gate, norm_final):
    batch, seq, d = x.shape
    n = batch * seq
    depth = w_in.shape[0]
    h = x.reshape(n, d)
    wq = 3 * DN_WIDTH
    for i in range(depth):
        wi = w_in[i]
        c0 = wq + DN_WIDTH
        c1 = c0 + 2 * DN_HEADS
        w_dn = wi[:, :c0].astype(BF16)
        w_ba = jnp.pad(wi[:, c0:c1], ((0, 0), (0, LANES - 2 * DN_HEADS))).astype(BF16)
        att_scale = DA_HEAD_DIM ** -0.5
        w_qk = jnp.concatenate([wi[:, c1:c1 + DA_WIDTH] * att_scale,
                                wi[:, c1 + DA_WIDTH:c1 + 2 * DA_WIDTH]], axis=1).astype(BF16)
        w_vt = wi[:, c1 + 2 * DA_WIDTH:].T.astype(BF16)
        proj_dn, ba = _dn_proj(h, norm_mix[i], w_dn, w_ba, 2 * DN_WIDTH)
        qk, v_t = _attn_proj(h, norm_mix[i], w_qk, w_vt, min(ATT_TQ, seq))

        o_dn = _deltanet(proj_dn, ba, conv_w[i], dn_dt_bias[i], dn_a_log[i], dn_norm[i], batch, seq)
        lam_init = 0.8 - 0.6 * math.exp(-0.3 * i)
        lam_vecs = jnp.stack([da_lambda_q1[i], da_lambda_k1[i], da_lambda_q2[i], da_lambda_k2[i]])
        o_da = _diffattn(qk, v_t, lam_vecs, da_norm[i], lam_init, batch, seq)
        h = _outproj(h, o_dn, o_da, w_out[i].astype(BF16))

        j = i // 2
        last = i == depth - 1
        wp = ple_proj[i].astype(BF16)
        wg = ple_gate[i].astype(BF16)
        if i % 2 == 0:
            h = _ffn(h, norm_ffn[i], ffn_w1[j].astype(BF16), ffn_w3[j].astype(BF16),
                     ffn_w2[j].astype(BF16))
            h = _ple(h, p[i].reshape(n, -1), wp, ple_norm[i], wg, norm_final, last)
        else:
            idx, gates = _router(h, norm_ffn[i], moe_router[j])
            tm = min(MOE_TM, n)
            plan = _route_plan(idx, gates, n, tm)
            y = _experts(h, norm_ffn[i], moe_w1[j].astype(BF16), moe_w3[j].astype(BF16),
                         moe_w2[j].astype(BF16), plan, tm)
            h = _ple(h, p[i].reshape(n, -1), wp, ple_norm[i], wg, norm_final, last, y=y)
    return h.reshape(batch, seq, d)
```

```python
import functools
import math

import jax
import jax.numpy as jnp
from jax import lax
from jax.experimental import pallas as pl
from jax.experimental.pallas import tpu as pltpu

F32 = jnp.float32
BF16 = jnp.bfloat16
EPS = 1e-6

CHUNK = 64
DN_HEADS = 8
DN_HEAD_DIM = 128
DN_WIDTH = DN_HEADS * DN_HEAD_DIM
CONV_K = 4
DN_CPS = 2
DA_HEADS = 8
DA_HEAD_DIM = 64
DA_V_DIM = 2 * DA_HEAD_DIM
DA_WIDTH = DA_HEADS * DA_V_DIM
N_EXPERTS = 8
TOP_K = 2
LANES = 128
SUBLANES = 8
VMEM_LIMIT = 56 * 1024 * 1024

PROJ_TM = 512
FFN_TM = 512
FFN_TF = 1024
MOE_TM = 512
MOE_TF = 1024
ATT_TQ = 256
ROW_TM = 512
HALO = SUBLANES


def _cparams(sem):
    return pltpu.CompilerParams(dimension_semantics=sem, vmem_limit_bytes=VMEM_LIMIT)


def _sigmoid(x):
    return 1.0 / (1.0 + jnp.exp(-x))


def _silu(x):
    return x * _sigmoid(x)


def _softplus(x):
    return jnp.maximum(x, 0.0) + jnp.log1p(jnp.exp(-jnp.abs(x)))


def _rms(x, gain):
    y = x * lax.rsqrt(jnp.mean(x * x, axis=-1, keepdims=True) + EPS)
    return y * gain


def _mm(a, b):
    return jnp.dot(a.astype(BF16), b.astype(BF16), preferred_element_type=F32)


def _mm_nt(a, b):
    return lax.dot_general(a.astype(BF16), b.astype(BF16), (((1,), (1,)), ((), ())),
                           preferred_element_type=F32)


def _mm_tn(a, b):
    return lax.dot_general(a.astype(BF16), b.astype(BF16), (((0,), (0,)), ((), ())),
                           preferred_element_type=F32)


def _split3(a):
    a1 = a.astype(BF16)
    r1 = a - a1.astype(F32)
    a2 = r1.astype(BF16)
    a3 = (r1 - a2.astype(F32)).astype(BF16)
    return a1, a2, a3


def _dn_proj_body(x_ref, g_ref, w_ref, wba_ref, o_ref, ba_ref, xn_ref):
    @pl.when(pl.program_id(1) == 0)
    def _():
        xn_ref[...] = _rms(x_ref[...], g_ref[...]).astype(BF16)
        ba_ref[...] = jnp.dot(xn_ref[...], wba_ref[...], preferred_element_type=F32)

    o_ref[...] = jnp.dot(xn_ref[...], w_ref[...], preferred_element_type=F32)


def _dn_proj(x, gain, w, w_ba, tn):
    n, d = x.shape
    nout = w.shape[1]
    tm = min(PROJ_TM, n)
    return pl.pallas_call(
        _dn_proj_body,
        grid=(n // tm, nout // tn),
        in_specs=[pl.BlockSpec((tm, d), lambda i, j: (i, 0)),
                  pl.BlockSpec((1, d), lambda i, j: (0, 0)),
                  pl.BlockSpec((d, tn), lambda i, j: (0, j)),
                  pl.BlockSpec((d, LANES), lambda i, j: (0, 0))],
        out_specs=[pl.BlockSpec((tm, tn), lambda i, j: (i, j)),
                   pl.BlockSpec((tm, LANES), lambda i, j: (i, 0))],
        out_shape=[jax.ShapeDtypeStruct((n, nout), F32), jax.ShapeDtypeStruct((n, LANES), F32)],
        scratch_shapes=[pltpu.VMEM((tm, d), BF16)],
        compiler_params=_cparams(("parallel", "arbitrary")),
    )(x, gain.reshape(1, d), w, w_ba)


def _unit_lower_inverse(ms, r, c):
    eye = (r == c).astype(F32)
    bd16 = (r >> 4) == (c >> 4)
    in32 = ((r >> 5) == (c >> 5)) & ((r >> 4) > (c >> 4))
    in64 = (r >> 5) > (c >> 5)
    xs = [jnp.where(bd16, -m, 0.0) for m in ms]
    ps = [eye + x for x in xs]
    xs = [_mm(x, x) for x in xs]
    for _ in range(2):
        ts = [_mm(jnp.concatenate([p, x], axis=0), x) for p, x in zip(ps, xs)]
        ps = [p + t[0:CHUNK] for p, t in zip(ps, ts)]
        xs = [t[CHUNK:2 * CHUNK] for t in ts]
    ps = [p + _mm(p, x) for p, x in zip(ps, xs)]
    for sel in (in32, in64):
        ls = [_mm(p, jnp.where(sel, m, 0.0)) for p, m in zip(ps, ms)]
        ps = [p - _mm(l, p) for p, l in zip(ps, ls)]
    return ps


def _deltanet_body(qkv_ref, gate_ref, ba_ref, convw_ref, hp_ref, dnorm_ref, o_ref, ext_ref, s_ref):
    ci = pl.program_id(1)
    rows = DN_CPS * CHUNK

    @pl.when(ci == 0)
    def _():
        ext_ref[0:HALO, :] = jnp.zeros((HALO, 3 * DN_WIDTH), F32)
        s_ref[...] = jnp.zeros_like(s_ref)

    ext_ref[HALO:HALO + rows, :] = qkv_ref[...]
    conv = convw_ref[0:1, :] * ext_ref[HALO - 3:HALO - 3 + rows, :]
    for j in range(1, CONV_K):
        conv = conv + convw_ref[j:j + 1, :] * ext_ref[HALO - 3 + j:HALO - 3 + j + rows, :]
    ext_ref[0:HALO, :] = ext_ref[rows:rows + HALO, :]
    act = _silu(conv)

    ba = ba_ref[...]
    beta_all = _sigmoid(ba)
    g_all = -jnp.exp(hp_ref[1:2, :]) * _softplus(ba + hp_ref[0:1, :])
    r = lax.broadcasted_iota(jnp.int32, (CHUNK, CHUNK), 0)
    c = lax.broadcasted_iota(jnp.int32, (CHUNK, CHUNK), 1)
    incl = r >= c
    strict = r > c
    tri = incl.astype(BF16)
    dnorm = dnorm_ref[...]
    scale = DN_HEAD_DIM ** -0.5
    heads = range(DN_HEADS)
    subs = range(DN_CPS)
    chains = [(sc, h) for sc in subs for h in heads]
    span = lambda sc: slice(sc * CHUNK, (sc + 1) * CHUNK)

    def cumulative(g):
        g1, g2, g3 = _split3(g)
        return (jnp.dot(tri, g1, preferred_element_type=F32) + jnp.dot(tri, g2, preferred_element_type=F32)
                + jnp.dot(tri, g3, preferred_element_type=F32))

    gcum = [cumulative(g_all[span(sc)]) for sc in subs]
    gcum_t = [g.T for g in gcum]

    def l2n(x):
        return x * lax.rsqrt(jnp.sum(x * x, axis=-1, keepdims=True) + EPS)

    qs, ks, vs, betas, gcs, g_lasts, decays = {}, {}, {}, {}, {}, {}, {}
    for sc, h in chains:
        lo = h * DN_HEAD_DIM
        qs[sc, h] = l2n(act[span(sc), lo:lo + DN_HEAD_DIM]) * scale
        ks[sc, h] = l2n(act[span(sc), DN_WIDTH + lo:DN_WIDTH + lo + DN_HEAD_DIM])
        vs[sc, h] = act[span(sc), 2 * DN_WIDTH + lo:2 * DN_WIDTH + lo + DN_HEAD_DIM]
        betas[sc, h] = beta_all[span(sc), h:h + 1]
        gcs[sc, h] = gcum[sc][:, DN_HEADS + h:DN_HEADS + h + 1]
        g_lasts[sc, h] = gcum[sc][CHUNK - 1:CHUNK, DN_HEADS + h:DN_HEADS + h + 1]
        decays[sc, h] = jnp.where(incl, jnp.exp(jnp.where(
            incl, gcs[sc, h] - gcum_t[sc][DN_HEADS + h:DN_HEADS + h + 1, :], 0.0)), 0.0)
    kbs = {ch: ks[ch] * betas[ch] for ch in chains}

    kq = {ch: _mm_nt(jnp.concatenate([kbs[ch], qs[ch]], axis=0), ks[ch]) for ch in chains}
    ms = [jnp.where(strict, kq[ch][0:CHUNK] * decays[ch], 0.0) for ch in chains]
    attns = {ch: jnp.where(incl, kq[ch][CHUNK:2 * CHUNK] * decays[ch], 0.0) for ch in chains}
    ts = dict(zip(chains, _unit_lower_inverse(ms, r, c)))
    uw = {ch: _mm(ts[ch], jnp.concatenate([vs[ch] * betas[ch], kbs[ch] * jnp.exp(gcs[ch])], axis=1))
          for ch in chains}
    states = [s_ref[h] for h in heads]
    for sc in subs:
        ws = [_mm(jnp.concatenate([uw[sc, h][:, DN_HEAD_DIM:], qs[sc, h] * jnp.exp(gcs[sc, h])], axis=0),
                  states[h]) for h in heads]
        v_news = [uw[sc, h][:, 0:DN_HEAD_DIM] - ws[h][0:CHUNK] for h in heads]
        intra = [_mm(attns[sc, h], v_news[h]) for h in heads]
        upd = [_mm_tn(ks[sc, h] * jnp.exp(g_lasts[sc, h] - gcs[sc, h]), v_news[h]) for h in heads]
        states = [states[h] * jnp.exp(g_lasts[sc, h]) + upd[h] for h in heads]
        for h in heads:
            lo = h * DN_HEAD_DIM
            o = ws[h][CHUNK:2 * CHUNK] + intra[h]
            gt = gate_ref[span(sc), lo:lo + DN_HEAD_DIM]
            o_ref[span(sc), lo:lo + DN_HEAD_DIM] = (_rms(o, dnorm) * _silu(gt)).astype(o_ref.dtype)
    for h in heads:
        s_ref[h] = states[h]


def _deltanet(proj, ba, conv_w, dt_bias, a_log, dn_norm, batch, seq):
    n = batch * seq
    rows = DN_CPS * CHUNK
    nc = seq // rows
    wq = 3 * DN_WIDTH
    hp = jnp.zeros((SUBLANES, LANES), F32)
    hp = hp.at[0, DN_HEADS:2 * DN_HEADS].set(dt_bias).at[1, DN_HEADS:2 * DN_HEADS].set(a_log)
    row = lambda b, ci: b * nc + ci
    return pl.pallas_call(
        _deltanet_body,
        grid=(batch, nc),
        in_specs=[pl.BlockSpec((rows, wq), lambda b, ci: (row(b, ci), 0)),
                  pl.BlockSpec((rows, DN_WIDTH), lambda b, ci: (row(b, ci), wq // DN_WIDTH)),
                  pl.BlockSpec((rows, LANES), lambda b, ci: (row(b, ci), 0)),
                  pl.BlockSpec((CONV_K, wq), lambda b, ci: (0, 0)),
                  pl.BlockSpec((SUBLANES, LANES), lambda b, ci: (0, 0)),
                  pl.BlockSpec((1, DN_HEAD_DIM), lambda b, ci: (0, 0))],
        out_specs=pl.BlockSpec((rows, DN_WIDTH), lambda b, ci: (row(b, ci), 0)),
        out_shape=jax.ShapeDtypeStruct((n, DN_WIDTH), BF16),
        scratch_shapes=[pltpu.VMEM((rows + HALO, wq), F32),
                        pltpu.VMEM((DN_HEADS, DN_HEAD_DIM, DN_HEAD_DIM), F32)],
        compiler_params=_cparams(("parallel", "arbitrary")),
    )(proj, proj, ba, conv_w, hp, dn_norm.reshape(1, DN_HEAD_DIM))


def _diffattn_body(slopes_ref, q_ref, k_ref, vt_ref, lamv_ref, danorm_ref, o_ref,
                   m_ref, l_ref, a_ref, acc_ref, nrc_ref, dbias_ref, za_ref, zb_ref, zmaxa_ref, zmaxb_ref,
                   p_ref, *, tq, lam_init):
    slope = slopes_ref[pl.program_id(1)]
    nq = q_ref.shape[0] // tq
    cols = 2 * tq

    kr = lax.broadcasted_iota(jnp.int32, (tq, cols), 0)
    qc = lax.broadcasted_iota(jnp.int32, (tq, cols), 1) & (tq - 1)
    rc = (qc - kr).astype(F32)
    nrc_ref[...] = -slope * rc
    dbias_ref[...] = jnp.where((kr >> 6) <= (qc >> 6), -slope * jnp.abs(rc), -jnp.inf)

    lv = lamv_ref[...]
    lam = (jnp.exp(jnp.sum(lv[0:1] * lv[1:2], axis=-1, keepdims=True))
           - jnp.exp(jnp.sum(lv[2:3] * lv[3:4], axis=-1, keepdims=True)) + lam_init)

    def rows(j):
        return pl.ds(pl.multiple_of(j * tq, tq), tq)

    def query_block(qi, carry):
        q = q_ref[rows(qi), :]
        lane = lax.broadcasted_iota(jnp.int32, q.shape, 1)
        zero = jnp.zeros_like(q)
        qq = jnp.concatenate([jnp.where(lane < DA_HEAD_DIM, q, zero),
                              jnp.where(lane >= DA_HEAD_DIM, q, zero)], axis=0)

        def pv_update(j):
            acc_ref[...] = a_ref[...] * acc_ref[...] + jnp.dot(vt_ref[j], p_ref[...],
                                                               preferred_element_type=F32)

        def biased(j, bias_ref):
            z = _mm_nt(k_ref[rows(j), :], qq) + bias_ref[...]
            return z, jnp.max(z, axis=0, keepdims=True)

        za_ref[...], zmaxa_ref[...] = biased(0, nrc_ref)
        z0, m0 = biased(qi, dbias_ref)
        p0 = jnp.exp(z0 - m0)
        m_ref[...] = m0
        l_ref[...] = jnp.sum(p0, axis=0, keepdims=True)
        acc_ref[...] = jnp.zeros_like(acc_ref)
        p_ref[...] = p0.astype(BF16)
        a_ref[...] = jnp.ones_like(a_ref)

        def step(kj, zin_ref, zmaxin_ref, zout_ref, zmaxout_ref):
            pv_update(jnp.where(kj == 0, qi, kj - 1))
            zout_ref[...], zmaxout_ref[...] = biased(jnp.minimum(kj + 1, qi), nrc_ref)
            shift = slope * lax.convert_element_type((qi - kj) * tq, F32)
            m_old = m_ref[...]
            m_new = jnp.maximum(m_old, zmaxin_ref[...] - shift)
            p = jnp.exp(zin_ref[...] - (m_new + shift))
            scale = jnp.exp(m_old - m_new)
            l_ref[...] = scale * l_ref[...] + jnp.sum(p, axis=0, keepdims=True)
            m_ref[...] = m_new
            p_ref[...] = p.astype(BF16)
            a_ref[...] = scale

        def pair(i, carry):
            step(2 * i, za_ref, zmaxa_ref, zb_ref, zmaxb_ref)
            step(2 * i + 1, zb_ref, zmaxb_ref, za_ref, zmaxa_ref)
            return carry

        lax.fori_loop(0, qi >> 1, pair, 0)

        @pl.when((qi & 1) == 1)
        def _():
            step(qi - 1, za_ref, zmaxa_ref, zb_ref, zmaxb_ref)

        pv_update(jnp.where(qi == 0, qi, qi - 1))

        o1 = acc_ref[:, 0:tq] / l_ref[:, 0:tq]
        o2 = acc_ref[:, tq:2 * tq] / l_ref[:, tq:2 * tq]
        o = o1 - lam * o2
        o = o * lax.rsqrt(jnp.mean(o * o, axis=0, keepdims=True) + EPS) * danorm_ref[...]
        o_ref[rows(qi), :] = (o * (1.0 - lam_init)).T.astype(o_ref.dtype)
        return carry

    lax.fori_loop(0, nq, query_block, 0)


def _diffattn(qk, v_t, lam_vecs, da_norm, lam_init, batch, seq):
    n = batch * seq
    tq = v_t.shape[2]
    nq = seq // tq
    slopes = jnp.exp2(-8.0 * jnp.arange(1, DA_HEADS + 1, dtype=F32) / DA_HEADS)
    stat = pltpu.VMEM((1, 2 * tq), F32)
    tile = pltpu.VMEM((tq, 2 * tq), F32)
    grid_spec = pltpu.PrefetchScalarGridSpec(
        num_scalar_prefetch=1,
        grid=(batch, DA_HEADS),
        in_specs=[pl.BlockSpec((seq, DA_V_DIM), lambda b, h, sl: (b, h)),
                  pl.BlockSpec((seq, DA_V_DIM), lambda b, h, sl: (b, DA_HEADS + h)),
                  pl.BlockSpec((nq, DA_V_DIM, tq), lambda b, h, sl: (b, h, 0)),
                  pl.BlockSpec((4, DA_HEAD_DIM), lambda b, h, sl: (0, 0)),
                  pl.BlockSpec((DA_V_DIM, 1), lambda b, h, sl: (0, 0))],
        out_specs=pl.BlockSpec((seq, DA_V_DIM), lambda b, h, sl: (b, h)),
        scratch_shapes=[stat, stat, stat, pltpu.VMEM((DA_V_DIM, 2 * tq), F32), tile, tile, tile, tile,
                        stat, stat, pltpu.VMEM((tq, 2 * tq), BF16)],
    )
    return pl.pallas_call(
        functools.partial(_diffattn_body, tq=tq, lam_init=lam_init),
        grid_spec=grid_spec,
        out_shape=jax.ShapeDtypeStruct((n, DA_WIDTH), BF16),
        compiler_params=_cparams(("parallel", "parallel")),
    )(slopes, qk, qk, v_t, lam_vecs, da_norm.reshape(DA_V_DIM, 1))


def _attn_proj_body(x_ref, g_ref, wqk_ref, wvt_ref, qk_ref, vt_ref, *, tq):
    xn = _rms(x_ref[...], g_ref[...]).astype(BF16)
    qk_ref[...] = jnp.dot(xn, wqk_ref[...], preferred_element_type=F32).astype(qk_ref.dtype)
    for j in range(vt_ref.shape[0]):
        vt_ref[j] = lax.dot_general(wvt_ref[...], xn[j * tq:(j + 1) * tq], (((1,), (1,)), ((), ())),
                                    preferred_element_type=F32).astype(vt_ref.dtype)


def _attn_proj(x, gain, w_qk, w_vt, tq):
    n, d = x.shape
    tm = min(ROW_TM, n)
    nb = tm // tq
    return pl.pallas_call(
        functools.partial(_attn_proj_body, tq=tq),
        grid=(n // tm,),
        in_specs=[pl.BlockSpec((tm, d), lambda i: (i, 0)),
                  pl.BlockSpec((1, d), lambda i: (0, 0)),
                  pl.BlockSpec((d, 2 * DA_WIDTH), lambda i: (0, 0)),
                  pl.BlockSpec((DA_WIDTH, d), lambda i: (0, 0))],
        out_specs=[pl.BlockSpec((tm, 2 * DA_WIDTH), lambda i: (i, 0)),
                   pl.BlockSpec((nb, DA_WIDTH, tq), lambda i: (i, 0, 0))],
        out_shape=[jax.ShapeDtypeStruct((n, 2 * DA_WIDTH), BF16),
                   jax.ShapeDtypeStruct((n // tq, DA_WIDTH, tq), BF16)],
        compiler_params=_cparams(("parallel",)),
    )(x, gain.reshape(1, d), w_qk, w_vt)


def _outproj_body(h_ref, odn_ref, oda_ref, w_ref, o_ref):
    o_ref[...] = (h_ref[...]
                  + jnp.dot(odn_ref[...], w_ref[0:DN_WIDTH, :], preferred_element_type=F32)
                  + jnp.dot(oda_ref[...], w_ref[DN_WIDTH:DN_WIDTH + DA_WIDTH, :],
                            preferred_element_type=F32))


def _outproj(h, o_dn, o_da, w_out):
    n, d = h.shape
    tm = min(ROW_TM, n)
    return pl.pallas_call(
        _outproj_body,
        grid=(n // tm,),
        in_specs=[pl.BlockSpec((tm, d), lambda i: (i, 0)),
                  pl.BlockSpec((tm, DN_WIDTH), lambda i: (i, 0)),
                  pl.BlockSpec((tm, DA_WIDTH), lambda i: (i, 0)),
                  pl.BlockSpec((DN_WIDTH + DA_WIDTH, d), lambda i: (0, 0))],
        out_specs=pl.BlockSpec((tm, d), lambda i: (i, 0)),
        out_shape=jax.ShapeDtypeStruct((n, d), F32),
        compiler_params=_cparams(("parallel",)),
    )(h, o_dn, o_da, w_out)


def _ffn_body(x_ref, g_ref, w1_ref, w3_ref, w2_ref, o_ref, xn_ref):
    @pl.when(pl.program_id(1) == 0)
    def _():
        x = x_ref[...]
        xn_ref[...] = _rms(x, g_ref[...]).astype(BF16)
        o_ref[...] = x

    xn = xn_ref[...]
    g = jnp.dot(xn, w1_ref[...], preferred_element_type=F32)
    u = jnp.dot(xn, w3_ref[...], preferred_element_type=F32)
    a = (_silu(g) * u).astype(BF16)
    o_ref[...] += jnp.dot(a, w2_ref[...], preferred_element_type=F32)


def _ffn(x, gain, w1, w3, w2):
    n, d = x.shape
    dff = w1.shape[1]
    tm = min(FFN_TM, n)
    tf = FFN_TF
    return pl.pallas_call(
        _ffn_body,
        grid=(n // tm, dff // tf),
        in_specs=[pl.BlockSpec((tm, d), lambda i, f: (i, 0)),
                  pl.BlockSpec((1, d), lambda i, f: (0, 0)),
                  pl.BlockSpec((d, tf), lambda i, f: (0, f)),
                  pl.BlockSpec((d, tf), lambda i, f: (0, f)),
                  pl.BlockSpec((tf, d), lambda i, f: (f, 0))],
        out_specs=pl.BlockSpec((tm, d), lambda i, f: (i, 0)),
        out_shape=jax.ShapeDtypeStruct((n, d), F32),
        scratch_shapes=[pltpu.VMEM((tm, d), BF16)],
        compiler_params=_cparams(("parallel", "arbitrary")),
    )(x, gain.reshape(1, d), w1, w3, w2)


def _router_body(x_ref, g_ref, wt_ref, idx_ref, gate_ref):
    xn = _rms(x_ref[...], g_ref[...])
    x1, x2, x3 = _split3(xn)
    w1, w2, w3 = _split3(wt_ref[...])
    nt = lambda a, b: lax.dot_general(a, b, (((1,), (1,)), ((), ())), preferred_element_type=F32)
    logits = (nt(w3, x1) + nt(w2, x2) + nt(w1, x3)) + (nt(w2, x1) + nt(w1, x2)) + nt(w1, x1)
    e = lax.broadcasted_iota(jnp.int32, logits.shape, 0)
    m0 = jnp.max(logits, axis=0, keepdims=True)
    i0 = jnp.min(jnp.where(logits == m0, e, N_EXPERTS), axis=0, keepdims=True)
    rest = jnp.where(e == i0, -jnp.inf, logits)
    m1 = jnp.max(rest, axis=0, keepdims=True)
    i1 = jnp.min(jnp.where(rest == m1, e, N_EXPERTS), axis=0, keepdims=True)
    t = jnp.exp(m1 - m0)
    idx_ref[0:1, :] = i0
    idx_ref[1:2, :] = i1
    gate_ref[0:1, :] = 1.0 / (1.0 + t)
    gate_ref[1:2, :] = t / (1.0 + t)


def _router(x, gain, w_router):
    n, d = x.shape
    tm = min(ROW_TM, n)
    return pl.pallas_call(
        _router_body,
        grid=(n // tm,),
        in_specs=[pl.BlockSpec((tm, d), lambda i: (i, 0)),
                  pl.BlockSpec((1, d), lambda i: (0, 0)),
                  pl.BlockSpec((N_EXPERTS, d), lambda i: (0, 0))],
        out_specs=[pl.BlockSpec((TOP_K, tm), lambda i: (0, i)),
                   pl.BlockSpec((TOP_K, tm), lambda i: (0, i))],
        out_shape=[jax.ShapeDtypeStruct((TOP_K, n), jnp.int32),
                   jax.ShapeDtypeStruct((TOP_K, n), F32)],
        compiler_params=_cparams(("parallel",)),
    )(x, gain.reshape(1, d), w_router.T)


def _experts_body(be_ref, tok_ref, dst_ref, nb_ref, x_hbm, g_ref, gate_ref, w1_ref, w3_ref, w2_ref,
                  y_hbm, xg_ref, xn_ref, acc_ref, gsem, ssem, *, tm, nf):
    b = pl.program_id(0)
    f = pl.program_id(1)
    nb = nb_ref[0]
    slot = b & 1
    other = 1 - slot
    per_step, extra = divmod(tm, nf)

    def gather(blk, row, buf):
        return pltpu.make_async_copy(x_hbm.at[pl.ds(tok_ref[blk * tm + row], 1), :],
                                     xg_ref.at[buf, pl.ds(row, 1), :], gsem.at[buf])

    def scatter(blk, row, buf):
        return pltpu.make_async_copy(acc_ref.at[buf, pl.ds(row, 1), :],
                                     y_hbm.at[pl.ds(dst_ref[(blk + 1) * tm + row], 1), :], ssem.at[buf])

    def for_rows(fn):
        def step(i, carry):
            fn(i)
            return carry
        lax.fori_loop(0, tm, step, 0)

    def wait_gather(buf):
        pltpu.make_async_copy(x_hbm.at[pl.ds(0, tm), :], xg_ref.at[buf], gsem.at[buf]).wait()

    def wait_scatter(buf):
        pltpu.make_async_copy(acc_ref.at[buf], y_hbm.at[pl.ds(0, tm), :], ssem.at[buf]).wait()

    used = b < nb

    @pl.when(jnp.logical_and(f == 0, b == 0))
    def _():
        for_rows(lambda i: gather(0, i, 0).start())
        acc_ref[1] = jnp.zeros((tm, acc_ref.shape[2]), F32)

    @pl.when(jnp.logical_and(f == 0, b <= nb))
    def _():
        wait_gather(slot)

    @pl.when(jnp.logical_and(f == 0, jnp.logical_and(b <= nb, b >= 1)))
    def _():
        wait_scatter(slot)

    @pl.when(jnp.logical_and(f == 0, used))
    def _():
        xn_ref[...] = _rms(xg_ref[slot], g_ref[...]).astype(BF16)
        acc_ref[slot] = jnp.zeros((tm, acc_ref.shape[2]), F32)

    base = f * per_step + jnp.minimum(f, extra)

    @pl.when(used)
    def _():
        for t in range(per_step):
            gather(b + 1, base + t, other).start()
            scatter(b - 1, base + t, other).start()
        xn = xn_ref[...]
        g = jnp.dot(xn, w1_ref[...], preferred_element_type=F32)
        u = jnp.dot(xn, w3_ref[...], preferred_element_type=F32)
        a = (_silu(g) * u).astype(BF16)
        acc_ref[slot] += jnp.dot(a, w2_ref[...], preferred_element_type=F32)

    @pl.when(jnp.logical_and(f < extra, used))
    def _():
        gather(b + 1, base + per_step, other).start()
        scatter(b - 1, base + per_step, other).start()

    @pl.when(jnp.logical_and(f == nf - 1, used))
    def _():
        acc_ref[slot] = acc_ref[slot] * gate_ref[...]

    @pl.when(jnp.logical_and(f == 0, b == nb))
    def _():
        for_rows(lambda i: scatter(b - 1, i, other).start())
        wait_scatter(other)

    @pl.when(jnp.logical_and(f == 0, b >= nb))
    def _():
        acc_ref[0] = jnp.zeros((tm, acc_ref.shape[2]), F32)
        for_rows(lambda i: scatter(b, i, 0).start())
        wait_scatter(0)


def _experts(x, gain, w1, w3, w2, plan, tm):
    block_e, row_tok, row_dst, row_gate, n_used, n_grid = plan
    n, d = x.shape
    dff = w1.shape[2]
    tf = MOE_TF
    nf = dff // tf

    def widx(b, f, be, tok, dst, nb):
        live = b < nb[0]
        last = jnp.maximum(nb[0] - 1, 0)
        return jnp.where(live, be[b], be[last]), jnp.where(live, f, nf - 1)

    def w13_map(b, f, be, tok, dst, nb):
        e, ff = widx(b, f, be, tok, dst, nb)
        return e, 0, ff

    def w2_map(b, f, be, tok, dst, nb):
        e, ff = widx(b, f, be, tok, dst, nb)
        return e, ff, 0

    grid_spec = pltpu.PrefetchScalarGridSpec(
        num_scalar_prefetch=4,
        grid=(n_grid, nf),
        in_specs=[pl.BlockSpec(memory_space=pl.ANY),
                  pl.BlockSpec((1, d), lambda b, f, *_: (0, 0)),
                  pl.BlockSpec((tm, 1), lambda b, f, *_: (b, 0)),
                  pl.BlockSpec((None, d, tf), w13_map),
                  pl.BlockSpec((None, d, tf), w13_map),
                  pl.BlockSpec((None, tf, d), w2_map)],
        out_specs=pl.BlockSpec(memory_space=pl.ANY),
        scratch_shapes=[pltpu.VMEM((2, tm, d), F32), pltpu.VMEM((tm, d), BF16),
                        pltpu.VMEM((2, tm, d), F32),
                        pltpu.SemaphoreType.DMA((2,)), pltpu.SemaphoreType.DMA((2,))],
    )
    n_y = (n_grid + 1) * tm
    return pl.pallas_call(
        functools.partial(_experts_body, tm=tm, nf=nf),
        grid_spec=grid_spec,
        out_shape=jax.ShapeDtypeStruct((n_y, d), F32),
        compiler_params=_cparams(("arbitrary", "arbitrary")),
    )(block_e, row_tok, row_dst, n_used, x, gain.reshape(1, d), row_gate, w1, w3, w2)


def _route_plan(idx, gates, n, tm):
    n_assign = n * TOP_K
    flat_e = idx.T.reshape(n_assign)
    onehot = (flat_e[:, None] == jnp.arange(N_EXPERTS, dtype=jnp.int32)[None, :]).astype(jnp.int32)
    csum = jnp.cumsum(onehot, axis=0)
    counts = csum[-1]
    rank = jnp.sum((csum - onehot) * onehot, axis=1)
    padded = (counts + tm - 1) // tm * tm
    pad_end = jnp.cumsum(padded)
    pad_start = pad_end - padded
    dest = pad_start[flat_e] + rank
    n_blocks = -(-n_assign // tm) + N_EXPERTS
    n_grid = n_blocks + 1
    n_rows = n_grid * tm
    row_a = jnp.full((n_rows + tm,), -1, jnp.int32).at[tm + dest].set(
        jnp.arange(n_assign, dtype=jnp.int32))
    is_pad = row_a < 0
    a = jnp.maximum(row_a, 0)
    row_tok = (a // TOP_K)[tm:]
    row_gate = jnp.where(is_pad, 0.0, gates.T.reshape(n_assign)[a])[tm:]
    row_dst = jnp.where(is_pad, n_assign + jnp.cumsum(is_pad.astype(jnp.int32)) - 1,
                        (a % TOP_K) * n + a // TOP_K)
    block_e = jnp.minimum(
        jnp.searchsorted(pad_end, jnp.arange(n_grid, dtype=jnp.int32) * tm, side='right'),
        N_EXPERTS - 1).astype(jnp.int32)
    n_used = (pad_end[-1] // tm).astype(jnp.int32).reshape(1)
    return block_e, row_tok, row_dst, row_gate.reshape(n_rows, 1), n_used, n_grid


def _ple_body(*refs, final, combine):
    if combine:
        h_ref, y0_ref, y1_ref, p_ref, wp_ref, pn_ref, wg_ref, nf_ref, o_ref = refs
        h = h_ref[...] + (y0_ref[...] + y1_ref[...])
    else:
        h_ref, p_ref, wp_ref, pn_ref, wg_ref, nf_ref, o_ref = refs
        h = h_ref[...]
    e = _rms(_mm(p_ref[...], wp_ref[...]), pn_ref[...])
    out = h + _sigmoid(_mm(h, wg_ref[...])) * e
    if final:
        out = _rms(out, nf_ref[...])
    o_ref[...] = out


def _ple(h, p, w_proj, p_norm, w_gate, norm_final, final, y=None):
    n, d = h.shape
    dp = p.shape[1]
    tm = min(ROW_TM, n)
    row = lambda i: (i, 0)
    fixed = lambda i: (0, 0)
    specs = [pl.BlockSpec((tm, d), row)]
    args = [h]
    if y is not None:
        specs += [pl.BlockSpec((tm, d), row), pl.BlockSpec((tm, d), lambda i: (n // tm + i, 0))]
        args += [y, y]
    specs += [pl.BlockSpec((tm, dp), row), pl.BlockSpec((dp, d), fixed), pl.BlockSpec((1, d), fixed),
              pl.BlockSpec((d, d), fixed), pl.BlockSpec((1, d), fixed)]
    args += [p, w_proj, p_norm.reshape(1, d), w_gate, norm_final.reshape(1, d)]
    return pl.pallas_call(
        functools.partial(_ple_body, final=final, combine=y is not None),
        grid=(n // tm,),
        in_specs=specs,
        out_specs=pl.BlockSpec((tm, d), row),
        out_shape=jax.ShapeDtypeStruct((n, d), F32),
        compiler_params=_cparams(("parallel",)),
    )(*args)


def kernel(x, p, norm_mix, w_in, conv_w, dn_a_log, dn_dt_bias, dn_norm, da_lambda_q1, da_lambda_k1, da_lambda_q2, da_lambda_k2, da_norm, w_out, norm_ffn, ffn_w1, ffn_w3, ffn_w2, moe_router, moe_w1, moe_w3, moe_w2, ple_proj, ple_norm, ple_gate, norm_final):
    batch, seq, d = x.shape
    n = batch * seq
    depth = w_in.shape[0]
    h = x.reshape(n, d)
    wq = 3 * DN_WIDTH
    for i in range(depth):
        wi = w_in[i]
        c0 = wq + DN_WIDTH
        c1 = c0 + 2 * DN_HEADS
        w_dn = wi[:, :c0].astype(BF16)
        w_ba = jnp.pad(wi[:, c0:c1], ((0, 0), (0, LANES - 2 * DN_HEADS))).astype(BF16)
        att_scale = DA_HEAD_DIM ** -0.5
        w_qk = jnp.concatenate([wi[:, c1:c1 + DA_WIDTH] * att_scale,
                                wi[:, c1 + DA_WIDTH:c1 + 2 * DA_WIDTH]], axis=1).astype(BF16)
        w_vt = wi[:, c1 + 2 * DA_WIDTH:].T.astype(BF16)
        proj_dn, ba = _dn_proj(h, norm_mix[i], w_dn, w_ba, 2 * DN_WIDTH)
        qk, v_t = _attn_proj(h, norm_mix[i], w_qk, w_vt, min(ATT_TQ, seq))

        o_dn = _deltanet(proj_dn, ba, conv_w[i], dn_dt_bias[i], dn_a_log[i], dn_norm[i], batch, seq)
        lam_init = 0.8 - 0.6 * math.exp(-0.3 * i)
        lam_vecs = jnp.stack([da_lambda_q1[i], da_lambda_k1[i], da_lambda_q2[i], da_lambda_k2[i]])
        o_da = _diffattn(qk, v_t, lam_vecs, da_norm[i], lam_init, batch, seq)
        h = _outproj(h, o_dn, o_da, w_out[i].astype(BF16))

        j = i // 2
        last = i == depth - 1
        wp = ple_proj[i].astype(BF16)
        wg = ple_gate[i].astype(BF16)
        if i % 2 == 0:
            h = _ffn(h, norm_ffn[i], ffn_w1[j].astype(BF16), ffn_w3[j].astype(BF16),
                     ffn_w2[j].astype(BF16))
            h = _ple(h, p[i].reshape(n, -1), wp, ple_norm[i], wg, norm_final, last)
        else:
            idx, gates = _router(h, norm_ffn[i], moe_router[j])
            tm = min(MOE_TM, n)
            plan = _route_plan(idx, gates, n, tm)
            y = _experts(h, norm_ffn[i], moe_w1[j].astype(BF16), moe_w3[j].astype(BF16),
                         moe_w2[j].astype(BF16), plan, tm)
            h = _ple(h, p[i].reshape(n, -1), wp, ple_norm[i], wg, norm_final, last, y=y)
    return h.reshape(batch, seq, d)
```

```python
import functools
import math

import jax
import jax.numpy as jnp
from jax import lax
from jax.experimental import pallas as pl
from jax.experimental.pallas import tpu as pltpu

F32 = jnp.float32
BF16 = jnp.bfloat16
EPS = 1e-6

CHUNK = 64
DN_HEADS = 8
DN_HEAD_DIM = 128
DN_WIDTH = DN_HEADS * DN_HEAD_DIM
CONV_K = 4
DN_CPS = 2
DA_HEADS = 8
DA_HEAD_DIM = 64
DA_V_DIM = 2 * DA_HEAD_DIM
DA_WIDTH = DA_HEADS * DA_V_DIM
N_EXPERTS = 8
TOP_K = 2
LANES = 128
SUBLANES = 8
VMEM_LIMIT = 56 * 1024 * 1024

PROJ_TM = 1024
FFN_TM = 512
FFN_TF = 1024
MOE_TM = 512
MOE_TF = 1024
ATT_TQ = 256
ROW_TM = 512
HALO = SUBLANES


def _cparams(sem):
    return pltpu.CompilerParams(dimension_semantics=sem, vmem_limit_bytes=VMEM_LIMIT)


def _sigmoid(x):
    return 1.0 / (1.0 + jnp.exp(-x))


def _silu(x):
    return x * _sigmoid(x)


def _softplus(x):
    return jnp.maximum(x, 0.0) + jnp.log1p(jnp.exp(-jnp.abs(x)))


def _rms(x, gain):
    y = x * lax.rsqrt(jnp.mean(x * x, axis=-1, keepdims=True) + EPS)
    return y * gain


def _mm(a, b):
    return jnp.dot(a.astype(BF16), b.astype(BF16), preferred_element_type=F32)


def _mm_nt(a, b):
    return lax.dot_general(a.astype(BF16), b.astype(BF16), (((1,), (1,)), ((), ())),
                           preferred_element_type=F32)


def _mm_tn(a, b):
    return lax.dot_general(a.astype(BF16), b.astype(BF16), (((0,), (0,)), ((), ())),
                           preferred_element_type=F32)


def _split3(a):
    a1 = a.astype(BF16)
    r1 = a - a1.astype(F32)
    a2 = r1.astype(BF16)
    a3 = (r1 - a2.astype(F32)).astype(BF16)
    return a1, a2, a3


def _dn_proj_body(x_ref, g_ref, w_ref, wba_ref, o_ref, ba_ref, xn_ref):
    @pl.when(pl.program_id(1) == 0)
    def _():
        xn_ref[...] = _rms(x_ref[...], g_ref[...]).astype(BF16)
        ba_ref[...] = jnp.dot(xn_ref[...], wba_ref[...], preferred_element_type=F32)

    o_ref[...] = jnp.dot(xn_ref[...], w_ref[...], preferred_element_type=F32)


def _dn_proj(x, gain, w, w_ba, tn):
    n, d = x.shape
    nout = w.shape[1]
    tm = min(PROJ_TM, n)
    return pl.pallas_call(
        _dn_proj_body,
        grid=(n // tm, nout // tn),
        in_specs=[pl.BlockSpec((tm, d), lambda i, j: (i, 0)),
                  pl.BlockSpec((1, d), lambda i, j: (0, 0)),
                  pl.BlockSpec((d, tn), lambda i, j: (0, j)),
                  pl.BlockSpec((d, LANES), lambda i, j: (0, 0))],
        out_specs=[pl.BlockSpec((tm, tn), lambda i, j: (i, j)),
                   pl.BlockSpec((tm, LANES), lambda i, j: (i, 0))],
        out_shape=[jax.ShapeDtypeStruct((n, nout), F32), jax.ShapeDtypeStruct((n, LANES), F32)],
        scratch_shapes=[pltpu.VMEM((tm, d), BF16)],
        compiler_params=_cparams(("parallel", "arbitrary")),
    )(x, gain.reshape(1, d), w, w_ba)


def _unit_lower_inverse(ms, r, c):
    eye = (r == c).astype(F32)
    bd16 = (r >> 4) == (c >> 4)
    in32 = ((r >> 5) == (c >> 5)) & ((r >> 4) > (c >> 4))
    in64 = (r >> 5) > (c >> 5)
    xs = [jnp.where(bd16, -m, 0.0) for m in ms]
    ps = [eye + x for x in xs]
    xs = [_mm(x, x) for x in xs]
    for _ in range(2):
        ts = [_mm(jnp.concatenate([p, x], axis=0), x) for p, x in zip(ps, xs)]
        ps = [p + t[0:CHUNK] for p, t in zip(ps, ts)]
        xs = [t[CHUNK:2 * CHUNK] for t in ts]
    ps = [p + _mm(p, x) for p, x in zip(ps, xs)]
    for sel in (in32, in64):
        ls = [_mm(p, jnp.where(sel, m, 0.0)) for p, m in zip(ps, ms)]
        ps = [p - _mm(l, p) for p, l in zip(ps, ls)]
    return ps


def _deltanet_body(qkv_ref, gate_ref, ba_ref, convw_ref, hp_ref, dnorm_ref, o_ref, ext_ref, s_ref):
    ci = pl.program_id(1)
    rows = DN_CPS * CHUNK

    @pl.when(ci == 0)
    def _():
        ext_ref[0:HALO, :] = jnp.zeros((HALO, 3 * DN_WIDTH), F32)
        s_ref[...] = jnp.zeros_like(s_ref)

    ext_ref[HALO:HALO + rows, :] = qkv_ref[...]
    conv = convw_ref[0:1, :] * ext_ref[HALO - 3:HALO - 3 + rows, :]
    for j in range(1, CONV_K):
        conv = conv + convw_ref[j:j + 1, :] * ext_ref[HALO - 3 + j:HALO - 3 + j + rows, :]
    ext_ref[0:HALO, :] = ext_ref[rows:rows + HALO, :]
    act = _silu(conv)

    ba = ba_ref[...]
    beta_all = _sigmoid(ba)
    g_all = -jnp.exp(hp_ref[1:2, :]) * _softplus(ba + hp_ref[0:1, :])
    r = lax.broadcasted_iota(jnp.int32, (CHUNK, CHUNK), 0)
    c = lax.broadcasted_iota(jnp.int32, (CHUNK, CHUNK), 1)
    incl = r >= c
    strict = r > c
    tri = incl.astype(BF16)
    dnorm = dnorm_ref[...]
    scale = DN_HEAD_DIM ** -0.5
    heads = range(DN_HEADS)
    subs = range(DN_CPS)
    chains = [(sc, h) for sc in subs for h in heads]
    span = lambda sc: slice(sc * CHUNK, (sc + 1) * CHUNK)

    def cumulative(g):
        g1, g2, g3 = _split3(g)
        return (jnp.dot(tri, g1, preferred_element_type=F32) + jnp.dot(tri, g2, preferred_element_type=F32)
                + jnp.dot(tri, g3, preferred_element_type=F32))

    gcum = [cumulative(g_all[span(sc)]) for sc in subs]
    gcum_t = [g.T for g in gcum]

    def l2n(x):
        return x * lax.rsqrt(jnp.sum(x * x, axis=-1, keepdims=True) + EPS)

    qs, ks, vs, betas, gcs, g_lasts, decays = {}, {}, {}, {}, {}, {}, {}
    for sc, h in chains:
        lo = h * DN_HEAD_DIM
        qs[sc, h] = l2n(act[span(sc), lo:lo + DN_HEAD_DIM]) * scale
        ks[sc, h] = l2n(act[span(sc), DN_WIDTH + lo:DN_WIDTH + lo + DN_HEAD_DIM])
        vs[sc, h] = act[span(sc), 2 * DN_WIDTH + lo:2 * DN_WIDTH + lo + DN_HEAD_DIM]
        betas[sc, h] = beta_all[span(sc), h:h + 1]
        gcs[sc, h] = gcum[sc][:, DN_HEADS + h:DN_HEADS + h + 1]
        g_lasts[sc, h] = gcum[sc][CHUNK - 1:CHUNK, DN_HEADS + h:DN_HEADS + h + 1]
        decays[sc, h] = jnp.where(incl, jnp.exp(jnp.where(
            incl, gcs[sc, h] - gcum_t[sc][DN_HEADS + h:DN_HEADS + h + 1, :], 0.0)), 0.0)
    kbs = {ch: ks[ch] * betas[ch] for ch in chains}

    kq = {ch: _mm_nt(jnp.concatenate([kbs[ch], qs[ch]], axis=0), ks[ch]) for ch in chains}
    ms = [jnp.where(strict, kq[ch][0:CHUNK] * decays[ch], 0.0) for ch in chains]
    attns = {ch: jnp.where(incl, kq[ch][CHUNK:2 * CHUNK] * decays[ch], 0.0) for ch in chains}
    ts = dict(zip(chains, _unit_lower_inverse(ms, r, c)))
    uw = {ch: _mm(ts[ch], jnp.concatenate([vs[ch] * betas[ch], kbs[ch] * jnp.exp(gcs[ch])], axis=1))
          for ch in chains}
    states = [s_ref[h] for h in heads]
    for sc in subs:
        ws = [_mm(jnp.concatenate([uw[sc, h][:, DN_HEAD_DIM:], qs[sc, h] * jnp.exp(gcs[sc, h])], axis=0),
                  states[h]) for h in heads]
        v_news = [uw[sc, h][:, 0:DN_HEAD_DIM] - ws[h][0:CHUNK] for h in heads]
        intra = [_mm(attns[sc, h], v_news[h]) for h in heads]
        upd = [_mm_tn(ks[sc, h] * jnp.exp(g_lasts[sc, h] - gcs[sc, h]), v_news[h]) for h in heads]
        states = [states[h] * jnp.exp(g_lasts[sc, h]) + upd[h] for h in heads]
        for h in heads:
            lo = h * DN_HEAD_DIM
            o = ws[h][CHUNK:2 * CHUNK] + intra[h]
            gt = gate_ref[span(sc), lo:lo + DN_HEAD_DIM]
            o_ref[span(sc), lo:lo + DN_HEAD_DIM] = (_rms(o, dnorm) * _silu(gt)).astype(o_ref.dtype)
    for h in heads:
        s_ref[h] = states[h]


def _deltanet(proj, ba, conv_w, dt_bias, a_log, dn_norm, batch, seq):
    n = batch * seq
    rows = DN_CPS * CHUNK
    nc = seq // rows
    wq = 3 * DN_WIDTH
    hp = jnp.zeros((SUBLANES, LANES), F32)
    hp = hp.at[0, DN_HEADS:2 * DN_HEADS].set(dt_bias).at[1, DN_HEADS:2 * DN_HEADS].set(a_log)
    row = lambda b, ci: b * nc + ci
    return pl.pallas_call(
        _deltanet_body,
        grid=(batch, nc),
        in_specs=[pl.BlockSpec((rows, wq), lambda b, ci: (row(b, ci), 0)),
                  pl.BlockSpec((rows, DN_WIDTH), lambda b, ci: (row(b, ci), wq // DN_WIDTH)),
                  pl.BlockSpec((rows, LANES), lambda b, ci: (row(b, ci), 0)),
                  pl.BlockSpec((CONV_K, wq), lambda b, ci: (0, 0)),
                  pl.BlockSpec((SUBLANES, LANES), lambda b, ci: (0, 0)),
                  pl.BlockSpec((1, DN_HEAD_DIM), lambda b, ci: (0, 0))],
        out_specs=pl.BlockSpec((rows, DN_WIDTH), lambda b, ci: (row(b, ci), 0)),
        out_shape=jax.ShapeDtypeStruct((n, DN_WIDTH), BF16),
        scratch_shapes=[pltpu.VMEM((rows + HALO, wq), F32),
                        pltpu.VMEM((DN_HEADS, DN_HEAD_DIM, DN_HEAD_DIM), F32)],
        compiler_params=_cparams(("parallel", "arbitrary")),
    )(proj, proj, ba, conv_w, hp, dn_norm.reshape(1, DN_HEAD_DIM))


def _diffattn_body(slopes_ref, q_ref, k_ref, vt_ref, lamv_ref, danorm_ref, o_ref,
                   m_ref, l_ref, a_ref, acc_ref, nrc_ref, dbias_ref, za_ref, zb_ref, zmaxa_ref, zmaxb_ref,
                   p_ref, *, tq, lam_init):
    slope = slopes_ref[pl.program_id(1)]
    nq = q_ref.shape[0] // tq
    cols = 2 * tq

    kr = lax.broadcasted_iota(jnp.int32, (tq, cols), 0)
    qc = lax.broadcasted_iota(jnp.int32, (tq, cols), 1) & (tq - 1)
    rc = (qc - kr).astype(F32)
    nrc_ref[...] = -slope * rc
    dbias_ref[...] = jnp.where((kr >> 6) <= (qc >> 6), -slope * jnp.abs(rc), -jnp.inf)

    lv = lamv_ref[...]
    lam = (jnp.exp(jnp.sum(lv[0:1] * lv[1:2], axis=-1, keepdims=True))
           - jnp.exp(jnp.sum(lv[2:3] * lv[3:4], axis=-1, keepdims=True)) + lam_init)

    def rows(j):
        return pl.ds(pl.multiple_of(j * tq, tq), tq)

    def query_block(qi, carry):
        q = q_ref[rows(qi), :]
        lane = lax.broadcasted_iota(jnp.int32, q.shape, 1)
        zero = jnp.zeros_like(q)
        qq = jnp.concatenate([jnp.where(lane < DA_HEAD_DIM, q, zero),
                              jnp.where(lane >= DA_HEAD_DIM, q, zero)], axis=0)

        def pv_update(j):
            acc_ref[...] = a_ref[...] * acc_ref[...] + jnp.dot(vt_ref[j], p_ref[...],
                                                               preferred_element_type=F32)

        def biased(j, bias_ref):
            z = _mm_nt(k_ref[rows(j), :], qq) + bias_ref[...]
            return z, jnp.max(z, axis=0, keepdims=True)

        za_ref[...], zmaxa_ref[...] = biased(0, nrc_ref)
        z0, m0 = biased(qi, dbias_ref)
        p0 = jnp.exp(z0 - m0)
        m_ref[...] = m0
        l_ref[...] = jnp.sum(p0, axis=0, keepdims=True)
        acc_ref[...] = jnp.zeros_like(acc_ref)
        p_ref[...] = p0.astype(BF16)
        a_ref[...] = jnp.ones_like(a_ref)

        def step(kj, zin_ref, zmaxin_ref, zout_ref, zmaxout_ref):
            pv_update(jnp.where(kj == 0, qi, kj - 1))
            zout_ref[...], zmaxout_ref[...] = biased(jnp.minimum(kj + 1, qi), nrc_ref)
            shift = slope * lax.convert_element_type((qi - kj) * tq, F32)
            m_old = m_ref[...]
            m_new = jnp.maximum(m_old, zmaxin_ref[...] - shift)
            p = jnp.exp(zin_ref[...] - (m_new + shift))
            scale = jnp.exp(m_old - m_new)
            l_ref[...] = scale * l_ref[...] + jnp.sum(p, axis=0, keepdims=True)
            m_ref[...] = m_new
            p_ref[...] = p.astype(BF16)
            a_ref[...] = scale

        def pair(i, carry):
            step(2 * i, za_ref, zmaxa_ref, zb_ref, zmaxb_ref)
            step(2 * i + 1, zb_ref, zmaxb_ref, za_ref, zmaxa_ref)
            return carry

        lax.fori_loop(0, qi >> 1, pair, 0)

        @pl.when((qi & 1) == 1)
        def _():
            step(qi - 1, za_ref, zmaxa_ref, zb_ref, zmaxb_ref)

        pv_update(jnp.where(qi == 0, qi, qi - 1))

        o1 = acc_ref[:, 0:tq] / l_ref[:, 0:tq]
        o2 = acc_ref[:, tq:2 * tq] / l_ref[:, tq:2 * tq]
        o = o1 - lam * o2
        o = o * lax.rsqrt(jnp.mean(o * o, axis=0, keepdims=True) + EPS) * danorm_ref[...]
        o_ref[rows(qi), :] = (o * (1.0 - lam_init)).T.astype(o_ref.dtype)
        return carry

    lax.fori_loop(0, nq, query_block, 0)


def _diffattn(qk, v_t, lam_vecs, da_norm, lam_init, batch, seq):
    n = batch * seq
    tq = v_t.shape[2]
    nq = seq // tq
    slopes = jnp.exp2(-8.0 * jnp.arange(1, DA_HEADS + 1, dtype=F32) / DA_HEADS)
    stat = pltpu.VMEM((1, 2 * tq), F32)
    tile = pltpu.VMEM((tq, 2 * tq), F32)
    grid_spec = pltpu.PrefetchScalarGridSpec(
        num_scalar_prefetch=1,
        grid=(batch, DA_HEADS),
        in_specs=[pl.BlockSpec((seq, DA_V_DIM), lambda b, h, sl: (b, h)),
                  pl.BlockSpec((seq, DA_V_DIM), lambda b, h, sl: (b, DA_HEADS + h)),
                  pl.BlockSpec((nq, DA_V_DIM, tq), lambda b, h, sl: (b, h, 0)),
                  pl.BlockSpec((4, DA_HEAD_DIM), lambda b, h, sl: (0, 0)),
                  pl.BlockSpec((DA_V_DIM, 1), lambda b, h, sl: (0, 0))],
        out_specs=pl.BlockSpec((seq, DA_V_DIM), lambda b, h, sl: (b, h)),
        scratch_shapes=[stat, stat, stat, pltpu.VMEM((DA_V_DIM, 2 * tq), F32), tile, tile, tile, tile,
                        stat, stat, pltpu.VMEM((tq, 2 * tq), BF16)],
    )
    return pl.pallas_call(
        functools.partial(_diffattn_body, tq=tq, lam_init=lam_init),
        grid_spec=grid_spec,
        out_shape=jax.ShapeDtypeStruct((n, DA_WIDTH), BF16),
        compiler_params=_cparams(("parallel", "parallel")),
    )(slopes, qk, qk, v_t, lam_vecs, da_norm.reshape(DA_V_DIM, 1))


def _attn_proj_body(x_ref, g_ref, wqk_ref, wvt_ref, qk_ref, vt_ref, *, tq):
    xn = _rms(x_ref[...], g_ref[...]).astype(BF16)
    qk_ref[...] = jnp.dot(xn, wqk_ref[...], preferred_element_type=F32).astype(qk_ref.dtype)
    for j in range(vt_ref.shape[0]):
        vt_ref[j] = lax.dot_general(wvt_ref[...], xn[j * tq:(j + 1) * tq], (((1,), (1,)), ((), ())),
                                    preferred_element_type=F32).astype(vt_ref.dtype)


def _attn_proj(x, gain, w_qk, w_vt, tq):
    n, d = x.shape
    tm = min(ROW_TM, n)
    nb = tm // tq
    return pl.pallas_call(
        functools.partial(_attn_proj_body, tq=tq),
        grid=(n // tm,),
        in_specs=[pl.BlockSpec((tm, d), lambda i: (i, 0)),
                  pl.BlockSpec((1, d), lambda i: (0, 0)),
                  pl.BlockSpec((d, 2 * DA_WIDTH), lambda i: (0, 0)),
                  pl.BlockSpec((DA_WIDTH, d), lambda i: (0, 0))],
        out_specs=[pl.BlockSpec((tm, 2 * DA_WIDTH), lambda i: (i, 0)),
                   pl.BlockSpec((nb, DA_WIDTH, tq), lambda i: (i, 0, 0))],
        out_shape=[jax.ShapeDtypeStruct((n, 2 * DA_WIDTH), BF16),
                   jax.ShapeDtypeStruct((n // tq, DA_WIDTH, tq), BF16)],
        compiler_params=_cparams(("parallel",)),
    )(x, gain.reshape(1, d), w_qk, w_vt)


def _outproj_body(h_ref, odn_ref, oda_ref, w_ref, o_ref):
    o_ref[...] = (h_ref[...]
                  + jnp.dot(odn_ref[...], w_ref[0:DN_WIDTH, :], preferred_element_type=F32)
                  + jnp.dot(oda_ref[...], w_ref[DN_WIDTH:DN_WIDTH + DA_WIDTH, :],
                            preferred_element_type=F32))


def _outproj(h, o_dn, o_da, w_out):
    n, d = h.shape
    tm = min(ROW_TM, n)
    return pl.pallas_call(
        _outproj_body,
        grid=(n // tm,),
        in_specs=[pl.BlockSpec((tm, d), lambda i: (i, 0)),
                  pl.BlockSpec((tm, DN_WIDTH), lambda i: (i, 0)),
                  pl.BlockSpec((tm, DA_WIDTH), lambda i: (i, 0)),
                  pl.BlockSpec((DN_WIDTH + DA_WIDTH, d), lambda i: (0, 0))],
        out_specs=pl.BlockSpec((tm, d), lambda i: (i, 0)),
        out_shape=jax.ShapeDtypeStruct((n, d), F32),
        compiler_params=_cparams(("parallel",)),
    )(h, o_dn, o_da, w_out)


def _ffn_body(x_ref, g_ref, w1_ref, w3_ref, w2_ref, o_ref, xn_ref):
    @pl.when(pl.program_id(1) == 0)
    def _():
        x = x_ref[...]
        xn_ref[...] = _rms(x, g_ref[...]).astype(BF16)
        o_ref[...] = x

    xn = xn_ref[...]
    g = jnp.dot(xn, w1_ref[...], preferred_element_type=F32)
    u = jnp.dot(xn, w3_ref[...], preferred_element_type=F32)
    a = (_silu(g) * u).astype(BF16)
    o_ref[...] += jnp.dot(a, w2_ref[...], preferred_element_type=F32)


def _ffn(x, gain, w1, w3, w2):
    n, d = x.shape
    dff = w1.shape[1]
    tm = min(FFN_TM, n)
    tf = FFN_TF
    return pl.pallas_call(
        _ffn_body,
        grid=(n // tm, dff // tf),
        in_specs=[pl.BlockSpec((tm, d), lambda i, f: (i, 0)),
                  pl.BlockSpec((1, d), lambda i, f: (0, 0)),
                  pl.BlockSpec((d, tf), lambda i, f: (0, f)),
                  pl.BlockSpec((d, tf), lambda i, f: (0, f)),
                  pl.BlockSpec((tf, d), lambda i, f: (f, 0))],
        out_specs=pl.BlockSpec((tm, d), lambda i, f: (i, 0)),
        out_shape=jax.ShapeDtypeStruct((n, d), F32),
        scratch_shapes=[pltpu.VMEM((tm, d), BF16)],
        compiler_params=_cparams(("parallel", "arbitrary")),
    )(x, gain.reshape(1, d), w1, w3, w2)


def _router_body(x_ref, g_ref, wt_ref, idx_ref, gate_ref):
    xn = _rms(x_ref[...], g_ref[...])
    x1, x2, x3 = _split3(xn)
    w1, w2, w3 = _split3(wt_ref[...])
    nt = lambda a, b: lax.dot_general(a, b, (((1,), (1,)), ((), ())), preferred_element_type=F32)
    logits = (nt(w3, x1) + nt(w2, x2) + nt(w1, x3)) + (nt(w2, x1) + nt(w1, x2)) + nt(w1, x1)
    e = lax.broadcasted_iota(jnp.int32, logits.shape, 0)
    m0 = jnp.max(logits, axis=0, keepdims=True)
    i0 = jnp.min(jnp.where(logits == m0, e, N_EXPERTS), axis=0, keepdims=True)
    rest = jnp.where(e == i0, -jnp.inf, logits)
    m1 = jnp.max(rest, axis=0, keepdims=True)
    i1 = jnp.min(jnp.where(rest == m1, e, N_EXPERTS), axis=0, keepdims=True)
    t = jnp.exp(m1 - m0)
    idx_ref[0:1, :] = i0
    idx_ref[1:2, :] = i1
    gate_ref[0:1, :] = 1.0 / (1.0 + t)
    gate_ref[1:2, :] = t / (1.0 + t)


def _router(x, gain, w_router):
    n, d = x.shape
    tm = min(ROW_TM, n)
    return pl.pallas_call(
        _router_body,
        grid=(n // tm,),
        in_specs=[pl.BlockSpec((tm, d), lambda i: (i, 0)),
                  pl.BlockSpec((1, d), lambda i: (0, 0)),
                  pl.BlockSpec((N_EXPERTS, d), lambda i: (0, 0))],
        out_specs=[pl.BlockSpec((TOP_K, tm), lambda i: (0, i)),
                   pl.BlockSpec((TOP_K, tm), lambda i: (0, i))],
        out_shape=[jax.ShapeDtypeStruct((TOP_K, n), jnp.int32),
                   jax.ShapeDtypeStruct((TOP_K, n), F32)],
        compiler_params=_cparams(("parallel",)),
    )(x, gain.reshape(1, d), w_router.T)


def _experts_body(be_ref, tok_ref, dst_ref, nb_ref, x_hbm, g_ref, gate_ref, w1_ref, w3_ref, w2_ref,
                  y_hbm, xg_ref, xn_ref, acc_ref, gsem, ssem, *, tm, nf):
    b = pl.program_id(0)
    f = pl.program_id(1)
    nb = nb_ref[0]
    slot = b & 1
    other = 1 - slot
    per_step, extra = divmod(tm, nf)

    def gather(blk, row, buf):
        return pltpu.make_async_copy(x_hbm.at[pl.ds(tok_ref[blk * tm + row], 1), :],
                                     xg_ref.at[buf, pl.ds(row, 1), :], gsem.at[buf])

    def scatter(blk, row, buf):
        return pltpu.make_async_copy(acc_ref.at[buf, pl.ds(row, 1), :],
                                     y_hbm.at[pl.ds(dst_ref[(blk + 1) * tm + row], 1), :], ssem.at[buf])

    def for_rows(fn):
        def step(i, carry):
            fn(i)
            return carry
        lax.fori_loop(0, tm, step, 0)

    def wait_gather(buf):
        pltpu.make_async_copy(x_hbm.at[pl.ds(0, tm), :], xg_ref.at[buf], gsem.at[buf]).wait()

    def wait_scatter(buf):
        pltpu.make_async_copy(acc_ref.at[buf], y_hbm.at[pl.ds(0, tm), :], ssem.at[buf]).wait()

    used = b < nb

    @pl.when(jnp.logical_and(f == 0, b == 0))
    def _():
        for_rows(lambda i: gather(0, i, 0).start())
        acc_ref[1] = jnp.zeros((tm, acc_ref.shape[2]), F32)

    @pl.when(jnp.logical_and(f == 0, b <= nb))
    def _():
        wait_gather(slot)

    @pl.when(jnp.logical_and(f == 0, jnp.logical_and(b <= nb, b >= 1)))
    def _():
        wait_scatter(slot)

    @pl.when(jnp.logical_and(f == 0, used))
    def _():
        xn_ref[...] = _rms(xg_ref[slot], g_ref[...]).astype(BF16)
        acc_ref[slot] = jnp.zeros((tm, acc_ref.shape[2]), F32)

    base = f * per_step + jnp.minimum(f, extra)

    @pl.when(used)
    def _():
        for t in range(per_step):
            gather(b + 1, base + t, other).start()
            scatter(b - 1, base + t, other).start()
        xn = xn_ref[...]
        g = jnp.dot(xn, w1_ref[...], preferred_element_type=F32)
        u = jnp.dot(xn, w3_ref[...], preferred_element_type=F32)
        a = (_silu(g) * u).astype(BF16)
        acc_ref[slot] += jnp.dot(a, w2_ref[...], preferred_element_type=F32)

    @pl.when(jnp.logical_and(f < extra, used))
    def _():
        gather(b + 1, base + per_step, other).start()
        scatter(b - 1, base + per_step, other).start()

    @pl.when(jnp.logical_and(f == nf - 1, used))
    def _():
        acc_ref[slot] = acc_ref[slot] * gate_ref[...]

    @pl.when(jnp.logical_and(f == 0, b == nb))
    def _():
        for_rows(lambda i: scatter(b - 1, i, other).start())
        wait_scatter(other)

    @pl.when(jnp.logical_and(f == 0, b >= nb))
    def _():
        acc_ref[0] = jnp.zeros((tm, acc_ref.shape[2]), F32)
        for_rows(lambda i: scatter(b, i, 0).start())
        wait_scatter(0)


def _experts(x, gain, w1, w3, w2, plan, tm):
    block_e, row_tok, row_dst, row_gate, n_used, n_grid = plan
    n, d = x.shape
    dff = w1.shape[2]
    tf = MOE_TF
    nf = dff // tf

    def widx(b, f, be, tok, dst, nb):
        live = b < nb[0]
        last = jnp.maximum(nb[0] - 1, 0)
        return jnp.where(live, be[b], be[last]), jnp.where(live, f, nf - 1)

    def w13_map(b, f, be, tok, dst, nb):
        e, ff = widx(b, f, be, tok, dst, nb)
        return e, 0, ff

    def w2_map(b, f, be, tok, dst, nb):
        e, ff = widx(b, f, be, tok, dst, nb)
        return e, ff, 0

    grid_spec = pltpu.PrefetchScalarGridSpec(
        num_scalar_prefetch=4,
        grid=(n_grid, nf),
        in_specs=[pl.BlockSpec(memory_space=pl.ANY),
                  pl.BlockSpec((1, d), lambda b, f, *_: (0, 0)),
                  pl.BlockSpec((tm, 1), lambda b, f, *_: (b, 0)),
                  pl.BlockSpec((None, d, tf), w13_map),
                  pl.BlockSpec((None, d, tf), w13_map),
                  pl.BlockSpec((None, tf, d), w2_map)],
        out_specs=pl.BlockSpec(memory_space=pl.ANY),
        scratch_shapes=[pltpu.VMEM((2, tm, d), F32), pltpu.VMEM((tm, d), BF16),
                        pltpu.VMEM((2, tm, d), F32),
                        pltpu.SemaphoreType.DMA((2,)), pltpu.SemaphoreType.DMA((2,))],
    )
    n_y = (n_grid + 1) * tm
    return pl.pallas_call(
        functools.partial(_experts_body, tm=tm, nf=nf),
        grid_spec=grid_spec,
        out_shape=jax.ShapeDtypeStruct((n_y, d), F32),
        compiler_params=_cparams(("arbitrary", "arbitrary")),
    )(block_e, row_tok, row_dst, n_used, x, gain.reshape(1, d), row_gate, w1, w3, w2)


def _route_plan(idx, gates, n, tm):
    n_assign = n * TOP_K
    flat_e = idx.T.reshape(n_assign)
    onehot = (flat_e[:, None] == jnp.arange(N_EXPERTS, dtype=jnp.int32)[None, :]).astype(jnp.int32)
    csum = jnp.cumsum(onehot, axis=0)
    counts = csum[-1]
    rank = jnp.sum((csum - onehot) * onehot, axis=1)
    padded = (counts + tm - 1) // tm * tm
    pad_end = jnp.cumsum(padded)
    pad_start = pad_end - padded
    dest = pad_start[flat_e] + rank
    n_blocks = -(-n_assign // tm) + N_EXPERTS
    n_grid = n_blocks + 1
    n_rows = n_grid * tm
    row_a = jnp.full((n_rows + tm,), -1, jnp.int32).at[tm + dest].set(
        jnp.arange(n_assign, dtype=jnp.int32))
    is_pad = row_a < 0
    a = jnp.maximum(row_a, 0)
    row_tok = (a // TOP_K)[tm:]
    row_gate = jnp.where(is_pad, 0.0, gates.T.reshape(n_assign)[a])[tm:]
    row_dst = jnp.where(is_pad, n_assign + jnp.cumsum(is_pad.astype(jnp.int32)) - 1,
                        (a % TOP_K) * n + a // TOP_K)
    block_e = jnp.minimum(
        jnp.searchsorted(pad_end, jnp.arange(n_grid, dtype=jnp.int32) * tm, side='right'),
        N_EXPERTS - 1).astype(jnp.int32)
    n_used = (pad_end[-1] // tm).astype(jnp.int32).reshape(1)
    return block_e, row_tok, row_dst, row_gate.reshape(n_rows, 1), n_used, n_grid


def _ple_body(*refs, final, combine):
    if combine:
        h_ref, y0_ref, y1_ref, p_ref, wp_ref, pn_ref, wg_ref, nf_ref, o_ref = refs
        h = h_ref[...] + (y0_ref[...] + y1_ref[...])
    else:
        h_ref, p_ref, wp_ref, pn_ref, wg_ref, nf_ref, o_ref = refs
        h = h_ref[...]
    e = _rms(_mm(p_ref[...], wp_ref[...]), pn_ref[...])
    out = h + _sigmoid(_mm(h, wg_ref[...])) * e
    if final:
        out = _rms(out, nf_ref[...])
    o_ref[...] = out


def _ple(h, p, w_proj, p_norm, w_gate, norm_final, final, y=None):
    n, d = h.shape
    dp = p.shape[1]
    tm = min(ROW_TM, n)
    row = lambda i: (i, 0)
    fixed = lambda i: (0, 0)
    specs = [pl.BlockSpec((tm, d), row)]
    args = [h]
    if y is not None:
        specs += [pl.BlockSpec((tm, d), row), pl.BlockSpec((tm, d), lambda i: (n // tm + i, 0))]
        args += [y, y]
    specs += [pl.BlockSpec((tm, dp), row), pl.BlockSpec((dp, d), fixed), pl.BlockSpec((1, d), fixed),
              pl.BlockSpec((d, d), fixed), pl.BlockSpec((1, d), fixed)]
    args += [p, w_proj, p_norm.reshape(1, d), w_gate, norm_final.reshape(1, d)]
    return pl.pallas_call(
        functools.partial(_ple_body, final=final, combine=y is not None),
        grid=(n // tm,),
        in_specs=specs,
        out_specs=pl.BlockSpec((tm, d), row),
        out_shape=jax.ShapeDtypeStruct((n, d), F32),
        compiler_params=_cparams(("parallel",)),
    )(*args)


def kernel(x, p, norm_mix, w_in, conv_w, dn_a_log, dn_dt_bias, dn_norm, da_lambda_q1, da_lambda_k1, da_lambda_q2, da_lambda_k2, da_norm, w_out, norm_ffn, ffn_w1, ffn_w3, ffn_w2, moe_router, moe_w1, moe_w3, moe_w2, ple_proj, ple_norm, ple_gate, norm_final):
    batch, seq, d = x.shape
    n = batch * seq
    depth = w_in.shape[0]
    h = x.reshape(n, d)
    wq = 3 * DN_WIDTH
    for i in range(depth):
        wi = w_in[i]
        c0 = wq + DN_WIDTH
        c1 = c0 + 2 * DN_HEADS
        w_dn = wi[:, :c0].astype(BF16)
        w_ba = jnp.pad(wi[:, c0:c1], ((0, 0), (0, LANES - 2 * DN_HEADS))).astype(BF16)
        att_scale = DA_HEAD_DIM ** -0.5
        w_qk = jnp.concatenate([wi[:, c1:c1 + DA_WIDTH] * att_scale,
                                wi[:, c1 + DA_WIDTH:c1 + 2 * DA_WIDTH]], axis=1).astype(BF16)
        w_vt = wi[:, c1 + 2 * DA_WIDTH:].T.astype(BF16)
        proj_dn, ba = _dn_proj(h, norm_mix[i], w_dn, w_ba, DN_WIDTH)
        qk, v_t = _attn_proj(h, norm_mix[i], w_qk, w_vt, min(ATT_TQ, seq))

        o_dn = _deltanet(proj_dn, ba, conv_w[i], dn_dt_bias[i], dn_a_log[i], dn_norm[i], batch, seq)
        lam_init = 0.8 - 0.6 * math.exp(-0.3 * i)
        lam_vecs = jnp.stack([da_lambda_q1[i], da_lambda_k1[i], da_lambda_q2[i], da_lambda_k2[i]])
        o_da = _diffattn(qk, v_t, lam_vecs, da_norm[i], lam_init, batch, seq)
        h = _outproj(h, o_dn, o_da, w_out[i].astype(BF16))

        j = i // 2
        last = i == depth - 1
        wp = ple_proj[i].astype(BF16)
        wg = ple_gate[i].astype(BF16)
        if i % 2 == 0:
            h = _ffn(h, norm_ffn[i], ffn_w1[j].astype(BF16), ffn_w3[j].astype(BF16),
                     ffn_w2[j].astype(BF16))
            h = _ple(h, p[i].reshape(n, -1), wp, ple_norm[i], wg, norm_final, last)
        else:
            idx, gates = _router(h, norm_ffn[i], moe_router[j])
            tm = min(MOE_TM, n)
            plan = _route_plan(idx, gates, n, tm)
            y = _experts(h, norm_ffn[i], moe_w1[j].astype(BF16), moe_w3[j].astype(BF16),
                         moe_w2[j].astype(BF16), plan, tm)
            h = _ple(h, p[i].reshape(n, -1), wp, ple_norm[i], wg, norm_final, last, y=y)
    return h.reshape(batch, seq, d)
```

```python
import functools
import math

import jax
import jax.numpy as jnp
from jax import lax
from jax.experimental import pallas as pl
from jax.experimental.pallas import tpu as pltpu

F32 = jnp.float32
BF16 = jnp.bfloat16
EPS = 1e-6

CHUNK = 64
DN_HEADS = 8
DN_HEAD_DIM = 128
DN_WIDTH = DN_HEADS * DN_HEAD_DIM
CONV_K = 4
DN_CPS = 2
DA_HEADS = 8
DA_HEAD_DIM = 64
DA_V_DIM = 2 * DA_HEAD_DIM
DA_WIDTH = DA_HEADS * DA_V_DIM
N_EXPERTS = 8
TOP_K = 2
LANES = 128
SUBLANES = 8
VMEM_LIMIT = 56 * 1024 * 1024

PROJ_TM = 1024
FFN_TM = 512
FFN_TF = 1024
MOE_TM = 512
MOE_TF = 1024
ATT_TQ = 256
ROW_TM = 512
HALO = SUBLANES


def _cparams(sem):
    return pltpu.CompilerParams(dimension_semantics=sem, vmem_limit_bytes=VMEM_LIMIT)


def _sigmoid(x):
    return 1.0 / (1.0 + jnp.exp(-x))


def _silu(x):
    return x * _sigmoid(x)


def _softplus(x):
    return jnp.maximum(x, 0.0) + jnp.log1p(jnp.exp(-jnp.abs(x)))


def _rms(x, gain):
    y = x * lax.rsqrt(jnp.mean(x * x, axis=-1, keepdims=True) + EPS)
    return y * gain


def _mm(a, b):
    return jnp.dot(a.astype(BF16), b.astype(BF16), preferred_element_type=F32)


def _mm_nt(a, b):
    return lax.dot_general(a.astype(BF16), b.astype(BF16), (((1,), (1,)), ((), ())),
                           preferred_element_type=F32)


def _mm_tn(a, b):
    return lax.dot_general(a.astype(BF16), b.astype(BF16), (((0,), (0,)), ((), ())),
                           preferred_element_type=F32)


def _split3(a):
    a1 = a.astype(BF16)
    r1 = a - a1.astype(F32)
    a2 = r1.astype(BF16)
    a3 = (r1 - a2.astype(F32)).astype(BF16)
    return a1, a2, a3


def _dn_proj_body(x_ref, g_ref, w_ref, wba_ref, o_ref, ba_ref, xn_ref):
    @pl.when(pl.program_id(1) == 0)
    def _():
        xn_ref[...] = _rms(x_ref[...], g_ref[...]).astype(BF16)
        ba_ref[...] = jnp.dot(xn_ref[...], wba_ref[...], preferred_element_type=F32)

    o_ref[...] = jnp.dot(xn_ref[...], w_ref[...], preferred_element_type=F32)


def _dn_proj(x, gain, w, w_ba, tn):
    n, d = x.shape
    nout = w.shape[1]
    tm = min(PROJ_TM, n)
    return pl.pallas_call(
        _dn_proj_body,
        grid=(n // tm, nout // tn),
        in_specs=[pl.BlockSpec((tm, d), lambda i, j: (i, 0)),
                  pl.BlockSpec((1, d), lambda i, j: (0, 0)),
                  pl.BlockSpec((d, tn), lambda i, j: (0, j)),
                  pl.BlockSpec((d, LANES), lambda i, j: (0, 0))],
        out_specs=[pl.BlockSpec((tm, tn), lambda i, j: (i, j)),
                   pl.BlockSpec((tm, LANES), lambda i, j: (i, 0))],
        out_shape=[jax.ShapeDtypeStruct((n, nout), F32), jax.ShapeDtypeStruct((n, LANES), F32)],
        scratch_shapes=[pltpu.VMEM((tm, d), BF16)],
        compiler_params=_cparams(("parallel", "arbitrary")),
    )(x, gain.reshape(1, d), w, w_ba)


def _unit_lower_inverse(ms, r, c):
    eye = (r == c).astype(F32)
    bd16 = (r >> 4) == (c >> 4)
    in32 = ((r >> 5) == (c >> 5)) & ((r >> 4) > (c >> 4))
    in64 = (r >> 5) > (c >> 5)
    xs = [jnp.where(bd16, -m, 0.0) for m in ms]
    ps = [eye + x for x in xs]
    xs = [_mm(x, x) for x in xs]
    for _ in range(2):
        ts = [_mm(jnp.concatenate([p, x], axis=0), x) for p, x in zip(ps, xs)]
        ps = [p + t[0:CHUNK] for p, t in zip(ps, ts)]
        xs = [t[CHUNK:2 * CHUNK] for t in ts]
    ps = [p + _mm(p, x) for p, x in zip(ps, xs)]
    for sel in (in32, in64):
        ls = [_mm(p, jnp.where(sel, m, 0.0)) for p, m in zip(ps, ms)]
        ps = [p - _mm(l, p) for p, l in zip(ps, ls)]
    return ps


def _deltanet_body(qkv_ref, gate_ref, ba_ref, convw_ref, hp_ref, dnorm_ref, o_ref, ext_ref, s_ref):
    ci = pl.program_id(1)
    rows = DN_CPS * CHUNK

    @pl.when(ci == 0)
    def _():
        ext_ref[0:HALO, :] = jnp.zeros((HALO, 3 * DN_WIDTH), F32)
        s_ref[...] = jnp.zeros_like(s_ref)

    ext_ref[HALO:HALO + rows, :] = qkv_ref[...]
    conv = convw_ref[0:1, :] * ext_ref[HALO - 3:HALO - 3 + rows, :]
    for j in range(1, CONV_K):
        conv = conv + convw_ref[j:j + 1, :] * ext_ref[HALO - 3 + j:HALO - 3 + j + rows, :]
    ext_ref[0:HALO, :] = ext_ref[rows:rows + HALO, :]
    act = _silu(conv)

    ba = ba_ref[...]
    beta_all = _sigmoid(ba)
    g_all = -jnp.exp(hp_ref[1:2, :]) * _softplus(ba + hp_ref[0:1, :])
    r = lax.broadcasted_iota(jnp.int32, (CHUNK, CHUNK), 0)
    c = lax.broadcasted_iota(jnp.int32, (CHUNK, CHUNK), 1)
    incl = r >= c
    strict = r > c
    tri = incl.astype(BF16)
    dnorm = dnorm_ref[...]
    scale = DN_HEAD_DIM ** -0.5
    heads = range(DN_HEADS)
    subs = range(DN_CPS)
    chains = [(sc, h) for sc in subs for h in heads]
    span = lambda sc: slice(sc * CHUNK, (sc + 1) * CHUNK)

    def cumulative(g):
        g1, g2, g3 = _split3(g)
        return (jnp.dot(tri, g1, preferred_element_type=F32) + jnp.dot(tri, g2, preferred_element_type=F32)
                + jnp.dot(tri, g3, preferred_element_type=F32))

    gcum = [cumulative(g_all[span(sc)]) for sc in subs]
    gcum_t = [g.T for g in gcum]

    def l2n(x):
        return x * lax.rsqrt(jnp.sum(x * x, axis=-1, keepdims=True) + EPS)

    qs, ks, vs, betas, gcs, g_lasts, decays = {}, {}, {}, {}, {}, {}, {}
    for sc, h in chains:
        lo = h * DN_HEAD_DIM
        qs[sc, h] = l2n(act[span(sc), lo:lo + DN_HEAD_DIM]) * scale
        ks[sc, h] = l2n(act[span(sc), DN_WIDTH + lo:DN_WIDTH + lo + DN_HEAD_DIM])
        vs[sc, h] = act[span(sc), 2 * DN_WIDTH + lo:2 * DN_WIDTH + lo + DN_HEAD_DIM]
        betas[sc, h] = beta_all[span(sc), h:h + 1]
        gcs[sc, h] = gcum[sc][:, DN_HEADS + h:DN_HEADS + h + 1]
        g_lasts[sc, h] = gcum[sc][CHUNK - 1:CHUNK, DN_HEADS + h:DN_HEADS + h + 1]
        decays[sc, h] = jnp.where(incl, jnp.exp(jnp.where(
            incl, gcs[sc, h] - gcum_t[sc][DN_HEADS + h:DN_HEADS + h + 1, :], 0.0)), 0.0)
    kbs = {ch: ks[ch] * betas[ch] for ch in chains}

    kq = {ch: _mm_nt(jnp.concatenate([kbs[ch], qs[ch]], axis=0), ks[ch]) for ch in chains}
    ms = [jnp.where(strict, kq[ch][0:CHUNK] * decays[ch], 0.0) for ch in chains]
    attns = {ch: jnp.where(incl, kq[ch][CHUNK:2 * CHUNK] * decays[ch], 0.0) for ch in chains}
    ts = dict(zip(chains, _unit_lower_inverse(ms, r, c)))
    uw = {ch: _mm(ts[ch], jnp.concatenate([vs[ch] * betas[ch], kbs[ch] * jnp.exp(gcs[ch])], axis=1))
          for ch in chains}
    states = [s_ref[h] for h in heads]
    for sc in subs:
        ws = [_mm(jnp.concatenate([uw[sc, h][:, DN_HEAD_DIM:], qs[sc, h] * jnp.exp(gcs[sc, h])], axis=0),
                  states[h]) for h in heads]
        v_news = [uw[sc, h][:, 0:DN_HEAD_DIM] - ws[h][0:CHUNK] for h in heads]
        intra = [_mm(attns[sc, h], v_news[h]) for h in heads]
        upd = [_mm_tn(ks[sc, h] * jnp.exp(g_lasts[sc, h] - gcs[sc, h]), v_news[h]) for h in heads]
        states = [states[h] * jnp.exp(g_lasts[sc, h]) + upd[h] for h in heads]
        for h in heads:
            lo = h * DN_HEAD_DIM
            o = ws[h][CHUNK:2 * CHUNK] + intra[h]
            gt = gate_ref[span(sc), lo:lo + DN_HEAD_DIM]
            o_ref[span(sc), lo:lo + DN_HEAD_DIM] = (_rms(o, dnorm) * _silu(gt)).astype(o_ref.dtype)
    for h in heads:
        s_ref[h] = states[h]


def _deltanet(proj, ba, conv_w, dt_bias, a_log, dn_norm, batch, seq):
    n = batch * seq
    rows = DN_CPS * CHUNK
    nc = seq // rows
    wq = 3 * DN_WIDTH
    hp = jnp.zeros((SUBLANES, LANES), F32)
    hp = hp.at[0, DN_HEADS:2 * DN_HEADS].set(dt_bias).at[1, DN_HEADS:2 * DN_HEADS].set(a_log)
    row = lambda b, ci: b * nc + ci
    return pl.pallas_call(
        _deltanet_body,
        grid=(batch, nc),
        in_specs=[pl.BlockSpec((rows, wq), lambda b, ci: (row(b, ci), 0)),
                  pl.BlockSpec((rows, DN_WIDTH), lambda b, ci: (row(b, ci), wq // DN_WIDTH)),
                  pl.BlockSpec((rows, LANES), lambda b, ci: (row(b, ci), 0)),
                  pl.BlockSpec((CONV_K, wq), lambda b, ci: (0, 0)),
                  pl.BlockSpec((SUBLANES, LANES), lambda b, ci: (0, 0)),
                  pl.BlockSpec((1, DN_HEAD_DIM), lambda b, ci: (0, 0))],
        out_specs=pl.BlockSpec((rows, DN_WIDTH), lambda b, ci: (row(b, ci), 0)),
        out_shape=jax.ShapeDtypeStruct((n, DN_WIDTH), BF16),
        scratch_shapes=[pltpu.VMEM((rows + HALO, wq), F32),
                        pltpu.VMEM((DN_HEADS, DN_HEAD_DIM, DN_HEAD_DIM), F32)],
        compiler_params=_cparams(("parallel", "arbitrary")),
    )(proj, proj, ba, conv_w, hp, dn_norm.reshape(1, DN_HEAD_DIM))


def _diffattn_body(slopes_ref, q_ref, k_ref, vt_ref, lamv_ref, danorm_ref, o_ref,
                   m_ref, l_ref, a_ref, acc_ref, bias_ref, za_ref, zb_ref, zmaxa_ref, zmaxb_ref,
                   p_ref, *, tq, lam_init):
    slope = slopes_ref[pl.program_id(1)]
    nq = q_ref.shape[0] // tq
    cols = 2 * tq

    kr = lax.broadcasted_iota(jnp.int32, (tq, cols), 0)
    qc = lax.broadcasted_iota(jnp.int32, (tq, cols), 1) & (tq - 1)
    rc = (qc - kr).astype(F32)
    bias_ref[0] = -slope * rc
    bias_ref[1] = jnp.where((kr >> 6) <= (qc >> 6), -slope * jnp.abs(rc), -jnp.inf)

    lv = lamv_ref[...]
    lam = (jnp.exp(jnp.sum(lv[0:1] * lv[1:2], axis=-1, keepdims=True))
           - jnp.exp(jnp.sum(lv[2:3] * lv[3:4], axis=-1, keepdims=True)) + lam_init)

    def rows(j):
        return pl.ds(pl.multiple_of(j * tq, tq), tq)

    def query_block(qi, carry):
        q = q_ref[rows(qi), :]
        lane = lax.broadcasted_iota(jnp.int32, q.shape, 1)
        zero = jnp.zeros_like(q)
        qq = jnp.concatenate([jnp.where(lane < DA_HEAD_DIM, q, zero),
                              jnp.where(lane >= DA_HEAD_DIM, q, zero)], axis=0)

        def pv_update(j):
            acc_ref[...] = a_ref[...] * acc_ref[...] + jnp.dot(vt_ref[j], p_ref[...],
                                                               preferred_element_type=F32)

        def biased(j):
            z = _mm_nt(k_ref[rows(j), :], qq) + bias_ref[jnp.where(j == qi, 1, 0)]
            return z, jnp.max(z, axis=0, keepdims=True)

        m_ref[...] = jnp.full_like(m_ref, -jnp.inf)
        l_ref[...] = jnp.zeros_like(l_ref)
        acc_ref[...] = jnp.zeros_like(acc_ref)
        p_ref[...] = jnp.zeros_like(p_ref)
        a_ref[...] = jnp.ones_like(a_ref)
        za_ref[...], zmaxa_ref[...] = biased(0)

        def step(kj, zin_ref, zmaxin_ref, zout_ref=None, zmaxout_ref=None):
            pv_update(jnp.maximum(kj - 1, 0))
            if zout_ref is not None:
                zout_ref[...], zmaxout_ref[...] = biased(kj + 1)
            shift = slope * lax.convert_element_type((qi - kj) * tq, F32)
            m_old = m_ref[...]
            m_new = jnp.maximum(m_old, zmaxin_ref[...] - shift)
            p = jnp.exp(zin_ref[...] - (m_new + shift))
            scale = jnp.exp(m_old - m_new)
            l_ref[...] = scale * l_ref[...] + jnp.sum(p, axis=0, keepdims=True)
            m_ref[...] = m_new
            p_ref[...] = p.astype(BF16)
            a_ref[...] = scale

        def pair(i, carry):
            step(2 * i, za_ref, zmaxa_ref, zb_ref, zmaxb_ref)
            step(2 * i + 1, zb_ref, zmaxb_ref, za_ref, zmaxa_ref)
            return carry

        lax.fori_loop(0, qi >> 1, pair, 0)

        @pl.when((qi & 1) == 1)
        def _():
            step(qi - 1, za_ref, zmaxa_ref, zb_ref, zmaxb_ref)
            step(qi, zb_ref, zmaxb_ref)

        @pl.when((qi & 1) == 0)
        def _():
            step(qi, za_ref, zmaxa_ref)

        pv_update(qi)

        o1 = acc_ref[:, 0:tq] / l_ref[:, 0:tq]
        o2 = acc_ref[:, tq:2 * tq] / l_ref[:, tq:2 * tq]
        o = o1 - lam * o2
        o = o * lax.rsqrt(jnp.mean(o * o, axis=0, keepdims=True) + EPS) * danorm_ref[...]
        o_ref[rows(qi), :] = (o * (1.0 - lam_init)).T.astype(o_ref.dtype)
        return carry

    lax.fori_loop(0, nq, query_block, 0)


def _diffattn(qk, v_t, lam_vecs, da_norm, lam_init, batch, seq):
    n = batch * seq
    tq = v_t.shape[2]
    nq = seq // tq
    slopes = jnp.exp2(-8.0 * jnp.arange(1, DA_HEADS + 1, dtype=F32) / DA_HEADS)
    stat = pltpu.VMEM((1, 2 * tq), F32)
    tile = pltpu.VMEM((tq, 2 * tq), F32)
    grid_spec = pltpu.PrefetchScalarGridSpec(
        num_scalar_prefetch=1,
        grid=(batch, DA_HEADS),
        in_specs=[pl.BlockSpec((seq, DA_V_DIM), lambda b, h, sl: (b, h)),
                  pl.BlockSpec((seq, DA_V_DIM), lambda b, h, sl: (b, DA_HEADS + h)),
                  pl.BlockSpec((nq, DA_V_DIM, tq), lambda b, h, sl: (b, h, 0)),
                  pl.BlockSpec((4, DA_HEAD_DIM), lambda b, h, sl: (0, 0)),
                  pl.BlockSpec((DA_V_DIM, 1), lambda b, h, sl: (0, 0))],
        out_specs=pl.BlockSpec((seq, DA_V_DIM), lambda b, h, sl: (b, h)),
        scratch_shapes=[stat, stat, stat, pltpu.VMEM((DA_V_DIM, 2 * tq), F32),
                        pltpu.VMEM((2, tq, 2 * tq), F32), tile, tile,
                        stat, stat, pltpu.VMEM((tq, 2 * tq), BF16)],
    )
    return pl.pallas_call(
        functools.partial(_diffattn_body, tq=tq, lam_init=lam_init),
        grid_spec=grid_spec,
        out_shape=jax.ShapeDtypeStruct((n, DA_WIDTH), BF16),
        compiler_params=_cparams(("parallel", "parallel")),
    )(slopes, qk, qk, v_t, lam_vecs, da_norm.reshape(DA_V_DIM, 1))


def _attn_proj_body(x_ref, g_ref, wqk_ref, wvt_ref, qk_ref, vt_ref, *, tq):
    xn = _rms(x_ref[...], g_ref[...]).astype(BF16)
    qk_ref[...] = jnp.dot(xn, wqk_ref[...], preferred_element_type=F32).astype(qk_ref.dtype)
    for j in range(vt_ref.shape[0]):
        vt_ref[j] = lax.dot_general(wvt_ref[...], xn[j * tq:(j + 1) * tq], (((1,), (1,)), ((), ())),
                                    preferred_element_type=F32).astype(vt_ref.dtype)


def _attn_proj(x, gain, w_qk, w_vt, tq):
    n, d = x.shape
    tm = min(ROW_TM, n)
    nb = tm // tq
    return pl.pallas_call(
        functools.partial(_attn_proj_body, tq=tq),
        grid=(n // tm,),
        in_specs=[pl.BlockSpec((tm, d), lambda i: (i, 0)),
                  pl.BlockSpec((1, d), lambda i: (0, 0)),
                  pl.BlockSpec((d, 2 * DA_WIDTH), lambda i: (0, 0)),
                  pl.BlockSpec((DA_WIDTH, d), lambda i: (0, 0))],
        out_specs=[pl.BlockSpec((tm, 2 * DA_WIDTH), lambda i: (i, 0)),
                   pl.BlockSpec((nb, DA_WIDTH, tq), lambda i: (i, 0, 0))],
        out_shape=[jax.ShapeDtypeStruct((n, 2 * DA_WIDTH), BF16),
                   jax.ShapeDtypeStruct((n // tq, DA_WIDTH, tq), BF16)],
        compiler_params=_cparams(("parallel",)),
    )(x, gain.reshape(1, d), w_qk, w_vt)


def _outproj_body(h_ref, odn_ref, oda_ref, w_ref, o_ref):
    o_ref[...] = (h_ref[...]
                  + jnp.dot(odn_ref[...], w_ref[0:DN_WIDTH, :], preferred_element_type=F32)
                  + jnp.dot(oda_ref[...], w_ref[DN_WIDTH:DN_WIDTH + DA_WIDTH, :],
                            preferred_element_type=F32))


def _outproj(h, o_dn, o_da, w_out):
    n, d = h.shape
    tm = min(ROW_TM, n)
    return pl.pallas_call(
        _outproj_body,
        grid=(n // tm,),
        in_specs=[pl.BlockSpec((tm, d), lambda i: (i, 0)),
                  pl.BlockSpec((tm, DN_WIDTH), lambda i: (i, 0)),
                  pl.BlockSpec((tm, DA_WIDTH), lambda i: (i, 0)),
                  pl.BlockSpec((DN_WIDTH + DA_WIDTH, d), lambda i: (0, 0))],
        out_specs=pl.BlockSpec((tm, d), lambda i: (i, 0)),
        out_shape=jax.ShapeDtypeStruct((n, d), F32),
        compiler_params=_cparams(("parallel",)),
    )(h, o_dn, o_da, w_out)


def _ffn_body(x_ref, g_ref, w1_ref, w3_ref, w2_ref, o_ref, xn_ref):
    @pl.when(pl.program_id(1) == 0)
    def _():
        x = x_ref[...]
        xn_ref[...] = _rms(x, g_ref[...]).astype(BF16)
        o_ref[...] = x

    xn = xn_ref[...]
    g = jnp.dot(xn, w1_ref[...], preferred_element_type=F32)
    u = jnp.dot(xn, w3_ref[...], preferred_element_type=F32)
    a = (_silu(g) * u).astype(BF16)
    o_ref[...] += jnp.dot(a, w2_ref[...], preferred_element_type=F32)


def _ffn(x, gain, w1, w3, w2):
    n, d = x.shape
    dff = w1.shape[1]
    tm = min(FFN_TM, n)
    tf = FFN_TF
    return pl.pallas_call(
        _ffn_body,
        grid=(n // tm, dff // tf),
        in_specs=[pl.BlockSpec((tm, d), lambda i, f: (i, 0)),
                  pl.BlockSpec((1, d), lambda i, f: (0, 0)),
                  pl.BlockSpec((d, tf), lambda i, f: (0, f)),
                  pl.BlockSpec((d, tf), lambda i, f: (0, f)),
                  pl.BlockSpec((tf, d), lambda i, f: (f, 0))],
        out_specs=pl.BlockSpec((tm, d), lambda i, f: (i, 0)),
        out_shape=jax.ShapeDtypeStruct((n, d), F32),
        scratch_shapes=[pltpu.VMEM((tm, d), BF16)],
        compiler_params=_cparams(("parallel", "arbitrary")),
    )(x, gain.reshape(1, d), w1, w3, w2)


def _router_body(x_ref, g_ref, wt_ref, idx_ref, gate_ref):
    xn = _rms(x_ref[...], g_ref[...])
    x1, x2, x3 = _split3(xn)
    w1, w2, w3 = _split3(wt_ref[...])
    nt = lambda a, b: lax.dot_general(a, b, (((1,), (1,)), ((), ())), preferred_element_type=F32)
    logits = (nt(w3, x1) + nt(w2, x2) + nt(w1, x3)) + (nt(w2, x1) + nt(w1, x2)) + nt(w1, x1)
    e = lax.broadcasted_iota(jnp.int32, logits.shape, 0)
    m0 = jnp.max(logits, axis=0, keepdims=True)
    i0 = jnp.min(jnp.where(logits == m0, e, N_EXPERTS), axis=0, keepdims=True)
    rest = jnp.where(e == i0, -jnp.inf, logits)
    m1 = jnp.max(rest, axis=0, keepdims=True)
    i1 = jnp.min(jnp.where(rest == m1, e, N_EXPERTS), axis=0, keepdims=True)
    t = jnp.exp(m1 - m0)
    idx_ref[0:1, :] = i0
    idx_ref[1:2, :] = i1
    gate_ref[0:1, :] = 1.0 / (1.0 + t)
    gate_ref[1:2, :] = t / (1.0 + t)


def _router(x, gain, w_router):
    n, d = x.shape
    tm = min(ROW_TM, n)
    return pl.pallas_call(
        _router_body,
        grid=(n // tm,),
        in_specs=[pl.BlockSpec((tm, d), lambda i: (i, 0)),
                  pl.BlockSpec((1, d), lambda i: (0, 0)),
                  pl.BlockSpec((N_EXPERTS, d), lambda i: (0, 0))],
        out_specs=[pl.BlockSpec((TOP_K, tm), lambda i: (0, i)),
                   pl.BlockSpec((TOP_K, tm), lambda i: (0, i))],
        out_shape=[jax.ShapeDtypeStruct((TOP_K, n), jnp.int32),
                   jax.ShapeDtypeStruct((TOP_K, n), F32)],
        compiler_params=_cparams(("parallel",)),
    )(x, gain.reshape(1, d), w_router.T)


def _experts_body(be_ref, tok_ref, dst_ref, nb_ref, x_hbm, g_ref, gate_ref, w1_ref, w3_ref, w2_ref,
                  y_hbm, xg_ref, xn_ref, acc_ref, gsem, ssem, *, tm, nf):
    b = pl.program_id(0)
    f = pl.program_id(1)
    nb = nb_ref[0]
    slot = b & 1
    other = 1 - slot
    per_step, extra = divmod(tm, nf)

    def gather(blk, row, buf):
        return pltpu.make_async_copy(x_hbm.at[pl.ds(tok_ref[blk * tm + row], 1), :],
                                     xg_ref.at[buf, pl.ds(row, 1), :], gsem.at[buf])

    def scatter(blk, row, buf):
        return pltpu.make_async_copy(acc_ref.at[buf, pl.ds(row, 1), :],
                                     y_hbm.at[pl.ds(dst_ref[(blk + 1) * tm + row], 1), :], ssem.at[buf])

    def for_rows(fn):
        def step(i, carry):
            fn(i)
            return carry
        lax.fori_loop(0, tm, step, 0)

    def wait_gather(buf):
        pltpu.make_async_copy(x_hbm.at[pl.ds(0, tm), :], xg_ref.at[buf], gsem.at[buf]).wait()

    def wait_scatter(buf):
        pltpu.make_async_copy(acc_ref.at[buf], y_hbm.at[pl.ds(0, tm), :], ssem.at[buf]).wait()

    used = b < nb

    @pl.when(jnp.logical_and(f == 0, b == 0))
    def _():
        for_rows(lambda i: gather(0, i, 0).start())
        acc_ref[1] = jnp.zeros((tm, acc_ref.shape[2]), F32)

    @pl.when(jnp.logical_and(f == 0, b <= nb))
    def _():
        wait_gather(slot)

    @pl.when(jnp.logical_and(f == 0, jnp.logical_and(b <= nb, b >= 1)))
    def _():
        wait_scatter(slot)

    @pl.when(jnp.logical_and(f == 0, used))
    def _():
        xn_ref[...] = _rms(xg_ref[slot], g_ref[...]).astype(BF16)
        acc_ref[slot] = jnp.zeros((tm, acc_ref.shape[2]), F32)

    base = f * per_step + jnp.minimum(f, extra)

    @pl.when(used)
    def _():
        for t in range(per_step):
            gather(b + 1, base + t, other).start()
            scatter(b - 1, base + t, other).start()
        xn = xn_ref[...]
        g = jnp.dot(xn, w1_ref[...], preferred_element_type=F32)
        u = jnp.dot(xn, w3_ref[...], preferred_element_type=F32)
        a = (_silu(g) * u).astype(BF16)
        acc_ref[slot] += jnp.dot(a, w2_ref[...], preferred_element_type=F32)

    @pl.when(jnp.logical_and(f < extra, used))
    def _():
        gather(b + 1, base + per_step, other).start()
        scatter(b - 1, base + per_step, other).start()

    @pl.when(jnp.logical_and(f == nf - 1, used))
    def _():
        acc_ref[slot] = acc_ref[slot] * gate_ref[...]

    @pl.when(jnp.logical_and(f == 0, b == nb))
    def _():
        for_rows(lambda i: scatter(b - 1, i, other).start())
        wait_scatter(other)

    @pl.when(jnp.logical_and(f == 0, b >= nb))
    def _():
        acc_ref[0] = jnp.zeros((tm, acc_ref.shape[2]), F32)
        for_rows(lambda i: scatter(b, i, 0).start())
        wait_scatter(0)


def _experts(x, gain, w1, w3, w2, plan, tm):
    block_e, row_tok, row_dst, row_gate, n_used, n_grid = plan
    n, d = x.shape
    dff = w1.shape[2]
    tf = MOE_TF
    nf = dff // tf

    def widx(b, f, be, tok, dst, nb):
        live = b < nb[0]
        last = jnp.maximum(nb[0] - 1, 0)
        return jnp.where(live, be[b], be[last]), jnp.where(live, f, nf - 1)

    def w13_map(b, f, be, tok, dst, nb):
        e, ff = widx(b, f, be, tok, dst, nb)
        return e, 0, ff

    def w2_map(b, f, be, tok, dst, nb):
        e, ff = widx(b, f, be, tok, dst, nb)
        return e, ff, 0

    grid_spec = pltpu.PrefetchScalarGridSpec(
        num_scalar_prefetch=4,
        grid=(n_grid, nf),
        in_specs=[pl.BlockSpec(memory_space=pl.ANY),
                  pl.BlockSpec((1, d), lambda b, f, *_: (0, 0)),
                  pl.BlockSpec((tm, 1), lambda b, f, *_: (b, 0)),
                  pl.BlockSpec((None, d, tf), w13_map),
                  pl.BlockSpec((None, d, tf), w13_map),
                  pl.BlockSpec((None, tf, d), w2_map)],
        out_specs=pl.BlockSpec(memory_space=pl.ANY),
        scratch_shapes=[pltpu.VMEM((2, tm, d), F32), pltpu.VMEM((tm, d), BF16),
                        pltpu.VMEM((2, tm, d), F32),
                        pltpu.SemaphoreType.DMA((2,)), pltpu.SemaphoreType.DMA((2,))],
    )
    n_y = (n_grid + 1) * tm
    return pl.pallas_call(
        functools.partial(_experts_body, tm=tm, nf=nf),
        grid_spec=grid_spec,
        out_shape=jax.ShapeDtypeStruct((n_y, d), F32),
        compiler_params=_cparams(("arbitrary", "arbitrary")),
    )(block_e, row_tok, row_dst, n_used, x, gain.reshape(1, d), row_gate, w1, w3, w2)


def _route_plan(idx, gates, n, tm):
    n_assign = n * TOP_K
    flat_e = idx.T.reshape(n_assign)
    onehot = (flat_e[:, None] == jnp.arange(N_EXPERTS, dtype=jnp.int32)[None, :]).astype(jnp.int32)
    csum = jnp.cumsum(onehot, axis=0)
    counts = csum[-1]
    rank = jnp.sum((csum - onehot) * onehot, axis=1)
    padded = (counts + tm - 1) // tm * tm
    pad_end = jnp.cumsum(padded)
    pad_start = pad_end - padded
    dest = pad_start[flat_e] + rank
    n_blocks = -(-n_assign // tm) + N_EXPERTS
    n_grid = n_blocks + 1
    n_rows = n_grid * tm
    row_a = jnp.full((n_rows + tm,), -1, jnp.int32).at[tm + dest].set(
        jnp.arange(n_assign, dtype=jnp.int32))
    is_pad = row_a < 0
    a = jnp.maximum(row_a, 0)
    row_tok = (a // TOP_K)[tm:]
    row_gate = jnp.where(is_pad, 0.0, gates.T.reshape(n_assign)[a])[tm:]
    row_dst = jnp.where(is_pad, n_assign + jnp.cumsum(is_pad.astype(jnp.int32)) - 1,
                        (a % TOP_K) * n + a // TOP_K)
    block_e = jnp.minimum(
        jnp.searchsorted(pad_end, jnp.arange(n_grid, dtype=jnp.int32) * tm, side='right'),
        N_EXPERTS - 1).astype(jnp.int32)
    n_used = (pad_end[-1] // tm).astype(jnp.int32).reshape(1)
    return block_e, row_tok, row_dst, row_gate.reshape(n_rows, 1), n_used, n_grid


def _ple_body(*refs, final, combine):
    if combine:
        h_ref, y0_ref, y1_ref, p_ref, wp_ref, pn_ref, wg_ref, nf_ref, o_ref = refs
        h = h_ref[...] + (y0_ref[...] + y1_ref[...])
    else:
        h_ref, p_ref, wp_ref, pn_ref, wg_ref, nf_ref, o_ref = refs
        h = h_ref[...]
    e = _rms(_mm(p_ref[...], wp_ref[...]), pn_ref[...])
    out = h + _sigmoid(_mm(h, wg_ref[...])) * e
    if final:
        out = _rms(out, nf_ref[...])
    o_ref[...] = out


def _ple(h, p, w_proj, p_norm, w_gate, norm_final, final, y=None):
    n, d = h.shape
    dp = p.shape[1]
    tm = min(ROW_TM, n)
    row = lambda i: (i, 0)
    fixed = lambda i: (0, 0)
    specs = [pl.BlockSpec((tm, d), row)]
    args = [h]
    if y is not None:
        specs += [pl.BlockSpec((tm, d), row), pl.BlockSpec((tm, d), lambda i: (n // tm + i, 0))]
        args += [y, y]
    specs += [pl.BlockSpec((tm, dp), row), pl.BlockSpec((dp, d), fixed), pl.BlockSpec((1, d), fixed),
              pl.BlockSpec((d, d), fixed), pl.BlockSpec((1, d), fixed)]
    args += [p, w_proj, p_norm.reshape(1, d), w_gate, norm_final.reshape(1, d)]
    return pl.pallas_call(
        functools.partial(_ple_body, final=final, combine=y is not None),
        grid=(n // tm,),
        in_specs=specs,
        out_specs=pl.BlockSpec((tm, d), row),
        out_shape=jax.ShapeDtypeStruct((n, d), F32),
        compiler_params=_cparams(("parallel",)),
    )(*args)


def kernel(x, p, norm_mix, w_in, conv_w, dn_a_log, dn_dt_bias, dn_norm, da_lambda_q1, da_lambda_k1, da_lambda_q2, da_lambda_k2, da_norm, w_out, norm_ffn, ffn_w1, ffn_w3, ffn_w2, moe_router, moe_w1, moe_w3, moe_w2, ple_proj, ple_norm, ple_gate, norm_final):
    batch, seq, d = x.shape
    n = batch * seq
    depth = w_in.shape[0]
    h = x.reshape(n, d)
    wq = 3 * DN_WIDTH
    for i in range(depth):
        wi = w_in[i]
        c0 = wq + DN_WIDTH
        c1 = c0 + 2 * DN_HEADS
        w_dn = wi[:, :c0].astype(BF16)
        w_ba = jnp.pad(wi[:, c0:c1], ((0, 0), (0, LANES - 2 * DN_HEADS))).astype(BF16)
        att_scale = DA_HEAD_DIM ** -0.5
        w_qk = jnp.concatenate([wi[:, c1:c1 + DA_WIDTH] * att_scale,
                                wi[:, c1 + DA_WIDTH:c1 + 2 * DA_WIDTH]], axis=1).astype(BF16)
        w_vt = wi[:, c1 + 2 * DA_WIDTH:].T.astype(BF16)
        proj_dn, ba = _dn_proj(h, norm_mix[i], w_dn, w_ba, DN_WIDTH)
        qk, v_t = _attn_proj(h, norm_mix[i], w_qk, w_vt, min(ATT_TQ, seq))

        o_dn = _deltanet(proj_dn, ba, conv_w[i], dn_dt_bias[i], dn_a_log[i], dn_norm[i], batch, seq)
        lam_init = 0.8 - 0.6 * math.exp(-0.3 * i)
        lam_vecs = jnp.stack([da_lambda_q1[i], da_lambda_k1[i], da_lambda_q2[i], da_lambda_k2[i]])
        o_da = _diffattn(qk, v_t, lam_vecs, da_norm[i], lam_init, batch, seq)
        h = _outproj(h, o_dn, o_da, w_out[i].astype(BF16))

        j = i // 2
        last = i == depth - 1
        wp = ple_proj[i].astype(BF16)
        wg = ple_gate[i].astype(BF16)
        if i % 2 == 0:
            h = _ffn(h, norm_ffn[i], ffn_w1[j].astype(BF16), ffn_w3[j].astype(BF16),
                     ffn_w2[j].astype(BF16))
            h = _ple(h, p[i].reshape(n, -1), wp, ple_norm[i], wg, norm_final, last)
        else:
            idx, gates = _router(h, norm_ffn[i], moe_router[j])
            tm = min(MOE_TM, n)
            plan = _route_plan(idx, gates, n, tm)
            y = _experts(h, norm_ffn[i], moe_w1[j].astype(BF16), moe_w3[j].astype(BF16),
                         moe_w2[j].astype(BF16), plan, tm)
            h = _ple(h, p[i].reshape(n, -1), wp, ple_norm[i], wg, norm_final, last, y=y)
    return h.reshape(batch, seq, d)
```

```python
import functools
import math

import jax
import jax.numpy as jnp
from jax import lax
from jax.experimental import pallas as pl
from jax.experimental.pallas import tpu as pltpu

F32 = jnp.float32
BF16 = jnp.bfloat16
EPS = 1e-6

CHUNK = 64
DN_HEADS = 8
DN_HEAD_DIM = 128
DN_WIDTH = DN_HEADS * DN_HEAD_DIM
CONV_K = 4
DN_CPS = 2
DA_HEADS = 8
DA_HEAD_DIM = 64
DA_V_DIM = 2 * DA_HEAD_DIM
DA_WIDTH = DA_HEADS * DA_V_DIM
N_EXPERTS = 8
TOP_K = 2
LANES = 128
SUBLANES = 8
VMEM_LIMIT = 56 * 1024 * 1024

PROJ_TM = 1024
FFN_TM = 512
FFN_TF = 1024
MOE_TM = 512
MOE_TF = 1024
ATT_TQ = 256
ROW_TM = 512
HALO = SUBLANES


def _cparams(sem):
    return pltpu.CompilerParams(dimension_semantics=sem, vmem_limit_bytes=VMEM_LIMIT)


def _sigmoid(x):
    return 1.0 / (1.0 + jnp.exp(-x))


def _silu(x):
    return x * _sigmoid(x)


def _softplus(x):
    return jnp.maximum(x, 0.0) + jnp.log1p(jnp.exp(-jnp.abs(x)))


def _rms(x, gain):
    y = x * lax.rsqrt(jnp.mean(x * x, axis=-1, keepdims=True) + EPS)
    return y * gain


def _mm(a, b):
    return jnp.dot(a.astype(BF16), b.astype(BF16), preferred_element_type=F32)


def _mm_nt(a, b):
    return lax.dot_general(a.astype(BF16), b.astype(BF16), (((1,), (1,)), ((), ())),
                           preferred_element_type=F32)


def _mm_tn(a, b):
    return lax.dot_general(a.astype(BF16), b.astype(BF16), (((0,), (0,)), ((), ())),
                           preferred_element_type=F32)


def _split3(a):
    a1 = a.astype(BF16)
    r1 = a - a1.astype(F32)
    a2 = r1.astype(BF16)
    a3 = (r1 - a2.astype(F32)).astype(BF16)
    return a1, a2, a3


def _dn_proj_body(x_ref, g_ref, w_ref, wba_ref, o_ref, ba_ref, xn_ref):
    @pl.when(pl.program_id(1) == 0)
    def _():
        xn_ref[...] = _rms(x_ref[...], g_ref[...]).astype(BF16)
        ba_ref[...] = jnp.dot(xn_ref[...], wba_ref[...], preferred_element_type=F32)

    o_ref[...] = jnp.dot(xn_ref[...], w_ref[...], preferred_element_type=F32)


def _dn_proj(x, gain, w, w_ba, tn):
    n, d = x.shape
    nout = w.shape[1]
    tm = min(PROJ_TM, n)
    return pl.pallas_call(
        _dn_proj_body,
        grid=(n // tm, nout // tn),
        in_specs=[pl.BlockSpec((tm, d), lambda i, j: (i, 0)),
                  pl.BlockSpec((1, d), lambda i, j: (0, 0)),
                  pl.BlockSpec((d, tn), lambda i, j: (0, j)),
                  pl.BlockSpec((d, LANES), lambda i, j: (0, 0))],
        out_specs=[pl.BlockSpec((tm, tn), lambda i, j: (i, j)),
                   pl.BlockSpec((tm, LANES), lambda i, j: (i, 0))],
        out_shape=[jax.ShapeDtypeStruct((n, nout), F32), jax.ShapeDtypeStruct((n, LANES), F32)],
        scratch_shapes=[pltpu.VMEM((tm, d), BF16)],
        compiler_params=_cparams(("parallel", "arbitrary")),
    )(x, gain.reshape(1, d), w, w_ba)


def _unit_lower_inverse(ms, r, c):
    eye = (r == c).astype(F32)
    bd16 = (r >> 4) == (c >> 4)
    in32 = ((r >> 5) == (c >> 5)) & ((r >> 4) > (c >> 4))
    in64 = (r >> 5) > (c >> 5)
    xs = [jnp.where(bd16, -m, 0.0) for m in ms]
    ps = [eye + x for x in xs]
    xs = [_mm(x, x) for x in xs]
    for _ in range(2):
        ts = [_mm(jnp.concatenate([p, x], axis=0), x) for p, x in zip(ps, xs)]
        ps = [p + t[0:CHUNK] for p, t in zip(ps, ts)]
        xs = [t[CHUNK:2 * CHUNK] for t in ts]
    ps = [p + _mm(p, x) for p, x in zip(ps, xs)]
    for sel in (in32, in64):
        ls = [_mm(p, jnp.where(sel, m, 0.0)) for p, m in zip(ps, ms)]
        ps = [p - _mm(l, p) for p, l in zip(ps, ls)]
    return ps


def _deltanet_body(qkv_ref, gate_ref, ba_ref, convw_ref, hp_ref, dnorm_ref, o_ref, ext_ref, s_ref):
    ci = pl.program_id(1)
    rows = DN_CPS * CHUNK

    @pl.when(ci == 0)
    def _():
        ext_ref[0:HALO, :] = jnp.zeros((HALO, 3 * DN_WIDTH), F32)
        s_ref[...] = jnp.zeros_like(s_ref)

    ext_ref[HALO:HALO + rows, :] = qkv_ref[...]
    conv = convw_ref[0:1, :] * ext_ref[HALO - 3:HALO - 3 + rows, :]
    for j in range(1, CONV_K):
        conv = conv + convw_ref[j:j + 1, :] * ext_ref[HALO - 3 + j:HALO - 3 + j + rows, :]
    ext_ref[0:HALO, :] = ext_ref[rows:rows + HALO, :]
    act = _silu(conv)

    ba = ba_ref[...]
    beta_all = _sigmoid(ba)
    g_all = -jnp.exp(hp_ref[1:2, :]) * _softplus(ba + hp_ref[0:1, :])
    r = lax.broadcasted_iota(jnp.int32, (CHUNK, CHUNK), 0)
    c = lax.broadcasted_iota(jnp.int32, (CHUNK, CHUNK), 1)
    incl = r >= c
    strict = r > c
    tri = incl.astype(BF16)
    dnorm = dnorm_ref[...]
    scale = DN_HEAD_DIM ** -0.5
    heads = range(DN_HEADS)
    subs = range(DN_CPS)
    chains = [(sc, h) for sc in subs for h in heads]
    span = lambda sc: slice(sc * CHUNK, (sc + 1) * CHUNK)

    def cumulative(g):
        g1, g2, g3 = _split3(g)
        return (jnp.dot(tri, g1, preferred_element_type=F32) + jnp.dot(tri, g2, preferred_element_type=F32)
                + jnp.dot(tri, g3, preferred_element_type=F32))

    gcum = [cumulative(g_all[span(sc)]) for sc in subs]
    gcum_t = [g.T for g in gcum]

    def l2n(x):
        return x * lax.rsqrt(jnp.sum(x * x, axis=-1, keepdims=True) + EPS)

    qs, ks, vs, betas, gcs, g_lasts, decays = {}, {}, {}, {}, {}, {}, {}
    for sc, h in chains:
        lo = h * DN_HEAD_DIM
        qs[sc, h] = l2n(act[span(sc), lo:lo + DN_HEAD_DIM]) * scale
        ks[sc, h] = l2n(act[span(sc), DN_WIDTH + lo:DN_WIDTH + lo + DN_HEAD_DIM])
        vs[sc, h] = act[span(sc), 2 * DN_WIDTH + lo:2 * DN_WIDTH + lo + DN_HEAD_DIM]
        betas[sc, h] = beta_all[span(sc), h:h + 1]
        gcs[sc, h] = gcum[sc][:, DN_HEADS + h:DN_HEADS + h + 1]
        g_lasts[sc, h] = gcum[sc][CHUNK - 1:CHUNK, DN_HEADS + h:DN_HEADS + h + 1]
        decays[sc, h] = jnp.where(incl, jnp.exp(jnp.where(
            incl, gcs[sc, h] - gcum_t[sc][DN_HEADS + h:DN_HEADS + h + 1, :], 0.0)), 0.0)
    kbs = {ch: ks[ch] * betas[ch] for ch in chains}

    kq = {ch: _mm_nt(jnp.concatenate([kbs[ch], qs[ch]], axis=0), ks[ch]) for ch in chains}
    ms = [jnp.where(strict, kq[ch][0:CHUNK] * decays[ch], 0.0) for ch in chains]
    attns = {ch: jnp.where(incl, kq[ch][CHUNK:2 * CHUNK] * decays[ch], 0.0) for ch in chains}
    ts = dict(zip(chains, _unit_lower_inverse(ms, r, c)))
    uw = {ch: _mm(ts[ch], jnp.concatenate([vs[ch] * betas[ch], kbs[ch] * jnp.exp(gcs[ch])], axis=1))
          for ch in chains}
    states = [s_ref[h] for h in heads]
    for sc in subs:
        ws = [_mm(jnp.concatenate([uw[sc, h][:, DN_HEAD_DIM:], qs[sc, h] * jnp.exp(gcs[sc, h])], axis=0),
                  states[h]) for h in heads]
        v_news = [uw[sc, h][:, 0:DN_HEAD_DIM] - ws[h][0:CHUNK] for h in heads]
        intra = [_mm(attns[sc, h], v_news[h]) for h in heads]
        upd = [_mm_tn(ks[sc, h] * jnp.exp(g_lasts[sc, h] - gcs[sc, h]), v_news[h]) for h in heads]
        states = [states[h] * jnp.exp(g_lasts[sc, h]) + upd[h] for h in heads]
        for h in heads:
            lo = h * DN_HEAD_DIM
            o = ws[h][CHUNK:2 * CHUNK] + intra[h]
            gt = gate_ref[span(sc), lo:lo + DN_HEAD_DIM]
            o_ref[span(sc), lo:lo + DN_HEAD_DIM] = (_rms(o, dnorm) * _silu(gt)).astype(o_ref.dtype)
    for h in heads:
        s_ref[h] = states[h]


def _deltanet(proj, ba, conv_w, dt_bias, a_log, dn_norm, batch, seq):
    n = batch * seq
    rows = DN_CPS * CHUNK
    nc = seq // rows
    wq = 3 * DN_WIDTH
    hp = jnp.zeros((SUBLANES, LANES), F32)
    hp = hp.at[0, DN_HEADS:2 * DN_HEADS].set(dt_bias).at[1, DN_HEADS:2 * DN_HEADS].set(a_log)
    row = lambda b, ci: b * nc + ci
    return pl.pallas_call(
        _deltanet_body,
        grid=(batch, nc),
        in_specs=[pl.BlockSpec((rows, wq), lambda b, ci: (row(b, ci), 0)),
                  pl.BlockSpec((rows, DN_WIDTH), lambda b, ci: (row(b, ci), wq // DN_WIDTH)),
                  pl.BlockSpec((rows, LANES), lambda b, ci: (row(b, ci), 0)),
                  pl.BlockSpec((CONV_K, wq), lambda b, ci: (0, 0)),
                  pl.BlockSpec((SUBLANES, LANES), lambda b, ci: (0, 0)),
                  pl.BlockSpec((1, DN_HEAD_DIM), lambda b, ci: (0, 0))],
        out_specs=pl.BlockSpec((rows, DN_WIDTH), lambda b, ci: (row(b, ci), 0)),
        out_shape=jax.ShapeDtypeStruct((n, DN_WIDTH), BF16),
        scratch_shapes=[pltpu.VMEM((rows + HALO, wq), F32),
                        pltpu.VMEM((DN_HEADS, DN_HEAD_DIM, DN_HEAD_DIM), F32)],
        compiler_params=_cparams(("parallel", "arbitrary")),
    )(proj, proj, ba, conv_w, hp, dn_norm.reshape(1, DN_HEAD_DIM))


def _diffattn_body(slopes_ref, q_ref, k_ref, vt_ref, lamv_ref, danorm_ref, o_ref,
                   m_ref, l_ref, a_ref, acc_ref, nrc_ref, dbias_ref, za_ref, zb_ref, zmaxa_ref, zmaxb_ref,
                   p_ref, *, tq, lam_init):
    slope = slopes_ref[pl.program_id(1)]
    nq = q_ref.shape[0] // tq
    cols = 2 * tq

    kr = lax.broadcasted_iota(jnp.int32, (tq, cols), 0)
    qc = lax.broadcasted_iota(jnp.int32, (tq, cols), 1) & (tq - 1)
    rc = (qc - kr).astype(F32)
    nrc_ref[...] = -slope * rc
    dbias_ref[...] = jnp.where((kr >> 6) <= (qc >> 6), -slope * jnp.abs(rc), -jnp.inf)

    lv = lamv_ref[...]
    lam = (jnp.exp(jnp.sum(lv[0:1] * lv[1:2], axis=-1, keepdims=True))
           - jnp.exp(jnp.sum(lv[2:3] * lv[3:4], axis=-1, keepdims=True)) + lam_init)

    def rows(j):
        return pl.ds(pl.multiple_of(j * tq, tq), tq)

    def query_block(qi, carry):
        q = q_ref[rows(qi), :]
        lane = lax.broadcasted_iota(jnp.int32, q.shape, 1)
        zero = jnp.zeros_like(q)
        qq = jnp.concatenate([jnp.where(lane < DA_HEAD_DIM, q, zero),
                              jnp.where(lane >= DA_HEAD_DIM, q, zero)], axis=0)

        def pv_update(j):
            acc_ref[...] = a_ref[...] * acc_ref[...] + jnp.dot(vt_ref[j], p_ref[...],
                                                               preferred_element_type=F32)

        def biased(j, bias_ref):
            z = _mm_nt(k_ref[rows(j), :], qq) + bias_ref[...]
            return z, jnp.max(z, axis=0, keepdims=True)

        za_ref[...], zmaxa_ref[...] = biased(0, nrc_ref)
        z0, m0 = biased(qi, dbias_ref)
        p0 = jnp.exp(z0 - m0)
        m_ref[...] = m0
        l_ref[...] = jnp.sum(p0, axis=0, keepdims=True)
        acc_ref[...] = jnp.zeros_like(acc_ref)
        p_ref[...] = p0.astype(BF16)
        a_ref[...] = jnp.ones_like(a_ref)

        def step(kj, zin_ref, zmaxin_ref, zout_ref, zmaxout_ref):
            pv_update(jnp.where(kj == 0, qi, kj - 1))
            zout_ref[...], zmaxout_ref[...] = biased(jnp.minimum(kj + 1, qi), nrc_ref)
            shift = slope * lax.convert_element_type((qi - kj) * tq, F32)
            m_old = m_ref[...]
            m_new = jnp.maximum(m_old, zmaxin_ref[...] - shift)
            p = jnp.exp(zin_ref[...] - (m_new + shift))
            scale = jnp.exp(m_old - m_new)
            l_ref[...] = scale * l_ref[...] + jnp.sum(p, axis=0, keepdims=True)
            m_ref[...] = m_new
            p_ref[...] = p.astype(BF16)
            a_ref[...] = scale

        def pair(i, carry):
            step(2 * i, za_ref, zmaxa_ref, zb_ref, zmaxb_ref)
            step(2 * i + 1, zb_ref, zmaxb_ref, za_ref, zmaxa_ref)
            return carry

        lax.fori_loop(0, qi >> 1, pair, 0)

        @pl.when((qi & 1) == 1)
        def _():
            step(qi - 1, za_ref, zmaxa_ref, zb_ref, zmaxb_ref)

        pv_update(jnp.where(qi == 0, qi, qi - 1))

        o1 = acc_ref[:, 0:tq] / l_ref[:, 0:tq]
        o2 = acc_ref[:, tq:2 * tq] / l_ref[:, tq:2 * tq]
        o = o1 - lam * o2
        o = o * lax.rsqrt(jnp.mean(o * o, axis=0, keepdims=True) + EPS) * danorm_ref[...]
        o_ref[rows(qi), :] = (o * (1.0 - lam_init)).T.astype(o_ref.dtype)
        return carry

    lax.fori_loop(0, nq, query_block, 0)


def _diffattn(qk, v_t, lam_vecs, da_norm, lam_init, batch, seq):
    n = batch * seq
    tq = v_t.shape[2]
    nq = seq // tq
    slopes = jnp.exp2(-8.0 * jnp.arange(1, DA_HEADS + 1, dtype=F32) / DA_HEADS)
    stat = pltpu.VMEM((1, 2 * tq), F32)
    tile = pltpu.VMEM((tq, 2 * tq), F32)
    grid_spec = pltpu.PrefetchScalarGridSpec(
        num_scalar_prefetch=1,
        grid=(batch, DA_HEADS),
        in_specs=[pl.BlockSpec((seq, DA_V_DIM), lambda b, h, sl: (b, h)),
                  pl.BlockSpec((seq, DA_V_DIM), lambda b, h, sl: (b, DA_HEADS + h)),
                  pl.BlockSpec((nq, DA_V_DIM, tq), lambda b, h, sl: (b, h, 0)),
                  pl.BlockSpec((4, DA_HEAD_DIM), lambda b, h, sl: (0, 0)),
                  pl.BlockSpec((DA_V_DIM, 1), lambda b, h, sl: (0, 0))],
        out_specs=pl.BlockSpec((seq, DA_V_DIM), lambda b, h, sl: (b, h)),
        scratch_shapes=[stat, stat, stat, pltpu.VMEM((DA_V_DIM, 2 * tq), F32), tile, tile, tile, tile,
                        stat, stat, pltpu.VMEM((tq, 2 * tq), BF16)],
    )
    return pl.pallas_call(
        functools.partial(_diffattn_body, tq=tq, lam_init=lam_init),
        grid_spec=grid_spec,
        out_shape=jax.ShapeDtypeStruct((n, DA_WIDTH), BF16),
        compiler_params=_cparams(("parallel", "parallel")),
    )(slopes, qk, qk, v_t, lam_vecs, da_norm.reshape(DA_V_DIM, 1))


def _attn_proj_body(x_ref, g_ref, wqk_ref, wvt_ref, qk_ref, vt_ref, *, tq):
    xn = _rms(x_ref[...], g_ref[...]).astype(BF16)
    qk_ref[...] = jnp.dot(xn, wqk_ref[...], preferred_element_type=F32).astype(qk_ref.dtype)
    for j in range(vt_ref.shape[0]):
        vt_ref[j] = lax.dot_general(wvt_ref[...], xn[j * tq:(j + 1) * tq], (((1,), (1,)), ((), ())),
                                    preferred_element_type=F32).astype(vt_ref.dtype)


def _attn_proj(x, gain, w_qk, w_vt, tq):
    n, d = x.shape
    tm = min(ROW_TM, n)
    nb = tm // tq
    return pl.pallas_call(
        functools.partial(_attn_proj_body, tq=tq),
        grid=(n // tm,),
        in_specs=[pl.BlockSpec((tm, d), lambda i: (i, 0)),
                  pl.BlockSpec((1, d), lambda i: (0, 0)),
                  pl.BlockSpec((d, 2 * DA_WIDTH), lambda i: (0, 0)),
                  pl.BlockSpec((DA_WIDTH, d), lambda i: (0, 0))],
        out_specs=[pl.BlockSpec((tm, 2 * DA_WIDTH), lambda i: (i, 0)),
                   pl.BlockSpec((nb, DA_WIDTH, tq), lambda i: (i, 0, 0))],
        out_shape=[jax.ShapeDtypeStruct((n, 2 * DA_WIDTH), BF16),
                   jax.ShapeDtypeStruct((n // tq, DA_WIDTH, tq), BF16)],
        compiler_params=_cparams(("parallel",)),
    )(x, gain.reshape(1, d), w_qk, w_vt)


def _top2_route(x, g_ref, wt_ref, idx_ref, gate_ref):
    xn = _rms(x, g_ref[...])
    x1, x2, x3 = _split3(xn)
    w1, w2, w3 = _split3(wt_ref[...])
    nt = lambda a, b: lax.dot_general(a, b, (((1,), (1,)), ((), ())), preferred_element_type=F32)
    logits = (nt(w3, x1) + nt(w2, x2) + nt(w1, x3)) + (nt(w2, x1) + nt(w1, x2)) + nt(w1, x1)
    e = lax.broadcasted_iota(jnp.int32, logits.shape, 0)
    m0 = jnp.max(logits, axis=0, keepdims=True)
    i0 = jnp.min(jnp.where(logits == m0, e, N_EXPERTS), axis=0, keepdims=True)
    rest = jnp.where(e == i0, -jnp.inf, logits)
    m1 = jnp.max(rest, axis=0, keepdims=True)
    i1 = jnp.min(jnp.where(rest == m1, e, N_EXPERTS), axis=0, keepdims=True)
    t = jnp.exp(m1 - m0)
    idx_ref[0:1, :] = i0
    idx_ref[1:2, :] = i1
    gate_ref[0:1, :] = 1.0 / (1.0 + t)
    gate_ref[1:2, :] = t / (1.0 + t)


def _outproj_body(h_ref, odn_ref, oda_ref, w_ref, *rest, route):
    out = (h_ref[...]
           + jnp.dot(odn_ref[...], w_ref[0:DN_WIDTH, :], preferred_element_type=F32)
           + jnp.dot(oda_ref[...], w_ref[DN_WIDTH:DN_WIDTH + DA_WIDTH, :], preferred_element_type=F32))
    if route:
        g_ref, wt_ref, o_ref, idx_ref, gate_ref = rest
        _top2_route(out, g_ref, wt_ref, idx_ref, gate_ref)
    else:
        o_ref, = rest
    o_ref[...] = out


def _outproj(h, o_dn, o_da, w_out, router=None):
    n, d = h.shape
    tm = min(ROW_TM, n)
    in_specs = [pl.BlockSpec((tm, d), lambda i: (i, 0)),
                pl.BlockSpec((tm, DN_WIDTH), lambda i: (i, 0)),
                pl.BlockSpec((tm, DA_WIDTH), lambda i: (i, 0)),
                pl.BlockSpec((DN_WIDTH + DA_WIDTH, d), lambda i: (0, 0))]
    args = [h, o_dn, o_da, w_out]
    out_specs = [pl.BlockSpec((tm, d), lambda i: (i, 0))]
    out_shape = [jax.ShapeDtypeStruct((n, d), F32)]
    if router is not None:
        gain, w_router = router
        in_specs += [pl.BlockSpec((1, d), lambda i: (0, 0)), pl.BlockSpec((N_EXPERTS, d), lambda i: (0, 0))]
        args += [gain.reshape(1, d), w_router.T]
        out_specs += [pl.BlockSpec((TOP_K, tm), lambda i: (0, i))] * 2
        out_shape += [jax.ShapeDtypeStruct((TOP_K, n), jnp.int32), jax.ShapeDtypeStruct((TOP_K, n), F32)]
    res = pl.pallas_call(
        functools.partial(_outproj_body, route=router is not None),
        grid=(n // tm,),
        in_specs=in_specs,
        out_specs=out_specs,
        out_shape=out_shape,
        compiler_params=_cparams(("parallel",)),
    )(*args)
    return res if router is not None else res[0]


def _ffn_body(x_ref, g_ref, w1_ref, w3_ref, w2_ref, o_ref, xn_ref):
    @pl.when(pl.program_id(1) == 0)
    def _():
        x = x_ref[...]
        xn_ref[...] = _rms(x, g_ref[...]).astype(BF16)
        o_ref[...] = x

    xn = xn_ref[...]
    g = jnp.dot(xn, w1_ref[...], preferred_element_type=F32)
    u = jnp.dot(xn, w3_ref[...], preferred_element_type=F32)
    a = (_silu(g) * u).astype(BF16)
    o_ref[...] += jnp.dot(a, w2_ref[...], preferred_element_type=F32)


def _ffn(x, gain, w1, w3, w2):
    n, d = x.shape
    dff = w1.shape[1]
    tm = min(FFN_TM, n)
    tf = FFN_TF
    return pl.pallas_call(
        _ffn_body,
        grid=(n // tm, dff // tf),
        in_specs=[pl.BlockSpec((tm, d), lambda i, f: (i, 0)),
                  pl.BlockSpec((1, d), lambda i, f: (0, 0)),
                  pl.BlockSpec((d, tf), lambda i, f: (0, f)),
                  pl.BlockSpec((d, tf), lambda i, f: (0, f)),
                  pl.BlockSpec((tf, d), lambda i, f: (f, 0))],
        out_specs=pl.BlockSpec((tm, d), lambda i, f: (i, 0)),
        out_shape=jax.ShapeDtypeStruct((n, d), F32),
        scratch_shapes=[pltpu.VMEM((tm, d), BF16)],
        compiler_params=_cparams(("parallel", "arbitrary")),
    )(x, gain.reshape(1, d), w1, w3, w2)


def _experts_body(be_ref, tok_ref, dst_ref, nb_ref, x_hbm, g_ref, gate_ref, w1_ref, w3_ref, w2_ref,
                  y_hbm, xg_ref, xn_ref, acc_ref, gsem, ssem, *, tm, nf):
    b = pl.program_id(0)
    f = pl.program_id(1)
    nb = nb_ref[0]
    slot = b & 1
    other = 1 - slot
    per_step, extra = divmod(tm, nf)

    def gather(blk, row, buf):
        return pltpu.make_async_copy(x_hbm.at[pl.ds(tok_ref[blk * tm + row], 1), :],
                                     xg_ref.at[buf, pl.ds(row, 1), :], gsem.at[buf])

    def scatter(blk, row, buf):
        return pltpu.make_async_copy(acc_ref.at[buf, pl.ds(row, 1), :],
                                     y_hbm.at[pl.ds(dst_ref[(blk + 1) * tm + row], 1), :], ssem.at[buf])

    def for_rows(fn):
        def step(i, carry):
            fn(i)
            return carry
        lax.fori_loop(0, tm, step, 0)

    def wait_gather(buf):
        pltpu.make_async_copy(x_hbm.at[pl.ds(0, tm), :], xg_ref.at[buf], gsem.at[buf]).wait()

    def wait_scatter(buf):
        pltpu.make_async_copy(acc_ref.at[buf], y_hbm.at[pl.ds(0, tm), :], ssem.at[buf]).wait()

    used = b < nb

    @pl.when(jnp.logical_and(f == 0, b == 0))
    def _():
        for_rows(lambda i: gather(0, i, 0).start())
        acc_ref[1] = jnp.zeros((tm, acc_ref.shape[2]), F32)

    @pl.when(jnp.logical_and(f == 0, b <= nb))
    def _():
        wait_gather(slot)

    @pl.when(jnp.logical_and(f == 0, jnp.logical_and(b <= nb, b >= 1)))
    def _():
        wait_scatter(slot)

    @pl.when(jnp.logical_and(f == 0, used))
    def _():
        xn_ref[...] = _rms(xg_ref[slot], g_ref[...]).astype(BF16)
        acc_ref[slot] = jnp.zeros((tm, acc_ref.shape[2]), F32)

    base = f * per_step + jnp.minimum(f, extra)

    @pl.when(used)
    def _():
        for t in range(per_step):
            gather(b + 1, base + t, other).start()
            scatter(b - 1, base + t, other).start()
        xn = xn_ref[...]
        g = jnp.dot(xn, w1_ref[...], preferred_element_type=F32)
        u = jnp.dot(xn, w3_ref[...], preferred_element_type=F32)
        a = (_silu(g) * u).astype(BF16)
        acc_ref[slot] += jnp.dot(a, w2_ref[...], preferred_element_type=F32)

    @pl.when(jnp.logical_and(f < extra, used))
    def _():
        gather(b + 1, base + per_step, other).start()
        scatter(b - 1, base + per_step, other).start()

    @pl.when(jnp.logical_and(f == nf - 1, used))
    def _():
        acc_ref[slot] = acc_ref[slot] * gate_ref[...]

    @pl.when(jnp.logical_and(f == 0, b == nb))
    def _():
        for_rows(lambda i: scatter(b - 1, i, other).start())
        wait_scatter(other)

    @pl.when(jnp.logical_and(f == 0, b >= nb))
    def _():
        acc_ref[0] = jnp.zeros((tm, acc_ref.shape[2]), F32)
        for_rows(lambda i: scatter(b, i, 0).start())
        wait_scatter(0)


def _experts(x, gain, w1, w3, w2, plan, tm):
    block_e, row_tok, row_dst, row_gate, n_used, n_grid = plan
    n, d = x.shape
    dff = w1.shape[2]
    tf = MOE_TF
    nf = dff // tf

    def widx(b, f, be, tok, dst, nb):
        live = b < nb[0]
        last = jnp.maximum(nb[0] - 1, 0)
        return jnp.where(live, be[b], be[last]), jnp.where(live, f, nf - 1)

    def w13_map(b, f, be, tok, dst, nb):
        e, ff = widx(b, f, be, tok, dst, nb)
        return e, 0, ff

    def w2_map(b, f, be, tok, dst, nb):
        e, ff = widx(b, f, be, tok, dst, nb)
        return e, ff, 0

    grid_spec = pltpu.PrefetchScalarGridSpec(
        num_scalar_prefetch=4,
        grid=(n_grid, nf),
        in_specs=[pl.BlockSpec(memory_space=pl.ANY),
                  pl.BlockSpec((1, d), lambda b, f, *_: (0, 0)),
                  pl.BlockSpec((tm, 1), lambda b, f, *_: (b, 0)),
                  pl.BlockSpec((None, d, tf), w13_map),
                  pl.BlockSpec((None, d, tf), w13_map),
                  pl.BlockSpec((None, tf, d), w2_map)],
        out_specs=pl.BlockSpec(memory_space=pl.ANY),
        scratch_shapes=[pltpu.VMEM((2, tm, d), F32), pltpu.VMEM((tm, d), BF16),
                        pltpu.VMEM((2, tm, d), F32),
                        pltpu.SemaphoreType.DMA((2,)), pltpu.SemaphoreType.DMA((2,))],
    )
    n_y = (n_grid + 1) * tm
    return pl.pallas_call(
        functools.partial(_experts_body, tm=tm, nf=nf),
        grid_spec=grid_spec,
        out_shape=jax.ShapeDtypeStruct((n_y, d), F32),
        compiler_params=_cparams(("arbitrary", "arbitrary")),
    )(block_e, row_tok, row_dst, n_used, x, gain.reshape(1, d), row_gate, w1, w3, w2)


def _route_plan(idx, gates, n, tm):
    n_assign = n * TOP_K
    flat_e = idx.T.reshape(n_assign)
    onehot = (flat_e[:, None] == jnp.arange(N_EXPERTS, dtype=jnp.int32)[None, :]).astype(jnp.int32)
    csum = jnp.cumsum(onehot, axis=0)
    counts = csum[-1]
    rank = jnp.sum((csum - onehot) * onehot, axis=1)
    padded = (counts + tm - 1) // tm * tm
    pad_end = jnp.cumsum(padded)
    pad_start = pad_end - padded
    dest = pad_start[flat_e] + rank
    n_blocks = -(-n_assign // tm) + N_EXPERTS
    n_grid = n_blocks + 1
    n_rows = n_grid * tm
    row_a = jnp.full((n_rows + tm,), -1, jnp.int32).at[tm + dest].set(
        jnp.arange(n_assign, dtype=jnp.int32))
    is_pad = row_a < 0
    a = jnp.maximum(row_a, 0)
    row_tok = (a // TOP_K)[tm:]
    row_gate = jnp.where(is_pad, 0.0, gates.T.reshape(n_assign)[a])[tm:]
    row_dst = jnp.where(is_pad, n_assign + jnp.cumsum(is_pad.astype(jnp.int32)) - 1,
                        (a % TOP_K) * n + a // TOP_K)
    block_e = jnp.minimum(
        jnp.searchsorted(pad_end, jnp.arange(n_grid, dtype=jnp.int32) * tm, side='right'),
        N_EXPERTS - 1).astype(jnp.int32)
    n_used = (pad_end[-1] // tm).astype(jnp.int32).reshape(1)
    return block_e, row_tok, row_dst, row_gate.reshape(n_rows, 1), n_used, n_grid


def _ple_body(*refs, final, combine):
    if combine:
        h_ref, y0_ref, y1_ref, p_ref, wp_ref, pn_ref, wg_ref, nf_ref, o_ref = refs
        h = h_ref[...] + (y0_ref[...] + y1_ref[...])
    else:
        h_ref, p_ref, wp_ref, pn_ref, wg_ref, nf_ref, o_ref = refs
        h = h_ref[...]
    e = _rms(_mm(p_ref[...], wp_ref[...]), pn_ref[...])
    out = h + _sigmoid(_mm(h, wg_ref[...])) * e
    if final:
        out = _rms(out, nf_ref[...])
    o_ref[...] = out


def _ple(h, p, w_proj, p_norm, w_gate, norm_final, final, y=None):
    n, d = h.shape
    dp = p.shape[1]
    tm = min(ROW_TM, n)
    row = lambda i: (i, 0)
    fixed = lambda i: (0, 0)
    specs = [pl.BlockSpec((tm, d), row)]
    args = [h]
    if y is not None:
        specs += [pl.BlockSpec((tm, d), row), pl.BlockSpec((tm, d), lambda i: (n // tm + i, 0))]
        args += [y, y]
    specs += [pl.BlockSpec((tm, dp), row), pl.BlockSpec((dp, d), fixed), pl.BlockSpec((1, d), fixed),
              pl.BlockSpec((d, d), fixed), pl.BlockSpec((1, d), fixed)]
    args += [p, w_proj, p_norm.reshape(1, d), w_gate, norm_final.reshape(1, d)]
    return pl.pallas_call(
        functools.partial(_ple_body, final=final, combine=y is not None),
        grid=(n // tm,),
        in_specs=specs,
        out_specs=pl.BlockSpec((tm, d), row),
        out_shape=jax.ShapeDtypeStruct((n, d), F32),
        compiler_params=_cparams(("parallel",)),
    )(*args)


def kernel(x, p, norm_mix, w_in, conv_w, dn_a_log, dn_dt_bias, dn_norm, da_lambda_q1, da_lambda_k1, da_lambda_q2, da_lambda_k2, da_norm, w_out, norm_ffn, ffn_w1, ffn_w3, ffn_w2, moe_router, moe_w1, moe_w3, moe_w2, ple_proj, ple_norm, ple_gate, norm_final):
    batch, seq, d = x.shape
    n = batch * seq
    depth = w_in.shape[0]
    h = x.reshape(n, d)
    wq = 3 * DN_WIDTH
    for i in range(depth):
        wi = w_in[i]
        c0 = wq + DN_WIDTH
        c1 = c0 + 2 * DN_HEADS
        w_dn = wi[:, :c0].astype(BF16)
        w_ba = jnp.pad(wi[:, c0:c1], ((0, 0), (0, LANES - 2 * DN_HEADS))).astype(BF16)
        att_scale = DA_HEAD_DIM ** -0.5
        w_qk = jnp.concatenate([wi[:, c1:c1 + DA_WIDTH] * att_scale,
                                wi[:, c1 + DA_WIDTH:c1 + 2 * DA_WIDTH]], axis=1).astype(BF16)
        w_vt = wi[:, c1 + 2 * DA_WIDTH:].T.astype(BF16)
        proj_dn, ba = _dn_proj(h, norm_mix[i], w_dn, w_ba, DN_WIDTH)
        qk, v_t = _attn_proj(h, norm_mix[i], w_qk, w_vt, min(ATT_TQ, seq))

        o_dn = _deltanet(proj_dn, ba, conv_w[i], dn_dt_bias[i], dn_a_log[i], dn_norm[i], batch, seq)
        lam_init = 0.8 - 0.6 * math.exp(-0.3 * i)
        lam_vecs = jnp.stack([da_lambda_q1[i], da_lambda_k1[i], da_lambda_q2[i], da_lambda_k2[i]])
        o_da = _diffattn(qk, v_t, lam_vecs, da_norm[i], lam_init, batch, seq)
        j = i // 2
        moe = i % 2 == 1
        res = _outproj(h, o_dn, o_da, w_out[i].astype(BF16),
                       router=(norm_ffn[i], moe_router[j]) if moe else None)
        h, idx, gates = res if moe else (res, None, None)

        last = i == depth - 1
        wp = ple_proj[i].astype(BF16)
        wg = ple_gate[i].astype(BF16)
        if i % 2 == 0:
            h = _ffn(h, norm_ffn[i], ffn_w1[j].astype(BF16), ffn_w3[j].astype(BF16),
                     ffn_w2[j].astype(BF16))
            h = _ple(h, p[i].reshape(n, -1), wp, ple_norm[i], wg, norm_final, last)
        else:
            tm = min(MOE_TM, n)
            plan = _route_plan(idx, gates, n, tm)
            y = _experts(h, norm_ffn[i], moe_w1[j].astype(BF16), moe_w3[j].astype(BF16),
                         moe_w2[j].astype(BF16), plan, tm)
            h = _ple(h, p[i].reshape(n, -1), wp, ple_norm[i], wg, norm_final, last, y=y)
    return h.reshape(batch, seq, d)
```
